```python
import math
import jax, jax.numpy as jnp
from jax import lax
import numpy as np

D_MODEL = 1024
BATCH = 2
SEQ = 8192
DEPTH = 4
DEC_BATCH = 32
DEC_SEQ = 8
PAST_LEN = 8192
PAGE_SIZE = 128

N_EVEN = (DEPTH + 1) // 2
N_ODD = DEPTH // 2
H_A = 8
DH_A = 64
FOX_W = H_A * DH_A
Q_BLOCK = 128
H_B = 8
DH_B = 64
RW_W = H_B * DH_B
R_W = 64
R_A = 64
R_V = 32
R_G = 128
RW_GN_EPS = 64e-5
H_C = 8
DK_C = 128
DV_C = 128
GDN_K = H_C * DK_C
GDN_V = H_C * DV_C
GDN_QKV = 2 * GDN_K + GDN_V
CONV_C = 4
GDN_CHUNK = 64
D_FF = 2816
CONV_F = 3
EPS = 1e-6

FOX_COLS = 4 * FOX_W + H_A
RW_COLS = 3 * RW_W + R_W + R_A + R_G
EV_IN = FOX_COLS + RW_COLS
EV_MIX = FOX_W + RW_W
OD_IN = GDN_QKV + GDN_V + 2 * H_C

kernel_name = 'fox_rwkv7_gdn_convffn_hybrid_step'

F32 = jnp.float32


def rmsnorm(x, g, eps=EPS):
    xf = x.astype(F32)
    y = xf * lax.rsqrt(jnp.mean(xf * xf, axis=-1, keepdims=True) + eps)
    return (y * g.astype(F32)).astype(x.dtype)


def l2norm(x, eps=1e-6):
    xf = x.astype(F32)
    return xf * lax.rsqrt(jnp.sum(xf * xf, axis=-1, keepdims=True) + eps)


def split_cols(p, sizes):
    offs = [int(o) for o in np.cumsum(sizes)[:-1]]
    return jnp.split(p, offs, axis=-1)


def causal_dwconv(prev, x, w):
    width = w.shape[0]
    T = x.shape[1]
    xx = jnp.concatenate([prev.astype(x.dtype), x], axis=1)
    y = xx[:, 0:T] * w[0]
    for i in range(1, width):
        y = y + xx[:, i:i + T] * w[i]
    return y, xx[:, T:]


def fox_attend_prompt(q, k, v, logf):
    B, T, H, Dh = q.shape
    n_blk = T // Q_BLOCK
    scale = Dh ** -0.5
    Fk = jnp.cumsum(logf, axis=1).transpose(0, 2, 1)
    qb = q.reshape(B, n_blk, Q_BLOCK, H, Dh).transpose(1, 0, 2, 3, 4)
    Fqb = Fk.reshape(B, H, n_blk, Q_BLOCK).transpose(2, 0, 1, 3)
    kpos = jnp.arange(T)

    def one_block(args):
        qi, Fq, blk = args
        s = jnp.einsum('bqhd,bkhd->bhqk', qi, k, preferred_element_type=F32) * scale
        s = s + Fq[..., :, None] - Fk[..., None, :]
        qpos = blk * Q_BLOCK + jnp.arange(Q_BLOCK)
        s = jnp.where(kpos[None, :] <= qpos[:, None], s, -jnp.inf)
        p = jax.nn.softmax(s, axis=-1)
        return jnp.einsum('bhqk,bkhd->bqhd', p.astype(v.dtype), v)

    o = lax.map(one_block, (qb, Fqb, jnp.arange(n_blk)))
    return o.transpose(1, 0, 2, 3, 4).reshape(B, T, H, Dh)


def fox_attend_sample(q, k, v, logf, pool_k, pool_v, pool_lf, page_table):
    B, T, H, Dh = q.shape
    k_past = pool_k[page_table].reshape(B, -1, H, Dh)
    v_past = pool_v[page_table].reshape(B, -1, H, Dh)
    lf_past = pool_lf[page_table].reshape(B, -1, H)
    P = k_past.shape[1]
    k_all = jnp.concatenate([k_past.astype(k.dtype), k], axis=1)
    v_all = jnp.concatenate([v_past.astype(v.dtype), v], axis=1)
    Fk = jnp.cumsum(jnp.concatenate([lf_past.astype(F32), logf], axis=1), axis=1).transpose(0, 2, 1)
    Fq = Fk[..., P:]
    s = jnp.einsum('bqhd,bkhd->bhqk', q, k_all, preferred_element_type=F32) * (Dh ** -0.5)
    s = s + Fq[..., :, None] - Fk[..., None, :]
    mask = jnp.arange(P + T)[None, :] <= (P + jnp.arange(T))[:, None]
    p = jax.nn.softmax(jnp.where(mask, s, -jnp.inf), axis=-1)
    return jnp.einsum('bhqk,bkhd->bqhd', p.astype(v_all.dtype), v_all)


def rwkv7_scan(S0, r, w, k, v, a, b):
    def step(S, inp):
        rt, wt, kt, vt, at, bt = inp
        sa = jnp.einsum('bhij,bhj->bhi', S, at)
        S = S * wt[:, :, None, :] + sa[..., None] * bt[:, :, None, :] + vt[..., None] * kt[:, :, None, :]
        return S, jnp.einsum('bhij,bhj->bhi', S, rt)
    xs = tuple(t.transpose(1, 0, 2, 3) for t in (r, w, k, v, a, b))
    S, y = lax.scan(step, S0, xs)
    return y.transpose(1, 0, 2, 3), S


def rwkv7_mix(p, shift_prev, S0, v_first, e, W):
    B, T, _ = p.shape
    p_prev = jnp.concatenate([shift_prev[:, None, :].astype(p.dtype), p[:, :-1]], axis=1)
    ps = p + (p_prev - p) * W['rw_mu'][e]
    r, k, v, wd, ad, gd = split_cols(ps, [RW_W, RW_W, RW_W, R_W, R_A, R_G])
    w_log = -jax.nn.softplus(-(W['rw_w0'][e] + jnp.tanh(wd) @ W['rw_w2'][e]).astype(F32)) - 0.5
    decay = jnp.exp(-jnp.exp(w_log))
    a = jax.nn.sigmoid((W['rw_a0'][e] + ad @ W['rw_a2'][e]).astype(F32))
    g = jax.nn.sigmoid(gd) @ W['rw_g2'][e]
    if e == 0:
        v_first = v
    else:
        gate = jax.nn.sigmoid(W['rw_v0'][e - 1] + (v @ W['rw_v1'][e - 1]) @ W['rw_v2'][e - 1])
        v = v + (v_first - v) * gate
    hs = (B, T, H_B, DH_B)
    hw = (H_B, DH_B)
    rf = r.astype(F32).reshape(hs)
    kf = k.astype(F32).reshape(hs)
    vf = v.astype(F32).reshape(hs)
    af = a.reshape(hs)
    kk = l2norm(kf * W['rw_k_k'][e].reshape(hw))
    kf = kf * (1.0 + (af - 1.0) * W['rw_k_a'][e].reshape(hw))
    y, S = rwkv7_scan(S0.astype(F32), rf, decay.reshape(hs), kf, vf, -kk, kk * af)
    mu = jnp.mean(y, axis=-1, keepdims=True)
    var = jnp.mean(jnp.square(y - mu), axis=-1, keepdims=True)
    y = (y - mu) * lax.rsqrt(var + RW_GN_EPS)
    y = y * W['rw_ln_w'][e].reshape(hw) + W['rw_ln_b'][e].reshape(hw)
    y = y + jnp.sum(rf * kf * W['rw_r_k'][e], axis=-1, keepdims=True) * vf
    y = y.reshape(B, T, RW_W).astype(p.dtype) * g
    return y, p[:, -1], S, v_first


def gated_delta_chunked(q, k, v, g, beta, S0):
    B, T, H, Dk = q.shape
    Dv = v.shape[-1]
    C = math.gcd(T, GDN_CHUNK)
    N = T // C

    def chunks(t):
        return t.reshape(B, N, C, H, -1).transpose(1, 0, 3, 2, 4)
    qc, kc, vc = chunks(q), chunks(k), chunks(v)
    gc = g.reshape(B, N, C, H).transpose(1, 0, 3, 2)
    bc = beta.reshape(B, N, C, H).transpose(1, 0, 3, 2)
    G = jnp.cumsum(gc, axis=-1)
    tril = jnp.tril(jnp.ones((C, C), bool))
    strict = jnp.tril(jnp.ones((C, C), bool), -1)
    diff = G[..., :, None] - G[..., None, :]
    gamma = jnp.where(tril, jnp.exp(jnp.where(tril, diff, 0.0)), 0.0)
    kb = kc * bc[..., None]
    M = jnp.where(strict, jnp.einsum('nbhid,nbhjd->nbhij', kb, kc) * gamma, 0.0)
    eye = jnp.eye(C, dtype=F32)
    Tinv = lax.linalg.triangular_solve(eye + M, jnp.broadcast_to(eye, M.shape), left_side=True, lower=True)
    u = Tinv @ (vc * bc[..., None])
    wk = Tinv @ (kb * jnp.exp(G)[..., None])
    A = jnp.where(tril, jnp.einsum('nbhid,nbhjd->nbhij', qc, kc) * gamma, 0.0)

    def step(S, inp):
        qi, ki, ui, wi, Gi, Ai = inp
        v_new = ui - wi @ S
        o = (qi * jnp.exp(Gi)[..., None]) @ S + Ai @ v_new
        Gl = Gi[..., -1:]
        S = S * jnp.exp(Gl)[..., None] + jnp.einsum('bhcd,bhce->bhde', ki * jnp.exp(Gl - Gi)[..., None], v_new)
        return S, o
    S, o = lax.scan(step, S0, (qc, kc, u, wk, G, A))
    return o.transpose(1, 0, 3, 2, 4).reshape(B, T, H, Dv), S


def gdn_mix(p, conv_prev, S0, o_idx, W):
    B, T, _ = p.shape
    qkv, z, bl, al = split_cols(p, [GDN_QKV, GDN_V, H_C, H_C])
    qkv, new_conv = causal_dwconv(conv_prev, qkv, W['gdn_conv_w'][o_idx])
    qkv = jax.nn.silu(qkv)
    q, k, v = split_cols(qkv, [GDN_K, GDN_K, GDN_V])
    q = l2norm(q.reshape(B, T, H_C, DK_C)) * (DK_C ** -0.5)
    k = l2norm(k.reshape(B, T, H_C, DK_C))
    v = v.astype(F32).reshape(B, T, H_C, DV_C)
    beta = jax.nn.sigmoid(bl.astype(F32))
    g = -jnp.exp(W['gdn_A_log'][o_idx].astype(F32)) * jax.nn.softplus(al.astype(F32) + W['gdn_dt_bias'][o_idx])
    o, S = gated_delta_chunked(q, k, v, g, beta, S0.astype(F32))
    o = rmsnorm(o, W['gdn_norm_w'][o_idx]) * jax.nn.silu(z.astype(F32).reshape(B, T, H_C, DV_C))
    return o.reshape(B, T, GDN_V).astype(p.dtype), new_conv, S


def run_trunk(x, fox_past, rw_state, rw_shift, gdn_state, gdn_conv, ffn_conv, page_table, W):
    B, T, _ = x.shape
    fk, fv, flf, rwS, rwSh, gS, gCv, fCv = [], [], [], [], [], [], [], []
    v_first = None
    for layer in range(DEPTH):
        xn = rmsnorm(x, W['norm_mix'][layer])
        if layer % 2 == 0:
            e = layer // 2
            p = xn @ W['ev_w_in'][e]
            q, k, v, og, fl = split_cols(p[..., :FOX_COLS], [FOX_W, FOX_W, FOX_W, FOX_W, H_A])
            q = rmsnorm(q.reshape(B, T, H_A, DH_A), W['fox_q_gain'][e])
            k = rmsnorm(k.reshape(B, T, H_A, DH_A), W['fox_k_gain'][e])
            v = v.reshape(B, T, H_A, DH_A)
            logf = jax.nn.log_sigmoid(fl.astype(F32) + W['fox_b_f'][e])
            if fox_past is None:
                o = fox_attend_prompt(q, k, v, logf)
            else:
                o = fox_attend_sample(q, k, v, logf, fox_past[0][e], fox_past[1][e], fox_past[2][e], page_table)
            o_a = o.reshape(B, T, FOX_W) * jax.nn.sigmoid(og)
            o_b, sh, S, v_first = rwkv7_mix(p[..., FOX_COLS:], rw_shift[e], rw_state[e], v_first, e, W)
            mix = jnp.concatenate([o_a, o_b], axis=-1) @ W['ev_w_out'][e]
            fk.append(k)
            fv.append(v)
            flf.append(logf)
            rwS.append(S)
            rwSh.append(sh)
        else:
            o_idx = layer // 2
            p = xn @ W['od_w_in'][o_idx]
            o_c, cv, S = gdn_mix(p, gdn_conv[o_idx], gdn_state[o_idx], o_idx, W)
            mix = o_c @ W['od_w_out'][o_idx]
            gS.append(S)
            gCv.append(cv)
        x = x + mix
        xn = rmsnorm(x, W['norm_ffn'][layer])
        h = xn @ W['ffn_w_up'][layer]
        h, buf = causal_dwconv(ffn_conv[layer], h, W['ffn_conv_w'][layer])
        h = h + W['ffn_conv_b'][layer]
        u, gt = jnp.split(h, 2, axis=-1)
        x = x + (jax.nn.silu(gt) * u) @ W['ffn_w_down'][layer]
        fCv.append(buf)
    y = rmsnorm(x, W['norm_out'])
    return (y, jnp.stack(fk), jnp.stack(fv), jnp.stack(flf), jnp.stack(rwS), jnp.stack(rwSh),
            jnp.stack(gS), jnp.stack(gCv), jnp.stack(fCv))


def setup_inputs(seed: int = 0) -> dict:
    key = jax.random.key(seed)
    ks = iter(jax.random.split(key, 64))

    def nrm(shape, s=1.0):
        return s * jax.random.normal(next(ks), shape, F32)

    def uni(shape, lo, hi):
        return jax.random.uniform(next(ks), shape, F32, lo, hi)

    n_pages = PAST_LEN // PAGE_SIZE
    n_used = DEC_BATCH * n_pages
    n_pool = n_used + n_used // 4
    page_table = jax.random.permutation(next(ks), n_pool)[:n_used].reshape(DEC_BATCH, n_pages).astype(jnp.int32)
    dt = jnp.exp(uni((N_ODD, H_C), math.log(1e-3), math.log(1e-1)))
    d = D_MODEL
    return {
        'x_prompt': nrm((BATCH, SEQ, d)),
        'x_sample': nrm((DEC_BATCH, DEC_SEQ, d)),
        'cache_fox_k': nrm((N_EVEN, n_pool, PAGE_SIZE, H_A, DH_A)),
        'cache_fox_v': nrm((N_EVEN, n_pool, PAGE_SIZE, H_A, DH_A)),
        'cache_fox_logf': jax.nn.log_sigmoid(4.0 + nrm((N_EVEN, n_pool, PAGE_SIZE, H_A))),
        'state_rwkv': nrm((N_EVEN, DEC_BATCH, H_B, DH_B, DH_B), 0.5),
        'state_rwkv_shift': nrm((N_EVEN, DEC_BATCH, RW_COLS)),
        'state_gdn': nrm((N_ODD, DEC_BATCH, H_C, DK_C, DV_C), 0.1),
        'state_gdn_conv': nrm((N_ODD, DEC_BATCH, CONV_C - 1, GDN_QKV)),
        'state_ffn_conv': nrm((DEPTH, DEC_BATCH, CONV_F - 1, 2 * D_FF)),
        'page_table': page_table,
        'norm_mix': 1.0 + nrm((DEPTH, d), 0.05),
        'norm_ffn': 1.0 + nrm((DEPTH, d), 0.05),
        'norm_out': 1.0 + nrm((d,), 0.05),
        'ev_w_in': nrm((N_EVEN, d, EV_IN), d ** -0.5),
        'ev_w_out': nrm((N_EVEN, EV_MIX, d), EV_MIX ** -0.5),
        'fox_b_f': uni((N_EVEN, H_A), 2.0, 6.0),
        'fox_q_gain': 1.0 + nrm((N_EVEN, DH_A), 0.05),
        'fox_k_gain': 1.0 + nrm((N_EVEN, DH_A), 0.05),
        'rw_mu': uni((N_EVEN, RW_COLS), 0.0, 1.0),
        'rw_w0': uni((N_EVEN, RW_W), -6.0, -1.0),
        'rw_w2': nrm((N_EVEN, R_W, RW_W), 0.1 * R_W ** -0.5),
        'rw_a0': nrm((N_EVEN, RW_W), 0.1),
        'rw_a2': nrm((N_EVEN, R_A, RW_W), R_A ** -0.5),
        'rw_g2': nrm((N_EVEN, R_G, RW_W), R_G ** -0.5),
        'rw_k_k': 0.85 + nrm((N_EVEN, RW_W), 0.05),
        'rw_k_a': 1.0 + nrm((N_EVEN, RW_W), 0.05),
        'rw_r_k': nrm((N_EVEN, H_B, DH_B), 0.1),
        'rw_ln_w': 1.0 + nrm((N_EVEN, RW_W), 0.05),
        'rw_ln_b': nrm((N_EVEN, RW_W), 0.01),
        'rw_v0': 0.5 + nrm((N_EVEN - 1, RW_W), 0.1),
        'rw_v1': nrm((N_EVEN - 1, RW_W, R_V), RW_W ** -0.5),
        'rw_v2': nrm((N_EVEN - 1, R_V, RW_W), 0.1 * R_V ** -0.5),
        'od_w_in': nrm((N_ODD, d, OD_IN), d ** -0.5),
        'od_w_out': nrm((N_ODD, GDN_V, d), GDN_V ** -0.5),
        'gdn_conv_w': nrm((N_ODD, CONV_C, GDN_QKV), CONV_C ** -0.5),
        'gdn_A_log': jnp.log(uni((N_ODD, H_C), 1.0, 16.0)),
        'gdn_dt_bias': dt + jnp.log(-jnp.expm1(-dt)),
        'gdn_norm_w': 1.0 + nrm((N_ODD, DV_C), 0.05),
        'ffn_w_up': nrm((DEPTH, d, 2 * D_FF), d ** -0.5),
        'ffn_conv_w': nrm((DEPTH, CONV_F, 2 * D_FF), CONV_F ** -0.5),
        'ffn_conv_b': nrm((DEPTH, 2 * D_FF), 0.02),
        'ffn_w_down': nrm((DEPTH, D_FF, d), D_FF ** -0.5),
    }


def reference(x_prompt, x_sample, cache_fox_k, cache_fox_v, cache_fox_logf, state_rwkv, state_rwkv_shift,
              state_gdn, state_gdn_conv, state_ffn_conv, page_table, norm_mix, norm_ffn, norm_out,
              ev_w_in, ev_w_out, fox_b_f, fox_q_gain, fox_k_gain, rw_mu, rw_w0, rw_w2, rw_a0, rw_a2, rw_g2,
              rw_k_k, rw_k_a, rw_r_k, rw_ln_w, rw_ln_b, rw_v0, rw_v1, rw_v2, od_w_in, od_w_out, gdn_conv_w,
              gdn_A_log, gdn_dt_bias, gdn_norm_w, ffn_w_up, ffn_conv_w, ffn_conv_b, ffn_w_down):
    W = dict(norm_mix=norm_mix, norm_ffn=norm_ffn, norm_out=norm_out, ev_w_in=ev_w_in, ev_w_out=ev_w_out,
             fox_b_f=fox_b_f, fox_q_gain=fox_q_gain, fox_k_gain=fox_k_gain, rw_mu=rw_mu, rw_w0=rw_w0,
             rw_w2=rw_w2, rw_a0=rw_a0, rw_a2=rw_a2, rw_g2=rw_g2, rw_k_k=rw_k_k, rw_k_a=rw_k_a, rw_r_k=rw_r_k,
             rw_ln_w=rw_ln_w, rw_ln_b=rw_ln_b, rw_v0=rw_v0, rw_v1=rw_v1, rw_v2=rw_v2, od_w_in=od_w_in,
             od_w_out=od_w_out, gdn_conv_w=gdn_conv_w, gdn_A_log=gdn_A_log, gdn_dt_bias=gdn_dt_bias,
             gdn_norm_w=gdn_norm_w, ffn_w_up=ffn_w_up, ffn_conv_w=ffn_conv_w, ffn_conv_b=ffn_conv_b,
             ffn_w_down=ffn_w_down)
    Bp = x_prompt.shape[0]
    dt = x_prompt.dtype
    (y_prompt, fk_p, fv_p, flf_p, rw_p, rwsh_p, gdn_p, gcv_p, fcv_p) = run_trunk(
        x_prompt, None,
        jnp.zeros((N_EVEN, Bp, H_B, DH_B, DH_B), F32),
        jnp.zeros((N_EVEN, Bp, RW_COLS), dt),
        jnp.zeros((N_ODD, Bp, H_C, DK_C, DV_C), F32),
        jnp.zeros((N_ODD, Bp, CONV_C - 1, GDN_QKV), dt),
        jnp.zeros((DEPTH, Bp, CONV_F - 1, 2 * D_FF), dt),
        page_table, W)
    (y_sample, fk_s, fv_s, flf_s, rw_s, rwsh_s, gdn_s, gcv_s, fcv_s) = run_trunk(
        x_sample, (cache_fox_k, cache_fox_v, cache_fox_logf), state_rwkv, state_rwkv_shift,
        state_gdn, state_gdn_conv, state_ffn_conv, page_table, W)
    return (y_prompt, y_sample, fk_p, fv_p, flf_p, fk_s, fv_s, flf_s, rw_p, rw_s, rwsh_p, rwsh_s,
            gdn_p, gdn_s, gcv_p, gcv_s, fcv_p, fcv_s)
```

```python
import functools
import math

import jax
import jax.numpy as jnp
from jax import lax
from jax.experimental import pallas as pl
from jax.experimental.pallas import tpu as pltpu

F32 = jnp.float32
BF16 = jnp.bfloat16

EPS = 1e-6
RW_GN_EPS = 64e-5
L2_EPS = 1e-6
NEG_BIG = -1e30

H_A = 8
DH = 64
FOX_W = 512
RW_W = 512
H_C = 8
DK = 128
CHUNK = 64
LANES = 128
VMEM_LIMIT = 56 * 1024 * 1024


def _cparams(*sem):
    return pltpu.CompilerParams(dimension_semantics=sem, vmem_limit_bytes=VMEM_LIMIT)


def _const_spec(shape):
    nd = len(shape)
    return pl.BlockSpec(shape, lambda *_: (0,) * nd, pipeline_mode=pl.Buffered(1))


def _mm(a, b):
    return jnp.dot(a, b, preferred_element_type=F32)


def _mm_nt(a, b):
    return lax.dot_general(a, b, (((1,), (1,)), ((), ())), preferred_element_type=F32)


def _mm_tn(a, b):
    return lax.dot_general(a, b, (((0,), (0,)), ((), ())), preferred_element_type=F32)


def _split3(x):
    hi = x.astype(BF16)
    r = x - hi.astype(F32)
    mid = r.astype(BF16)
    lo = (r - mid.astype(F32)).astype(BF16)
    return hi, mid, lo


def _sel_l(m01, x):
    hi, mid, lo = _split3(x)
    return _mm(m01, hi) + _mm(m01, mid) + _mm(m01, lo)


def _sel_r(x, m01):
    hi, mid, lo = _split3(x)
    return _mm(hi, m01) + _mm(mid, m01) + _mm(lo, m01)


def _sel_nt(m01, x):
    hi, mid, lo = _split3(x)
    return _mm_nt(m01, hi) + _mm_nt(m01, mid) + _mm_nt(m01, lo)


def _mm3(a, b):
    ah = a.astype(BF16)
    al = (a - ah.astype(F32)).astype(BF16)
    bh = b.astype(BF16)
    bl = (b - bh.astype(F32)).astype(BF16)
    return _mm(ah, bh) + _mm(ah, bl) + _mm(al, bh)


def _log_sigmoid(z):
    return jnp.minimum(z, 0.0) - jnp.log1p(jnp.exp(-jnp.abs(z)))


def _sigmoid(z):
    return 1.0 / (1.0 + jnp.exp(-z))


def _inv_unit_lower(n):
    size = n.shape[0]
    r = lax.broadcasted_iota(jnp.int32, (size, size), 0)
    c = lax.broadcasted_iota(jnp.int32, (size, size), 1)
    p = jnp.where(r == c, 1.0, 0.0) + n
    nk = n
    for _ in range(int(math.log2(CHUNK)) - 1):
        nk = _mm3(nk, nk)
        p = p + _mm3(p, nk)
    return p


def _head_scale(x2, fn):
    lane_h = lax.broadcasted_iota(jnp.int32, (1, x2.shape[1]), 1) // DH
    out = jnp.zeros_like(x2)
    for h in range(x2.shape[1] // DH):
        s = jnp.sum(x2[:, h * DH:(h + 1) * DH], axis=-1, keepdims=True)
        out = jnp.where(lane_h == h, fn(s), out)
    return out


def _rms_heads(x, gain):
    return x * _head_scale(x * x, lambda s: lax.rsqrt(s * (1.0 / DH) + EPS)) * gain


def _shift_rows(x, k, prev):
    row = lax.broadcasted_iota(jnp.int32, x.shape, 0)
    out = pltpu.roll(x, k, 0)
    nprev = prev.shape[0]
    for j in range(k):
        out = jnp.where(row == j, prev[nprev - k + j:nprev - k + j + 1, :], out)
    return out


def _nm_body(x_ref, g_ref, w_ref, *o_refs, splits):
    x = x_ref[...]
    xn = (x * lax.rsqrt(jnp.mean(x * x, axis=-1, keepdims=True) + EPS) * g_ref[...]).astype(BF16)
    off = 0
    for o_ref, n in zip(o_refs, splits):
        o_ref[...] = _mm(xn, w_ref[:, off:off + n])
        off += n


def _norm_matmul(x, g, w_bf, splits, tm):
    n, d = x.shape
    ntot = w_bf.shape[1]
    assert sum(splits) == ntot and n % tm == 0
    return pl.pallas_call(
        functools.partial(_nm_body, splits=tuple(splits)),
        grid=(n // tm,),
        in_specs=[pl.BlockSpec((tm, d), lambda i: (i, 0)), _const_spec((1, d)), _const_spec((d, ntot))],
        out_specs=[pl.BlockSpec((tm, s), lambda i: (i, 0)) for s in splits],
        out_shape=[jax.ShapeDtypeStruct((n, s), F32) for s in splits],
        compiler_params=_cparams("arbitrary"),
        name="norm_matmul",
    )(x, g.reshape(1, d), w_bf)


def _final_norm_body(x_ref, g_ref, o_ref):
    x = x_ref[...]
    o_ref[...] = x * lax.rsqrt(jnp.mean(x * x, axis=-1, keepdims=True) + EPS) * g_ref[...]


def _final_norm(x, g, tm):
    n, d = x.shape
    return pl.pallas_call(
        _final_norm_body, grid=(n // tm,),
        in_specs=[pl.BlockSpec((tm, d), lambda i: (i, 0)), _const_spec((1, d))],
        out_specs=pl.BlockSpec((tm, d), lambda i: (i, 0)),
        out_shape=jax.ShapeDtypeStruct((n, d), F32),
        compiler_params=_cparams("arbitrary"), name="final_norm",
    )(x, g.reshape(1, d))


def _evout_body(oa_ref, og_ref, ob_ref, w_ref, x_ref, o_ref):
    a = (oa_ref[...] * _sigmoid(og_ref[...])).astype(BF16)
    b = ob_ref[...].astype(BF16)
    o_ref[...] = x_ref[...] + _mm(a, w_ref[0:FOX_W, :]) + _mm(b, w_ref[FOX_W:FOX_W + RW_W, :])


def _ev_out(o_attn, p_fox, o_b, w_bf, x, tm):
    n, d = x.shape
    return pl.pallas_call(
        _evout_body, grid=(n // tm,),
        in_specs=[pl.BlockSpec((tm, FOX_W), lambda i: (i, 0)),
                  pl.BlockSpec((tm, FOX_W), lambda i: (i, 3)),
                  pl.BlockSpec((tm, RW_W), lambda i: (i, 0)),
                  _const_spec((FOX_W + RW_W, d)),
                  pl.BlockSpec((tm, d), lambda i: (i, 0))],
        out_specs=pl.BlockSpec((tm, d), lambda i: (i, 0)),
        out_shape=jax.ShapeDtypeStruct((n, d), F32),
        compiler_params=_cparams("arbitrary"), name="ev_out",
    )(o_attn, p_fox, o_b, w_bf, x)


def _odout_body(oc_ref, w_ref, x_ref, o_ref):
    o_ref[...] = x_ref[...] + _mm(oc_ref[...].astype(BF16), w_ref[...])


def _od_out(o_c, w_bf, x, tm):
    n, d = x.shape
    k = o_c.shape[1]
    return pl.pallas_call(
        _odout_body, grid=(n // tm,),
        in_specs=[pl.BlockSpec((tm, k), lambda i: (i, 0)), _const_spec((k, d)),
                  pl.BlockSpec((tm, d), lambda i: (i, 0))],
        out_specs=pl.BlockSpec((tm, d), lambda i: (i, 0)),
        out_shape=jax.ShapeDtypeStruct((n, d), F32),
        compiler_params=_cparams("arbitrary"), name="od_out",
    )(o_c, w_bf, x)


def _ffn_cols(xn, wup_ref, cw_ref, cb_ref, wdn_ref, acc, prev_fn, tail_fn, f, cwb):
    for c in range(f // cwb):
        ys = []
        for half in (0, 1):
            lo = half * f + c * cwb
            h = _mm(xn, wup_ref[:, lo:lo + cwb])
            hm1, hm2 = prev_fn(h, lo)
            ys.append(hm2 * cw_ref[0:1, lo:lo + cwb] + hm1 * cw_ref[1:2, lo:lo + cwb]
                      + h * cw_ref[2:3, lo:lo + cwb] + cb_ref[:, lo:lo + cwb])
            tail_fn(h, lo)
        u, gt = ys
        act = (gt * _sigmoid(gt) * u).astype(BF16)
        acc = acc + _mm(act, wdn_ref[c * cwb:(c + 1) * cwb, :])
    return acc


def _ffn_norm(x_ref, g_ref):
    x = x_ref[...]
    return x, (x * lax.rsqrt(jnp.mean(x * x, axis=-1, keepdims=True) + EPS) * g_ref[...]).astype(BF16)


def _ffn_seq_body(x_ref, g_ref, wup_ref, cw_ref, cb_ref, wdn_ref, init_ref, o_ref, st_ref, carry_ref,
                  *, tb, f, cwb):
    @pl.when(pl.program_id(1) == 0)
    def _():
        carry_ref[0:2, :] = init_ref[0]

    x, xn = _ffn_norm(x_ref, g_ref)
    row = lax.broadcasted_iota(jnp.int32, (tb, cwb), 0)

    def prev_fn(h, lo):
        c0 = carry_ref[0:1, lo:lo + cwb]
        c1 = carry_ref[1:2, lo:lo + cwb]
        hm1 = jnp.where(row == 0, c1, pltpu.roll(h, 1, 0))
        hm2 = jnp.where(row == 0, c0, jnp.where(row == 1, c1, pltpu.roll(h, 2, 0)))
        return hm1, hm2

    def tail_fn(h, lo):
        carry_ref[0:2, lo:lo + cwb] = h[tb - 2:tb, :]
        st_ref[0, :, lo:lo + cwb] = h[tb - 2:tb, :]

    o_ref[...] = _ffn_cols(xn, wup_ref, cw_ref, cb_ref, wdn_ref, x, prev_fn, tail_fn, f, cwb)


def _ffn_flat_body(x_ref, g_ref, wup_ref, cw_ref, cb_ref, wdn_ref, f1_ref, f2_ref, o_ref, st_ref,
                   *, rows, seq, f, cwb):
    x, xn = _ffn_norm(x_ref, g_ref)
    tmod = lax.broadcasted_iota(jnp.int32, (rows, cwb), 0) % seq

    def prev_fn(h, lo):
        hm1 = jnp.where(tmod == 0, f1_ref[:, lo:lo + cwb], pltpu.roll(h, 1, 0))
        hm2 = jnp.where(tmod < 2, f2_ref[:, lo:lo + cwb], pltpu.roll(h, 2, 0))
        return hm1, hm2

    def tail_fn(h, lo):
        st_ref[:, :, lo:lo + cwb] = h.reshape(rows // seq, seq, cwb)[:, seq - 2:seq, :]

    o_ref[...] = _ffn_cols(xn, wup_ref, cw_ref, cb_ref, wdn_ref, x, prev_fn, tail_fn, f, cwb)


def _ffn(x, g, wup_bf, conv_w, conv_b, wdn_bf, conv_prev, batch, seq):
    n, d = x.shape
    f2 = wup_bf.shape[1]
    f = f2 // 2
    cwb = f // 2
    weights = [_const_spec((1, d)), _const_spec((d, f2)), _const_spec((3, f2)), _const_spec((1, f2)),
               _const_spec((f, d))]
    out_shape = [jax.ShapeDtypeStruct((n, d), F32), jax.ShapeDtypeStruct((batch, 2, f2), F32)]
    args = (x, g.reshape(1, d), wup_bf, conv_w, conv_b.reshape(1, f2), wdn_bf)
    if seq >= 256:
        tb = 256
        nt = seq // tb
        return pl.pallas_call(
            functools.partial(_ffn_seq_body, tb=tb, f=f, cwb=cwb),
            grid=(batch, nt),
            in_specs=[pl.BlockSpec((tb, d), lambda b, t: (b * nt + t, 0))] + weights
                     + [pl.BlockSpec((1, 2, f2), lambda b, t: (b, 0, 0))],
            out_specs=[pl.BlockSpec((tb, d), lambda b, t: (b * nt + t, 0)),
                       pl.BlockSpec((1, 2, f2), lambda b, t: (b, 0, 0))],
            out_shape=out_shape,
            scratch_shapes=[pltpu.VMEM((8, f2), F32)],
            compiler_params=_cparams("arbitrary", "arbitrary"), name="ffn_seq",
        )(*args, conv_prev)
    zeros = jnp.zeros((batch, seq - 2, f2), F32)
    fill2 = jnp.concatenate([conv_prev, zeros], axis=1).reshape(n, f2)
    fill1 = jnp.concatenate([conv_prev[:, 1:2], zeros, zeros[:, :1]], axis=1).reshape(n, f2)
    return pl.pallas_call(
        functools.partial(_ffn_flat_body, rows=n, seq=seq, f=f, cwb=cwb),
        grid=(1,),
        in_specs=[pl.BlockSpec((n, d), lambda i: (0, 0))] + weights
                 + [pl.BlockSpec((n, f2), lambda i: (0, 0)), pl.BlockSpec((n, f2), lambda i: (0, 0))],
        out_specs=[pl.BlockSpec((n, d), lambda i: (0, 0)), pl.BlockSpec((batch, 2, f2), lambda i: (0, 0, 0))],
        out_shape=out_shape,
        compiler_params=_cparams("arbitrary"), name="ffn_flat",
    )(*args, fill1, fill2)


def _fox_consts():
    src = jnp.arange(FOX_W)
    place = jnp.zeros((FOX_W, H_A * LANES), F32).at[src, (src // DH) * LANES + src % DH].set(1.0)
    hh = jnp.arange(H_A)
    eq, ek = [], []
    for piece in range(3):
        eq.append(jnp.zeros((LANES, H_A * LANES), F32).at[hh, hh * LANES + DH + piece].set(1.0))
        ek.append(jnp.zeros((LANES, H_A * LANES), F32).at[hh, hh * LANES + DH + 3 + piece].set(-1.0))
    cq = jnp.zeros((1, H_A * LANES), F32)
    ck = jnp.zeros((1, H_A * LANES), F32)
    for piece in range(3):
        cq = cq.at[0, hh * LANES + DH + 3 + piece].set(1.0)
        ck = ck.at[0, hh * LANES + DH + piece].set(1.0)
    return place.astype(BF16), jnp.stack(eq).astype(BF16), jnp.stack(ek).astype(BF16), cq, ck


def _fox_prep_body(q_ref, k_ref, v_ref, fl_ref, qg_ref, kg_ref, bf_ref, pm_ref, eq_ref, ek_ref, cq_ref,
                   ck_ref, tri_ref, qa_ref, ka_ref, vb_ref, ko_ref, lf_ref, carry_ref, *, tm):
    @pl.when(pl.program_id(1) == 0)
    def _():
        carry_ref[...] = jnp.zeros_like(carry_ref)

    qn = _rms_heads(q_ref[...], qg_ref[...]) * (DH ** -0.5)
    kn = _rms_heads(k_ref[...], kg_ref[...])
    lf = _log_sigmoid(fl_ref[...] + bf_ref[...])
    lf_ref[...] = lf
    fcum = _sel_l(tri_ref[...], lf) + carry_ref[0:1, :]
    carry_ref[0:1, :] = fcum[tm - 1:tm, :]
    fh, fm, flo = _split3(fcum)
    pm = pm_ref[...]
    qa = (_mm(qn.astype(BF16), pm) + _mm(fh, eq_ref[0]) + _mm(fm, eq_ref[1]) + _mm(flo, eq_ref[2])
          + cq_ref[...])
    ka = (_mm(kn.astype(BF16), pm) + _mm(fh, ek_ref[0]) + _mm(fm, ek_ref[1]) + _mm(flo, ek_ref[2])
          + ck_ref[...])
    qa_ref[...] = qa.astype(BF16)
    ka_ref[...] = ka.astype(BF16)
    vb_ref[...] = v_ref[...].astype(BF16)
    ko_ref[...] = kn


def _fox_prep(p_fox, p_fl, q_gain, k_gain, b_f, batch, seq):
    n = p_fox.shape[0]
    tm = 256
    nt = seq // tm
    place, eq, ek, cq, ck = _fox_consts()
    tri = jnp.tril(jnp.ones((tm, tm), F32)).astype(BF16)
    bf = jnp.zeros((1, LANES), F32).at[0, :H_A].set(b_f)
    row = lambda c: pl.BlockSpec((tm, FOX_W), lambda b, t, c=c: (b * nt + t, c))
    wide = pl.BlockSpec((tm, H_A * LANES), lambda b, t: (b * nt + t, 0))
    narrow = pl.BlockSpec((tm, LANES), lambda b, t: (b * nt + t, 0))
    return pl.pallas_call(
        functools.partial(_fox_prep_body, tm=tm),
        grid=(batch, nt),
        in_specs=[row(0), row(1), row(2), narrow,
                  _const_spec((1, FOX_W)), _const_spec((1, FOX_W)), _const_spec((1, LANES)),
                  _const_spec(place.shape), _const_spec(eq.shape), _const_spec(ek.shape),
                  _const_spec(cq.shape), _const_spec(ck.shape), _const_spec(tri.shape)],
        out_specs=[wide, wide, row(0), row(0), narrow],
        out_shape=[jax.ShapeDtypeStruct((n, H_A * LANES), BF16), jax.ShapeDtypeStruct((n, H_A * LANES), BF16),
                   jax.ShapeDtypeStruct((n, FOX_W), BF16), jax.ShapeDtypeStruct((n, FOX_W), F32),
                   jax.ShapeDtypeStruct((n, LANES), F32)],
        scratch_shapes=[pltpu.VMEM((8, LANES), F32)],
        compiler_params=_cparams("arbitrary", "arbitrary"), name="fox_prep",
    )(p_fox, p_fox, p_fox, p_fl, jnp.tile(q_gain, H_A).reshape(1, FOX_W),
      jnp.tile(k_gain, H_A).reshape(1, FOX_W), bf, place, eq, ek, cq, ck, tri)


def _fox_attn_body(qa_ref, ka_ref, v_ref, o_ref, *, tq):
    i = pl.program_id(2)
    row = lax.broadcasted_iota(jnp.int32, (tq, tq), 0)
    col = lax.broadcasted_iota(jnp.int32, (tq, tq), 1)
    causal = col <= row
    outs = []
    for hh in range(2):
        q = qa_ref[:, hh * LANES:(hh + 1) * LANES]

        def block(j, carry, masked, q=q, hh=hh):
            m, l, acc = carry
            off = pl.multiple_of(j * tq, tq)
            k = ka_ref[pl.ds(off, tq), hh * LANES:(hh + 1) * LANES]
            v = v_ref[pl.ds(off, tq), :]
            s = _mm_nt(q, k)
            if masked:
                s = jnp.where(causal, s, NEG_BIG)
            m2 = jnp.maximum(m, jnp.max(s, axis=-1, keepdims=True))
            p = jnp.exp(s - m2)
            a = jnp.exp(m - m2)
            return m2, a * l + jnp.sum(p, axis=-1, keepdims=True), a * acc + _mm(p.astype(BF16), v)

        init = (jnp.full((tq, 1), NEG_BIG, F32), jnp.zeros((tq, 1), F32), jnp.zeros((tq, LANES), F32))
        carry = lax.fori_loop(0, i, lambda j, c: block(j, c, False), init)
        m, l, acc = block(i, carry, True)
        outs.append(acc / l)
    lane = lax.broadcasted_iota(jnp.int32, (1, LANES), 1)
    o_ref[...] = jnp.where(lane < DH, outs[0], outs[1])


def _fox_attn(q_aug, k_aug, v_bf, batch, seq):
    n = q_aug.shape[0]
    tq = min(512, seq)
    nq = seq // tq
    return pl.pallas_call(
        functools.partial(_fox_attn_body, tq=tq),
        grid=(batch, H_A // 2, nq),
        in_specs=[pl.BlockSpec((tq, 2 * LANES), lambda b, hp, i: (b * nq + i, hp)),
                  pl.BlockSpec((seq, 2 * LANES), lambda b, hp, i: (b, hp)),
                  pl.BlockSpec((seq, LANES), lambda b, hp, i: (b, hp))],
        out_specs=pl.BlockSpec((tq, LANES), lambda b, hp, i: (b * nq + i, hp)),
        out_shape=jax.ShapeDtypeStruct((n, FOX_W), F32),
        compiler_params=_cparams("arbitrary", "arbitrary", "arbitrary"), name="fox_attn",
    )(q_aug, k_aug, v_bf)


def _fox_decode_body(pt_ref, q_ref, k_ref, v_ref, fl_ref, qg_ref, kg_ref, bf_ref, ms_ref, ps_ref, pa_ref,
                     asel_ref, *rest, pps, seq):
    del pt_ref
    kp, vp, lp = rest[:pps], rest[pps:2 * pps], rest[2 * pps:3 * pps]
    o_ref, ko_ref, lfo_ref = rest[3 * pps:3 * pps + 3]
    qh_ref, cb_ref, m_ref, l_ref, acc_ref, car_ref = rest[3 * pps + 3:]
    j = pl.program_id(1)
    nrow = H_A * seq

    @pl.when(j == 0)
    def _new_tokens():
        q = _rms_heads(q_ref[...], qg_ref[...]) * (DH ** -0.5)
        k = _rms_heads(k_ref[...], kg_ref[...])
        ko_ref[...] = k
        lf = _log_sigmoid(fl_ref[...] + bf_ref[...])
        lfo_ref[...] = lf
        row = lax.broadcasted_iota(jnp.int32, (seq, LANES), 0)
        c = lf
        s = 1
        while s < seq:
            c = c + jnp.where(row >= s, pltpu.roll(c, s, 0), 0.0)
            s *= 2
        v = v_ref[...]
        zpad = jnp.zeros((LANES - seq, DH), F32)
        srows = []
        for h in range(H_A):
            qh = q[:, h * DH:(h + 1) * DH]
            qh_ref[h] = qh
            cb_ref[h * seq:(h + 1) * seq, :] = jnp.broadcast_to(c[:, h:h + 1], (seq, LANES))
            kpad = jnp.concatenate([k[:, h * DH:(h + 1) * DH], zpad], axis=0).astype(BF16)
            srows.append(_mm_nt(qh.astype(BF16), kpad))
        cneg = jnp.concatenate([-c, jnp.zeros((LANES - seq, LANES), F32)], axis=0)
        s_new = jnp.concatenate(srows, axis=0) + cb_ref[...] + _sel_nt(asel_ref[0:nrow, :], cneg)
        keyi = lax.broadcasted_iota(jnp.int32, (nrow, LANES), 1)
        ti = lax.broadcasted_iota(jnp.int32, (nrow, LANES), 0) % seq
        s_new = jnp.where(keyi <= ti, s_new, NEG_BIG)
        m = jnp.max(s_new, axis=-1, keepdims=True)
        p = jnp.exp(s_new - m)
        m_ref[...] = m
        l_ref[...] = jnp.sum(p, axis=-1, keepdims=True)
        accs = []
        for h in range(H_A):
            vpad = jnp.concatenate([v[:, h * DH:(h + 1) * DH], zpad], axis=0).astype(BF16)
            accs.append(_mm(p[h * seq:(h + 1) * seq, :].astype(BF16), vpad))
        acc_ref[...] = jnp.concatenate(accs, axis=0)
        car_ref[...] = jnp.zeros_like(car_ref)

    lft = jnp.concatenate([lp[i][...] for i in range(pps)]
                          + [jnp.zeros((LANES - pps * H_A, LANES), F32)], axis=0)
    r8 = _sel_r(lft, ms_ref[...])
    tot = jnp.broadcast_to(r8[:, 0:1] + lft[:, 0:1], (LANES, LANES))
    car = car_ref[...]
    rfull = r8 + _sel_l(ps_ref[...], tot) + car
    car_ref[...] = car + _sel_l(pa_ref[...], tot)
    bias = _sel_l(asel_ref[...], rfull)
    cb = cb_ref[...]
    blocks = []
    for i in range(pps):
        rows = [_mm(qh_ref[h].astype(BF16), kp[i][h].astype(BF16)) for h in range(H_A)]
        blocks.append(jnp.concatenate(rows, axis=0) + bias[i * nrow:(i + 1) * nrow, :] + cb)
    s_all = jnp.concatenate(blocks, axis=1)
    m_old = m_ref[...]
    m_new = jnp.maximum(m_old, jnp.max(s_all, axis=-1, keepdims=True))
    p = jnp.exp(s_all - m_new)
    a = jnp.exp(m_old - m_new)
    m_ref[...] = m_new
    l_ref[...] = a * l_ref[...] + jnp.sum(p, axis=-1, keepdims=True)
    pv = []
    for h in range(H_A):
        vh = jnp.concatenate([vp[i][h] for i in range(pps)], axis=1)
        pv.append(_mm_nt(p[h * seq:(h + 1) * seq, :].astype(BF16), vh.astype(BF16)))
    acc_ref[...] = a * acc_ref[...] + jnp.concatenate(pv, axis=0)

    @pl.when(j == pl.num_programs(1) - 1)
    def _():
        o = acc_ref[...] / l_ref[...]
        for h in range(H_A):
            o_ref[:, h * DH:(h + 1) * DH] = o[h * seq:(h + 1) * seq, :]


def _fox_decode(p_fox, p_fl, q_gain, k_gain, b_f, pool_k, pool_v, pool_lf, page_table, e, batch, seq):
    n = p_fox.shape[0]
    n_pages = page_table.shape[1]
    page = pool_lf.shape[2]
    assert page == LANES and seq == 8
    pps = 8
    while n_pages % pps:
        pps //= 2
    groups = n_pages // pps
    n_layers, n_pool = pool_k.shape[:2]
    pk = jnp.transpose(pool_k, (0, 1, 3, 4, 2))
    pv = jnp.transpose(pool_v, (0, 1, 3, 4, 2))
    plf = jnp.transpose(pool_lf, (0, 1, 3, 2))
    nrow = H_A * seq
    idx = jnp.arange(LANES)
    ms = (idx[:, None] > idx[None, :]).astype(BF16)
    same_h = (idx[:, None] % H_A) == (idx[None, :] % H_A)
    valid = (idx[:, None] < pps * H_A) & (idx[None, :] < pps * H_A)
    ps = (same_h & valid & (idx[None, :] // H_A < idx[:, None] // H_A)).astype(BF16)
    pa = (same_h & valid).astype(BF16)
    r = jnp.arange(pps * nrow)
    asel = jnp.zeros((pps * nrow, LANES), F32).at[r, (r // nrow) * H_A + (r % nrow) // seq].set(1.0).astype(BF16)
    bf = jnp.zeros((1, LANES), F32).at[0, :H_A].set(b_f)

    def page_spec(i, shape):
        def index(b, j, pt):
            return (e, pt[b, n_pages - 1 - (j * pps + i)]) + (0,) * len(shape)
        return pl.BlockSpec((None, None) + shape, index)

    rowspec = lambda c: pl.BlockSpec((seq, FOX_W), lambda b, j, pt, c=c: (b, c))
    cst = lambda shape: pl.BlockSpec(shape, lambda b, j, pt: (0,) * len(shape))
    in_specs = ([rowspec(0), rowspec(1), rowspec(2), pl.BlockSpec((seq, LANES), lambda b, j, pt: (b, 0)),
                 cst((1, FOX_W)), cst((1, FOX_W)), cst((1, LANES)), cst(ms.shape), cst(ps.shape), cst(pa.shape),
                 cst(asel.shape)]
                + [page_spec(i, (H_A, DH, page)) for i in range(pps)]
                + [page_spec(i, (H_A, DH, page)) for i in range(pps)]
                + [page_spec(i, (H_A, page)) for i in range(pps)])
    grid_spec = pltpu.PrefetchScalarGridSpec(
        num_scalar_prefetch=1, grid=(batch, groups), in_specs=in_specs,
        out_specs=[pl.BlockSpec((seq, FOX_W), lambda b, j, pt: (b, 0)),
                   pl.BlockSpec((seq, FOX_W), lambda b, j, pt: (b, 0)),
                   pl.BlockSpec((seq, LANES), lambda b, j, pt: (b, 0))],
        scratch_shapes=[pltpu.VMEM((H_A, seq, DH), F32), pltpu.VMEM((nrow, LANES), F32),
                        pltpu.VMEM((nrow, 1), F32), pltpu.VMEM((nrow, 1), F32),
                        pltpu.VMEM((nrow, DH), F32), pltpu.VMEM((LANES, LANES), F32)])
    return pl.pallas_call(
        functools.partial(_fox_decode_body, pps=pps, seq=seq),
        grid_spec=grid_spec,
        out_shape=[jax.ShapeDtypeStruct((n, FOX_W), F32), jax.ShapeDtypeStruct((n, FOX_W), F32),
                   jax.ShapeDtypeStruct((n, LANES), F32)],
        compiler_params=_cparams("arbitrary", "arbitrary"), name="fox_decode",
    )(page_table, p_fox, p_fox, p_fox, p_fl, jnp.tile(q_gain, H_A).reshape(1, FOX_W),
      jnp.tile(k_gain, H_A).reshape(1, FOX_W), bf, ms, ps, pa, asel,
      *([pk] * pps), *([pv] * pps), *([plf] * pps))


def _rw_prep_body(p_ref, init_ref, mu_ref, w0_ref, w2_ref, a0_ref, a2_ref, g2_ref, kk_ref, ka_ref, *rest,
                  tb, first):
    if first:
        r_o, lw_o, k_o, v_o, a_o, b_o, g_o, sh_o, carry_ref = rest
    else:
        v0_ref, v1_ref, v2_ref, vf_ref, r_o, lw_o, k_o, v_o, a_o, b_o, g_o, sh_o, carry_ref = rest

    @pl.when(pl.program_id(1) == 0)
    def _():
        carry_ref[7:8, :] = init_ref[0]

    p = p_ref[...]
    p_prev = _shift_rows(p, 1, carry_ref[...])
    carry_ref[7:8, :] = p[tb - 1:tb, :]
    sh_o[0] = p[tb - 1:tb, :]
    ps = p + (p_prev - p) * mu_ref[...]
    r = ps[:, 0:RW_W]
    k = ps[:, RW_W:2 * RW_W]
    v = ps[:, 2 * RW_W:3 * RW_W]
    x128 = ps[:, 3 * RW_W:3 * RW_W + LANES]
    gd = ps[:, 3 * RW_W + LANES:3 * RW_W + 2 * LANES]
    lane = lax.broadcasted_iota(jnp.int32, (1, LANES), 1)
    xw = jnp.where(lane < DH, jnp.tanh(x128), 0.0).astype(BF16)
    xa = jnp.where(lane < DH, 0.0, x128).astype(BF16)
    w_log = _log_sigmoid(w0_ref[...] + _mm(xw, w2_ref[...])) - 0.5
    lw_o[...] = -jnp.exp(w_log)
    a = _sigmoid(a0_ref[...] + _mm(xa, a2_ref[...]))
    g_o[...] = _mm(_sigmoid(gd).astype(BF16), g2_ref[...])
    if not first:
        gate = _sigmoid(v0_ref[...] + _mm(_mm(v.astype(BF16), v1_ref[...]).astype(BF16), v2_ref[...]))
        v = v + (vf_ref[...] - v) * gate
    kkx = k * kk_ref[...]
    kk = kkx * _head_scale(kkx * kkx, lambda s: lax.rsqrt(s + L2_EPS))
    r_o[...] = r
    k_o[...] = k * (1.0 + (a - 1.0) * ka_ref[...])
    v_o[...] = v
    a_o[...] = -kk
    b_o[...] = kk * a


def _rw_prep(p_rw, shift_prev, w, e, v_first, batch, seq):
    n, cols = p_rw.shape
    tb = min(256, seq)
    nt = seq // tb
    first = e == 0
    pad_rows = lambda m: jnp.concatenate([m, jnp.zeros((LANES - m.shape[0], m.shape[1]), m.dtype)], axis=0)
    w2p = pad_rows(w['rw_w2'][e]).astype(BF16)
    a2p = jnp.concatenate([jnp.zeros((DH, RW_W), F32), w['rw_a2'][e]], axis=0).astype(BF16)
    vec = lambda x: x.reshape(1, -1)
    args = [p_rw, shift_prev.reshape(batch, 1, cols), vec(w['rw_mu'][e]), vec(w['rw_w0'][e]), w2p,
            vec(w['rw_a0'][e]), a2p, w['rw_g2'][e].astype(BF16), vec(w['rw_k_k'][e]), vec(w['rw_k_a'][e])]
    rowspec = pl.BlockSpec((tb, RW_W), lambda b, t: (b * nt + t, 0))
    in_specs = [pl.BlockSpec((tb, cols), lambda b, t: (b * nt + t, 0)),
                pl.BlockSpec((1, 1, cols), lambda b, t: (b, 0, 0)),
                _const_spec((1, cols)), _const_spec((1, RW_W)), _const_spec((LANES, RW_W)),
                _const_spec((1, RW_W)), _const_spec((LANES, RW_W)), _const_spec((LANES, RW_W)),
                _const_spec((1, RW_W)), _const_spec((1, RW_W))]
    if not first:
        v1p = jnp.concatenate([w['rw_v1'][e - 1], jnp.zeros((RW_W, LANES - w['rw_v1'].shape[2]), F32)], axis=1)
        args += [vec(w['rw_v0'][e - 1]), v1p.astype(BF16), pad_rows(w['rw_v2'][e - 1]).astype(BF16), v_first]
        in_specs += [_const_spec((1, RW_W)), _const_spec((RW_W, LANES)), _const_spec((LANES, RW_W)), rowspec]
    outs = pl.pallas_call(
        functools.partial(_rw_prep_body, tb=tb, first=first),
        grid=(batch, nt), in_specs=in_specs,
        out_specs=[rowspec] * 7 + [pl.BlockSpec((1, 1, cols), lambda b, t: (b, 0, 0))],
        out_shape=[jax.ShapeDtypeStruct((n, RW_W), F32)] * 7 + [jax.ShapeDtypeStruct((batch, 1, cols), F32)],
        scratch_shapes=[pltpu.VMEM((8, cols), F32)],
        compiler_params=_cparams("arbitrary", "arbitrary"), name="rw_prep",
    )(*args)
    return outs[:7], outs[7].reshape(batch, cols)


def _pad_chunk(x, rows):
    if x.shape[0] == rows:
        return x
    return jnp.concatenate([x, jnp.zeros((rows - x.shape[0], x.shape[1]), x.dtype)], axis=0)


def _rw_scan_body(r_ref, lw_ref, k_ref, v_ref, a_ref, b_ref, g_ref, rk_ref, lnw_ref, lnb_ref, s0_ref, tri_ref,
                  bd_ref, o_ref, so_ref, s_ref, y_ref, *, tb):
    @pl.when(pl.program_id(2) == 0)
    def _():
        s_ref[...] = s0_ref[0, 0]

    c = CHUNK
    lane = lax.broadcasted_iota(jnp.int32, (1, LANES), 1)
    m0 = lane < DH
    r2i = lax.broadcasted_iota(jnp.int32, (2 * c, 2 * c), 0)
    c2i = lax.broadcasted_iota(jnp.int32, (2 * c, 2 * c), 1)
    strict = (r2i % c) > (c2i % c)
    lower = (r2i % c) >= (c2i % c)
    blockdiag = (r2i // DH) == (c2i // DH)

    def stack2(z):
        return jnp.concatenate([jnp.where(m0, z, 0.0), jnp.where(m0, 0.0, z)], axis=0)

    nchunk = max(tb // c, 1)
    for ci in range(nchunk):
        sl = slice(ci * c, min((ci + 1) * c, tb))
        lw = _pad_chunk(lw_ref[sl, :], c)
        r = _pad_chunk(r_ref[sl, :], c)
        k = _pad_chunk(k_ref[sl, :], c)
        v = _pad_chunk(v_ref[sl, :], c)
        a = _pad_chunk(a_ref[sl, :], c)
        b = _pad_chunk(b_ref[sl, :], c)
        cum = _sel_l(tri_ref[...], lw)
        e_end = jnp.exp(cum[c - 1:c, :] - cum)
        e_out = jnp.exp(-cum)
        a2 = stack2(a * jnp.exp(cum - lw)).astype(BF16)
        r2 = stack2(r * jnp.exp(cum)).astype(BF16)
        b2 = stack2(b * e_out).astype(BF16)
        k2 = stack2(k * e_out).astype(BF16)
        v2 = stack2(v)
        lm = _mm_nt(jnp.concatenate([a2, r2], axis=0), jnp.concatenate([b2, k2], axis=0))
        lab = jnp.where(strict, lm[0:2 * c, 0:2 * c], 0.0)
        lak = jnp.where(strict, lm[0:2 * c, 2 * c:4 * c], 0.0)
        mrb = jnp.where(lower, lm[2 * c:4 * c, 0:2 * c], 0.0)
        mrk = jnp.where(lower, lm[2 * c:4 * c, 2 * c:4 * c], 0.0)
        x = _inv_unit_lower(lab)
        xw = jnp.concatenate([x, _mm3(x, lak)], axis=1)
        st = s_ref[...]
        ars = _mm_nt(jnp.concatenate([a2, r2], axis=0), st.astype(BF16))
        u2 = _mm3(xw, jnp.concatenate([ars[0:2 * c, :], v2], axis=0))
        uv = jnp.concatenate([u2, v2], axis=0).astype(BF16)
        y2 = ars[2 * c:4 * c, :] + _mm(jnp.concatenate([mrb, mrk], axis=1).astype(BF16), uv)
        y = y2[0:c, :] + y2[c:2 * c, :]
        bk = jnp.concatenate([stack2(b * e_end), stack2(k * e_end)], axis=0).astype(BF16)
        upd = _mm_tn(uv, bk)
        s_ref[...] = st * jnp.exp(cum[c - 1:c, :]) + jnp.where(blockdiag, upd, 0.0)
        y_ref[sl, :] = y[0:sl.stop - sl.start, :]

    so_ref[0, 0] = s_ref[...]
    y = y_ref[...]
    bd = bd_ref[...]
    mu = _sel_r(y, bd) * (1.0 / DH)
    d = y - mu
    var = _sel_r(d * d, bd) * (1.0 / DH)
    yn = d * lax.rsqrt(var + RW_GN_EPS) * lnw_ref[...] + lnb_ref[...]
    rr = r_ref[...]
    kk = k_ref[...]
    vv = v_ref[...]
    yn = yn + _sel_r(rr * kk * rk_ref[...], bd) * vv
    o_ref[...] = yn * g_ref[...]


def _rw_scan(parts, w, e, s0_blk, batch, seq):
    r, lw, k, v, a, b, g = parts
    n = r.shape[0]
    tb = min(256, seq)
    nt = seq // tb
    npair = RW_W // LANES
    idx = jnp.arange(LANES)
    bd = ((idx[:, None] // DH) == (idx[None, :] // DH)).astype(BF16)
    tri = jnp.tril(jnp.ones((CHUNK, CHUNK), F32)).astype(BF16)
    rowspec = pl.BlockSpec((tb, LANES), lambda bb, hp, t: (bb * nt + t, hp))
    vecspec = pl.BlockSpec((1, LANES), lambda bb, hp, t: (0, hp))
    stspec = pl.BlockSpec((1, 1, LANES, LANES), lambda bb, hp, t: (bb, hp, 0, 0))
    cst = lambda shape: pl.BlockSpec(shape, lambda bb, hp, t: (0,) * len(shape))
    return pl.pallas_call(
        functools.partial(_rw_scan_body, tb=tb),
        grid=(batch, npair, nt),
        in_specs=[rowspec] * 7 + [vecspec] * 3 + [stspec, cst(tri.shape), cst(bd.shape)],
        out_specs=[rowspec, stspec],
        out_shape=[jax.ShapeDtypeStruct((n, RW_W), F32), jax.ShapeDtypeStruct(s0_blk.shape, F32)],
        scratch_shapes=[pltpu.VMEM((LANES, LANES), F32), pltpu.VMEM((tb, LANES), F32)],
        compiler_params=_cparams("arbitrary", "arbitrary", "arbitrary"), name="rw_scan",
    )(r, lw, k, v, a, b, g, w['rw_r_k'][e].reshape(1, RW_W), w['rw_ln_w'][e].reshape(1, RW_W),
      w['rw_ln_b'][e].reshape(1, RW_W), s0_blk, tri, bd)


def _rw_state_to_blocks(s):
    bsz = s.shape[0]
    s = s.reshape(bsz, H_A // 2, 2, DH, DH)
    z = jnp.zeros_like(s[:, :, 0])
    top = jnp.concatenate([s[:, :, 0], z], axis=-1)
    bot = jnp.concatenate([z, s[:, :, 1]], axis=-1)
    return jnp.concatenate([top, bot], axis=-2)


def _rw_blocks_to_state(sb):
    bsz = sb.shape[0]
    return jnp.stack([sb[:, :, :DH, :DH], sb[:, :, DH:, DH:]], axis=2).reshape(bsz, H_A, DH, DH)


def _gdn_body(q_ref, k_ref, v_ref, z_ref, ba_ref, qi_ref, ki_ref, vi_ref, cwq_ref, cwk_ref, cwv_ref,
              par_ref, nw_ref, s0_ref, tri_ref, sel_ref, o_ref, so_ref, qc_ref, kc_ref, vc_ref,
              s_ref, carry_ref, *, tb):
    hp = pl.program_id(1)

    @pl.when(pl.program_id(2) == 0)
    def _():
        carry_ref[0, 5:8, :] = qi_ref[0]
        carry_ref[1, 5:8, :] = ki_ref[0]
        carry_ref[2, 5:8, :] = vi_ref[0]
        s_ref[...] = s0_ref[0]

    def conv(x_ref, w_ref, idx, out_ref):
        x = x_ref[...]
        prev = carry_ref[idx]
        y = x * w_ref[3:4, :]
        for kk in range(1, 4):
            y = y + _shift_rows(x, kk, prev) * w_ref[3 - kk:4 - kk, :]
        if tb >= 3:
            tail = x[tb - 3:tb, :]
        else:
            tail = jnp.concatenate([prev[8 - (3 - tb):8, :], x], axis=0)
        carry_ref[idx, 5:8, :] = tail
        out_ref[0] = tail
        return y * _sigmoid(y)

    q = conv(q_ref, cwq_ref, 0, qc_ref)
    k = conv(k_ref, cwk_ref, 1, kc_ref)
    v = conv(v_ref, cwv_ref, 2, vc_ref)

    ba = ba_ref[...]
    lane = lax.broadcasted_iota(jnp.int32, (1, LANES), 1)
    beta_all = _sigmoid(ba)
    z_in = ba + par_ref[1:2, :]
    g_all = -jnp.exp(par_ref[0:1, :]) * (jnp.maximum(z_in, 0.0) + jnp.log1p(jnp.exp(-jnp.abs(z_in))))

    def column(x, idx):
        return jnp.sum(jnp.where(lane == idx, x, 0.0), axis=-1, keepdims=True)

    c = CHUNK
    r2i = lax.broadcasted_iota(jnp.int32, (2 * c, 2 * c), 0)
    c2i = lax.broadcasted_iota(jnp.int32, (2 * c, 2 * c), 1)
    same = (r2i // c) == (c2i // c)
    strict = same & (r2i > c2i)
    lower = same & (r2i >= c2i)

    heads = []
    for hh in range(2):
        sl = slice(hh * DK, (hh + 1) * DK)
        qh = q[:, sl]
        kh = k[:, sl]
        qn = qh * lax.rsqrt(jnp.sum(qh * qh, axis=-1, keepdims=True) + L2_EPS) * (DK ** -0.5)
        kn = kh * lax.rsqrt(jnp.sum(kh * kh, axis=-1, keepdims=True) + L2_EPS)
        beta = column(beta_all, 2 * hp + hh)
        g = column(g_all, H_C + 2 * hp + hh)
        heads.append((qn, kn, v[:, sl], jnp.broadcast_to(beta, (tb, DK)), jnp.broadcast_to(g, (tb, DK))))

    nchunk = max(tb // c, 1)
    for ci in range(nchunk):
        sl = slice(ci * c, min((ci + 1) * c, tb))
        nv = sl.stop - sl.start
        qn2, kn2, v2, b2, g2 = [jnp.concatenate([_pad_chunk(heads[0][i][sl, :], c),
                                                 _pad_chunk(heads[1][i][sl, :], c)], axis=0)
                                for i in range(5)]
        gc = _sel_l(tri_ref[...], g2)
        grow = _sel_nt(sel_ref[...], gc)
        diff = gc - grow
        gamma = jnp.where(lower, jnp.exp(jnp.where(lower, diff, 0.0)), 0.0)
        kb = kn2 * b2
        kk_qk = _mm_nt(jnp.concatenate([kb, qn2], axis=0).astype(BF16), kn2.astype(BF16))
        mmat = jnp.where(strict, kk_qk[0:2 * c, :] * gamma, 0.0)
        amat = jnp.where(lower, kk_qk[2 * c:4 * c, :] * gamma, 0.0)
        tinv = _inv_unit_lower(-mmat)
        eg = jnp.exp(gc)
        uw = _mm3(tinv, jnp.concatenate([v2 * b2, kb * eg], axis=1))
        u = uw[:, 0:DK]
        wk = uw[:, DK:2 * DK]
        qe = qn2 * eg
        vnew, qs = [], []
        for hh in range(2):
            rs = slice(hh * c, (hh + 1) * c)
            sb = s_ref[hh].astype(BF16)
            wq = _mm(jnp.concatenate([wk[rs, :], qe[rs, :]], axis=0).astype(BF16), sb)
            vnew.append(u[rs, :] - wq[0:c, :])
            qs.append(wq[c:2 * c, :])
        vn2 = jnp.concatenate(vnew, axis=0)
        o2 = jnp.concatenate(qs, axis=0) + _mm(amat.astype(BF16), vn2.astype(BF16))
        for hh in range(2):
            rs = slice(hh * c, (hh + 1) * c)
            glast = gc[hh * c + c - 1:hh * c + c, :]
            kdec = (kn2[rs, :] * jnp.exp(glast - gc[rs, :])).astype(BF16)
            s_ref[hh] = s_ref[hh] * jnp.exp(glast[:, 0:1]) + _mm_tn(kdec, vn2[rs, :].astype(BF16))
            oh = o2[rs, :][0:nv, :]
            zz = z_ref[sl, hh * DK:(hh + 1) * DK]
            on = oh * lax.rsqrt(jnp.mean(oh * oh, axis=-1, keepdims=True) + EPS) * nw_ref[...]
            o_ref[sl, hh * DK:(hh + 1) * DK] = on * (zz * _sigmoid(zz))
    so_ref[0] = s_ref[...]


def _gdn(p_qkv, p_z, p_ba, conv_prev, s0, w, o_idx, batch, seq):
    n = p_qkv.shape[0]
    tb = min(256, seq)
    nt = seq // tb
    npair = H_C // 2
    gk = H_C * DK
    cw = w['gdn_conv_w'][o_idx]
    par = jnp.zeros((8, LANES), F32)
    par = par.at[0, H_C:2 * H_C].set(w['gdn_A_log'][o_idx]).at[1, H_C:2 * H_C].set(w['gdn_dt_bias'][o_idx])
    idx = jnp.arange(2 * CHUNK)
    tri = (((idx[:, None] // CHUNK) == (idx[None, :] // CHUNK)) & (idx[:, None] >= idx[None, :])).astype(BF16)
    sel = jnp.zeros((2 * CHUNK, LANES), F32).at[:, 0].set(1.0).astype(BF16)
    s0p = s0.reshape(batch * npair, 2, DK, DK)
    wide = 2 * DK
    col = lambda off: pl.BlockSpec((tb, wide), lambda b, hp, t, off=off: (b * nt + t, off + hp))
    prev = lambda off: pl.BlockSpec((1, 3, wide), lambda b, hp, t, off=off: (b, 0, off + hp))
    wspec = lambda off: pl.BlockSpec((4, wide), lambda b, hp, t, off=off: (0, off + hp))
    cst = lambda shape: pl.BlockSpec(shape, lambda b, hp, t: (0,) * len(shape))
    stspec = pl.BlockSpec((1, 2, DK, DK), lambda b, hp, t: (b * npair + hp, 0, 0, 0))
    cvspec = pl.BlockSpec((1, 3, wide), lambda b, hp, t: (b, 0, hp))
    outs = pl.pallas_call(
        functools.partial(_gdn_body, tb=tb),
        grid=(batch, npair, nt),
        in_specs=[col(0), col(npair), col(2 * npair), col(0),
                  pl.BlockSpec((tb, LANES), lambda b, hp, t: (b * nt + t, 0)),
                  prev(0), prev(npair), prev(2 * npair), wspec(0), wspec(npair), wspec(2 * npair),
                  cst((8, LANES)), cst((1, DK)), stspec, cst(tri.shape), cst(sel.shape)],
        out_specs=[col(0), stspec, cvspec, cvspec, cvspec],
        out_shape=[jax.ShapeDtypeStruct((n, gk), F32), jax.ShapeDtypeStruct(s0p.shape, F32)]
                  + [jax.ShapeDtypeStruct((batch, 3, gk), F32)] * 3,
        scratch_shapes=[pltpu.VMEM((2, DK, DK), F32), pltpu.VMEM((3, 8, wide), F32)],
        compiler_params=_cparams("arbitrary", "arbitrary", "arbitrary"), name="gdn",
    )(p_qkv, p_qkv, p_qkv, p_z, p_ba, conv_prev, conv_prev, conv_prev, cw, cw, cw, par,
      w['gdn_norm_w'][o_idx].reshape(1, DK), s0p, tri, sel)
    o_c, s_out, qc, kc, vc = outs
    return o_c, jnp.concatenate([qc, kc, vc], axis=-1), s_out.reshape(batch, H_C, DK, DK)


def _run_trunk(x, fox_past, rw_state, rw_shift, gdn_state, gdn_conv, ffn_conv, page_table, w, wb):
    batch, seq, d = x.shape
    n = batch * seq
    depth = w['norm_mix'].shape[0]
    tm = 512 if n % 512 == 0 else n
    x = x.reshape(n, d)
    fk, fv, flf, rws, rwsh, gs, gcv, fcv = [], [], [], [], [], [], [], []
    v_first = None
    for layer in range(depth):
        if layer % 2 == 0:
            e = layer // 2
            p_fox, p_rw, p_fl = _norm_matmul(x, w['norm_mix'][layer], wb['ev_in'][e], (4 * FOX_W, 1792, LANES), tm)
            if fox_past is None:
                q_aug, k_aug, v_bf, k_out, lf = _fox_prep(p_fox, p_fl, w['fox_q_gain'][e], w['fox_k_gain'][e],
                                                          w['fox_b_f'][e], batch, seq)
                o_attn = _fox_attn(q_aug, k_aug, v_bf, batch, seq)
            else:
                o_attn, k_out, lf = _fox_decode(p_fox, p_fl, w['fox_q_gain'][e], w['fox_k_gain'][e], w['fox_b_f'][e],
                                                fox_past[0], fox_past[1], fox_past[2], page_table, e, batch, seq)
            parts, sh = _rw_prep(p_rw, rw_shift[e], w, e, v_first, batch, seq)
            if e == 0:
                v_first = parts[3]
            o_b, s_blk = _rw_scan(parts, w, e, _rw_state_to_blocks(rw_state[e]), batch, seq)
            x = _ev_out(o_attn, p_fox, o_b, wb['ev_out'][e], x, tm)
            fk.append(k_out.reshape(batch, seq, H_A, DH))
            fv.append(p_fox[:, 2 * FOX_W:3 * FOX_W].reshape(batch, seq, H_A, DH))
            flf.append(lf[:, :H_A].reshape(batch, seq, H_A))
            rws.append(_rw_blocks_to_state(s_blk))
            rwsh.append(sh)
        else:
            o_idx = layer // 2
            p_qkv, p_z, p_ba = _norm_matmul(x, w['norm_mix'][layer], wb['od_in'][o_idx],
                                            (3 * H_C * DK, H_C * DK, LANES), tm)
            o_c, cv, s_out = _gdn(p_qkv, p_z, p_ba, gdn_conv[o_idx], gdn_state[o_idx], w, o_idx, batch, seq)
            x = _od_out(o_c, wb['od_out'][o_idx], x, tm)
            gs.append(s_out)
            gcv.append(cv)
        x, buf = _ffn(x, w['norm_ffn'][layer], wb['ffn_up'][layer], w['ffn_conv_w'][layer],
                      w['ffn_conv_b'][layer], wb['ffn_down'][layer], ffn_conv[layer], batch, seq)
        fcv.append(buf)
    y = _final_norm(x, w['norm_out'], tm).reshape(batch, seq, d)
    return (y, jnp.stack(fk), jnp.stack(fv), jnp.stack(flf), jnp.stack(rws), jnp.stack(rwsh),
            jnp.stack(gs), jnp.stack(gcv), jnp.stack(fcv))


def _prep_weights(w):
    fox_cols = 4 * FOX_W + H_A
    ev = w['ev_w_in']
    pad = jnp.zeros(ev.shape[:2] + (LANES - H_A,), ev.dtype)
    ev_in = jnp.concatenate([ev[..., :4 * FOX_W], ev[..., fox_cols:], ev[..., 4 * FOX_W:fox_cols], pad], axis=-1)
    od = w['od_w_in']
    pad2 = jnp.zeros(od.shape[:2] + (LANES - 2 * H_C,), od.dtype)
    od_in = jnp.concatenate([od, pad2], axis=-1)
    return dict(ev_in=ev_in.astype(BF16), ev_out=w['ev_w_out'].astype(BF16), od_in=od_in.astype(BF16),
                od_out=w['od_w_out'].astype(BF16), ffn_up=w['ffn_w_up'].astype(BF16),
                ffn_down=w['ffn_w_down'].astype(BF16))


def kernel(x_prompt, x_sample, cache_fox_k, cache_fox_v, cache_fox_logf, state_rwkv, state_rwkv_shift, state_gdn, state_gdn_conv, state_ffn_conv, page_table, norm_mix, norm_ffn, norm_out, ev_w_in, ev_w_out, fox_b_f, fox_q_gain, fox_k_gain, rw_mu, rw_w0, rw_w2, rw_a0, rw_a2, rw_g2, rw_k_k, rw_k_a, rw_r_k, rw_ln_w, rw_ln_b, rw_v0, rw_v1, rw_v2, od_w_in, od_w_out, gdn_conv_w, gdn_A_log, gdn_dt_bias, gdn_norm_w, ffn_w_up, ffn_conv_w, ffn_conv_b, ffn_w_down):
    w = dict(norm_mix=norm_mix, norm_ffn=norm_ffn, norm_out=norm_out, ev_w_in=ev_w_in, ev_w_out=ev_w_out,
             fox_b_f=fox_b_f, fox_q_gain=fox_q_gain, fox_k_gain=fox_k_gain, rw_mu=rw_mu, rw_w0=rw_w0,
             rw_w2=rw_w2, rw_a0=rw_a0, rw_a2=rw_a2, rw_g2=rw_g2, rw_k_k=rw_k_k, rw_k_a=rw_k_a, rw_r_k=rw_r_k,
             rw_ln_w=rw_ln_w, rw_ln_b=rw_ln_b, rw_v0=rw_v0, rw_v1=rw_v1, rw_v2=rw_v2, od_w_in=od_w_in,
             od_w_out=od_w_out, gdn_conv_w=gdn_conv_w, gdn_A_log=gdn_A_log, gdn_dt_bias=gdn_dt_bias,
             gdn_norm_w=gdn_norm_w, ffn_w_up=ffn_w_up, ffn_conv_w=ffn_conv_w, ffn_conv_b=ffn_conv_b,
             ffn_w_down=ffn_w_down)
    wb = _prep_weights(w)
    bp = x_prompt.shape[0]
    n_even, n_odd, depth = ev_w_in.shape[0], od_w_in.shape[0], norm_mix.shape[0]
    rw_cols = rw_mu.shape[1]
    (y_p, fk_p, fv_p, flf_p, rw_p, rwsh_p, gdn_p, gcv_p, fcv_p) = _run_trunk(
        x_prompt, None,
        jnp.zeros((n_even, bp, H_A, DH, DH), F32), jnp.zeros((n_even, bp, rw_cols), F32),
        jnp.zeros((n_odd, bp, H_C, DK, DK), F32), jnp.zeros((n_odd, bp, 3, 3 * H_C * DK), F32),
        jnp.zeros((depth, bp, 2, ffn_w_up.shape[2]), F32), page_table, w, wb)
    (y_s, fk_s, fv_s, flf_s, rw_s, rwsh_s, gdn_s, gcv_s, fcv_s) = _run_trunk(
        x_sample, (cache_fox_k, cache_fox_v, cache_fox_logf), state_rwkv, state_rwkv_shift,
        state_gdn, state_gdn_conv, state_ffn_conv, page_table, w, wb)
    return (y_p, y_s, fk_p, fv_p, flf_p, fk_s, fv_s, flf_s, rw_p, rw_s, rwsh_p, rwsh_s,
            gdn_p, gdn_s, gcv_p, gcv_s, fcv_p, fcv_s)
```

```python
import functools
import math

import jax
import jax.numpy as jnp
from jax import lax
from jax.experimental import pallas as pl
from jax.experimental.pallas import tpu as pltpu

F32 = jnp.float32
BF16 = jnp.bfloat16

EPS = 1e-6
RW_GN_EPS = 64e-5
L2_EPS = 1e-6
NEG_BIG = -1e30

H_A = 8
DH = 64
FOX_W = 512
RW_W = 512
H_C = 8
DK = 128
CHUNK = 64
LANES = 128
VMEM_LIMIT = 56 * 1024 * 1024


def _cparams(*sem):
    return pltpu.CompilerParams(dimension_semantics=sem, vmem_limit_bytes=VMEM_LIMIT)


def _const_spec(shape):
    nd = len(shape)
    return pl.BlockSpec(shape, lambda *_: (0,) * nd, pipeline_mode=pl.Buffered(1))


def _mm(a, b):
    return jnp.dot(a, b, preferred_element_type=F32)


def _mm_nt(a, b):
    return lax.dot_general(a, b, (((1,), (1,)), ((), ())), preferred_element_type=F32)


def _mm_tn(a, b):
    return lax.dot_general(a, b, (((0,), (0,)), ((), ())), preferred_element_type=F32)


def _split3(x):
    hi = x.astype(BF16)
    r = x - hi.astype(F32)
    mid = r.astype(BF16)
    lo = (r - mid.astype(F32)).astype(BF16)
    return hi, mid, lo


def _sel_l(m01, x):
    hi, mid, lo = _split3(x)
    return _mm(m01, hi) + _mm(m01, mid) + _mm(m01, lo)


def _sel_r(x, m01):
    hi, mid, lo = _split3(x)
    return _mm(hi, m01) + _mm(mid, m01) + _mm(lo, m01)


def _sel_nt(m01, x):
    hi, mid, lo = _split3(x)
    return _mm_nt(m01, hi) + _mm_nt(m01, mid) + _mm_nt(m01, lo)


def _mm3(a, b):
    ah = a.astype(BF16)
    al = (a - ah.astype(F32)).astype(BF16)
    bh = b.astype(BF16)
    bl = (b - bh.astype(F32)).astype(BF16)
    return _mm(ah, bh) + _mm(ah, bl) + _mm(al, bh)


def _mm3_tn(a, b):
    ah = a.astype(BF16)
    al = (a - ah.astype(F32)).astype(BF16)
    bh = b.astype(BF16)
    bl = (b - bh.astype(F32)).astype(BF16)
    return _mm_tn(ah, bh) + _mm_tn(ah, bl) + _mm_tn(al, bh)


def _log_sigmoid(z):
    return jnp.minimum(z, 0.0) - jnp.log1p(jnp.exp(-jnp.abs(z)))


def _sigmoid(z):
    return 1.0 / (1.0 + jnp.exp(-z))


def _inv_unit_lower(ns):
    size = ns[0].shape[0]
    r = lax.broadcasted_iota(jnp.int32, (size, size), 0)
    c = lax.broadcasted_iota(jnp.int32, (size, size), 1)
    eye = jnp.where(r == c, 1.0, 0.0)
    ps = [eye + n for n in ns]
    nks = list(ns)
    for _ in range(int(math.log2(CHUNK)) - 1):
        nks = [_mm3(nk, nk) for nk in nks]
        ps = [p + _mm3(p, nk) for p, nk in zip(ps, nks)]
    return ps


def _head_scale(x2, fn):
    lane_h = lax.broadcasted_iota(jnp.int32, (1, x2.shape[1]), 1) // DH
    out = jnp.zeros_like(x2)
    for h in range(x2.shape[1] // DH):
        s = jnp.sum(x2[:, h * DH:(h + 1) * DH], axis=-1, keepdims=True)
        out = jnp.where(lane_h == h, fn(s), out)
    return out


def _rms_heads(x, gain):
    return x * _head_scale(x * x, lambda s: lax.rsqrt(s * (1.0 / DH) + EPS)) * gain


def _shift_rows(x, k, prev):
    row = lax.broadcasted_iota(jnp.int32, x.shape, 0)
    out = pltpu.roll(x, k, 0)
    nprev = prev.shape[0]
    for j in range(k):
        out = jnp.where(row == j, prev[nprev - k + j:nprev - k + j + 1, :], out)
    return out


def _nm_body(x_ref, g_ref, w_ref, *o_refs, splits):
    x = x_ref[...]
    xn = (x * lax.rsqrt(jnp.mean(x * x, axis=-1, keepdims=True) + EPS) * g_ref[...]).astype(BF16)
    off = 0
    for o_ref, n in zip(o_refs, splits):
        o_ref[...] = _mm(xn, w_ref[:, off:off + n])
        off += n


def _norm_matmul(x, g, w_bf, splits, tm):
    n, d = x.shape
    ntot = w_bf.shape[1]
    assert sum(splits) == ntot and n % tm == 0
    return pl.pallas_call(
        functools.partial(_nm_body, splits=tuple(splits)),
        grid=(n // tm,),
        in_specs=[pl.BlockSpec((tm, d), lambda i: (i, 0)), _const_spec((1, d)), _const_spec((d, ntot))],
        out_specs=[pl.BlockSpec((tm, s), lambda i: (i, 0)) for s in splits],
        out_shape=[jax.ShapeDtypeStruct((n, s), F32) for s in splits],
        compiler_params=_cparams("arbitrary"),
        name="norm_matmul",
    )(x, g.reshape(1, d), w_bf)


def _final_norm_body(x_ref, g_ref, o_ref):
    x = x_ref[...]
    o_ref[...] = x * lax.rsqrt(jnp.mean(x * x, axis=-1, keepdims=True) + EPS) * g_ref[...]


def _final_norm(x, g, tm):
    n, d = x.shape
    return pl.pallas_call(
        _final_norm_body, grid=(n // tm,),
        in_specs=[pl.BlockSpec((tm, d), lambda i: (i, 0)), _const_spec((1, d))],
        out_specs=pl.BlockSpec((tm, d), lambda i: (i, 0)),
        out_shape=jax.ShapeDtypeStruct((n, d), F32),
        compiler_params=_cparams("arbitrary"), name="final_norm",
    )(x, g.reshape(1, d))


def _evout_body(oa_ref, og_ref, ob_ref, w_ref, x_ref, o_ref):
    a = (oa_ref[...] * _sigmoid(og_ref[...])).astype(BF16)
    b = ob_ref[...].astype(BF16)
    o_ref[...] = x_ref[...] + _mm(a, w_ref[0:FOX_W, :]) + _mm(b, w_ref[FOX_W:FOX_W + RW_W, :])


def _ev_out(o_attn, p_fox, o_b, w_bf, x, tm):
    n, d = x.shape
    return pl.pallas_call(
        _evout_body, grid=(n // tm,),
        in_specs=[pl.BlockSpec((tm, FOX_W), lambda i: (i, 0)),
                  pl.BlockSpec((tm, FOX_W), lambda i: (i, 3)),
                  pl.BlockSpec((tm, RW_W), lambda i: (i, 0)),
                  _const_spec((FOX_W + RW_W, d)),
                  pl.BlockSpec((tm, d), lambda i: (i, 0))],
        out_specs=pl.BlockSpec((tm, d), lambda i: (i, 0)),
        out_shape=jax.ShapeDtypeStruct((n, d), F32),
        compiler_params=_cparams("arbitrary"), name="ev_out",
    )(o_attn, p_fox, o_b, w_bf, x)


def _odout_body(oc_ref, w_ref, x_ref, o_ref):
    o_ref[...] = x_ref[...] + _mm(oc_ref[...].astype(BF16), w_ref[...])


def _od_out(o_c, w_bf, x, tm):
    n, d = x.shape
    k = o_c.shape[1]
    return pl.pallas_call(
        _odout_body, grid=(n // tm,),
        in_specs=[pl.BlockSpec((tm, k), lambda i: (i, 0)), _const_spec((k, d)),
                  pl.BlockSpec((tm, d), lambda i: (i, 0))],
        out_specs=pl.BlockSpec((tm, d), lambda i: (i, 0)),
        out_shape=jax.ShapeDtypeStruct((n, d), F32),
        compiler_params=_cparams("arbitrary"), name="od_out",
    )(o_c, w_bf, x)


def _ffn_cols(xn, wup_ref, cw_ref, cb_ref, wdn_ref, acc, prev_fn, tail_fn, f, cwb):
    for c in range(f // cwb):
        ys = []
        for half in (0, 1):
            lo = half * f + c * cwb
            h = _mm(xn, wup_ref[:, lo:lo + cwb])
            hm1, hm2 = prev_fn(h, lo)
            ys.append(hm2 * cw_ref[0:1, lo:lo + cwb] + hm1 * cw_ref[1:2, lo:lo + cwb]
                      + h * cw_ref[2:3, lo:lo + cwb] + cb_ref[:, lo:lo + cwb])
            tail_fn(h, lo)
        u, gt = ys
        act = (gt * _sigmoid(gt) * u).astype(BF16)
        acc = acc + _mm(act, wdn_ref[c * cwb:(c + 1) * cwb, :])
    return acc


def _ffn_norm(x_ref, g_ref):
    x = x_ref[...]
    return x, (x * lax.rsqrt(jnp.mean(x * x, axis=-1, keepdims=True) + EPS) * g_ref[...]).astype(BF16)


def _ffn_seq_body(x_ref, g_ref, wup_ref, cw_ref, cb_ref, wdn_ref, init_ref, o_ref, st_ref, carry_ref,
                  *, tb, f, cwb):
    @pl.when(pl.program_id(1) == 0)
    def _():
        carry_ref[0:2, :] = init_ref[0]

    x, xn = _ffn_norm(x_ref, g_ref)
    row = lax.broadcasted_iota(jnp.int32, (tb, cwb), 0)

    def prev_fn(h, lo):
        c0 = carry_ref[0:1, lo:lo + cwb]
        c1 = carry_ref[1:2, lo:lo + cwb]
        hm1 = jnp.where(row == 0, c1, pltpu.roll(h, 1, 0))
        hm2 = jnp.where(row == 0, c0, jnp.where(row == 1, c1, pltpu.roll(h, 2, 0)))
        return hm1, hm2

    def tail_fn(h, lo):
        carry_ref[0:2, lo:lo + cwb] = h[tb - 2:tb, :]
        st_ref[0, :, lo:lo + cwb] = h[tb - 2:tb, :]

    o_ref[...] = _ffn_cols(xn, wup_ref, cw_ref, cb_ref, wdn_ref, x, prev_fn, tail_fn, f, cwb)


def _ffn_flat_body(x_ref, g_ref, wup_ref, cw_ref, cb_ref, wdn_ref, f1_ref, f2_ref, o_ref, st_ref,
                   *, rows, seq, f, cwb):
    x, xn = _ffn_norm(x_ref, g_ref)
    tmod = lax.broadcasted_iota(jnp.int32, (rows, cwb), 0) % seq

    def prev_fn(h, lo):
        hm1 = jnp.where(tmod == 0, f1_ref[:, lo:lo + cwb], pltpu.roll(h, 1, 0))
        hm2 = jnp.where(tmod < 2, f2_ref[:, lo:lo + cwb], pltpu.roll(h, 2, 0))
        return hm1, hm2

    def tail_fn(h, lo):
        st_ref[:, :, lo:lo + cwb] = h.reshape(rows // seq, seq, cwb)[:, seq - 2:seq, :]

    o_ref[...] = _ffn_cols(xn, wup_ref, cw_ref, cb_ref, wdn_ref, x, prev_fn, tail_fn, f, cwb)


def _ffn(x, g, wup_bf, conv_w, conv_b, wdn_bf, conv_prev, batch, seq):
    n, d = x.shape
    f2 = wup_bf.shape[1]
    f = f2 // 2
    cwb = f // 2
    weights = [_const_spec((1, d)), _const_spec((d, f2)), _const_spec((3, f2)), _const_spec((1, f2)),
               _const_spec((f, d))]
    out_shape = [jax.ShapeDtypeStruct((n, d), F32), jax.ShapeDtypeStruct((batch, 2, f2), F32)]
    args = (x, g.reshape(1, d), wup_bf, conv_w, conv_b.reshape(1, f2), wdn_bf)
    if seq >= 256:
        tb = 256
        nt = seq // tb
        return pl.pallas_call(
            functools.partial(_ffn_seq_body, tb=tb, f=f, cwb=cwb),
            grid=(batch, nt),
            in_specs=[pl.BlockSpec((tb, d), lambda b, t: (b * nt + t, 0))] + weights
                     + [pl.BlockSpec((1, 2, f2), lambda b, t: (b, 0, 0))],
            out_specs=[pl.BlockSpec((tb, d), lambda b, t: (b * nt + t, 0)),
                       pl.BlockSpec((1, 2, f2), lambda b, t: (b, 0, 0))],
            out_shape=out_shape,
            scratch_shapes=[pltpu.VMEM((8, f2), F32)],
            compiler_params=_cparams("arbitrary", "arbitrary"), name="ffn_seq",
        )(*args, conv_prev)
    zeros = jnp.zeros((batch, seq - 2, f2), F32)
    fill2 = jnp.concatenate([conv_prev, zeros], axis=1).reshape(n, f2)
    fill1 = jnp.concatenate([conv_prev[:, 1:2], zeros, zeros[:, :1]], axis=1).reshape(n, f2)
    return pl.pallas_call(
        functools.partial(_ffn_flat_body, rows=n, seq=seq, f=f, cwb=cwb),
        grid=(1,),
        in_specs=[pl.BlockSpec((n, d), lambda i: (0, 0))] + weights
                 + [pl.BlockSpec((n, f2), lambda i: (0, 0)), pl.BlockSpec((n, f2), lambda i: (0, 0))],
        out_specs=[pl.BlockSpec((n, d), lambda i: (0, 0)), pl.BlockSpec((batch, 2, f2), lambda i: (0, 0, 0))],
        out_shape=out_shape,
        compiler_params=_cparams("arbitrary"), name="ffn_flat",
    )(*args, fill1, fill2)


def _fox_consts():
    src = jnp.arange(FOX_W)
    place = jnp.zeros((FOX_W, H_A * LANES), F32).at[src, (src // DH) * LANES + src % DH].set(1.0)
    hh = jnp.arange(H_A)
    eq, ek = [], []
    for piece in range(3):
        eq.append(jnp.zeros((LANES, H_A * LANES), F32).at[hh, hh * LANES + DH + piece].set(1.0))
        ek.append(jnp.zeros((LANES, H_A * LANES), F32).at[hh, hh * LANES + DH + 3 + piece].set(-1.0))
    cq = jnp.zeros((1, H_A * LANES), F32)
    ck = jnp.zeros((1, H_A * LANES), F32)
    for piece in range(3):
        cq = cq.at[0, hh * LANES + DH + 3 + piece].set(1.0)
        ck = ck.at[0, hh * LANES + DH + piece].set(1.0)
    return place.astype(BF16), jnp.stack(eq).astype(BF16), jnp.stack(ek).astype(BF16), cq, ck


def _fox_prep_body(q_ref, k_ref, v_ref, fl_ref, qg_ref, kg_ref, bf_ref, pm_ref, pt_ref, eq_ref, ek_ref, cq_ref,
                   ck_ref, vone_ref, tri_ref, qa_ref, ka_ref, vt_ref, ko_ref, lf_ref, carry_ref, *, tm):
    @pl.when(pl.program_id(1) == 0)
    def _():
        carry_ref[...] = jnp.zeros_like(carry_ref)

    qn = _rms_heads(q_ref[...], qg_ref[...]) * (DH ** -0.5)
    kn = _rms_heads(k_ref[...], kg_ref[...])
    lf = _log_sigmoid(fl_ref[...] + bf_ref[...])
    lf_ref[...] = lf
    fcum = _sel_l(tri_ref[...], lf) + carry_ref[0:1, :]
    carry_ref[0:1, :] = fcum[tm - 1:tm, :]
    fh, fm, flo = _split3(fcum)
    pm = pm_ref[...]
    qa = (_mm(qn.astype(BF16), pm) + _mm(fh, eq_ref[0]) + _mm(fm, eq_ref[1]) + _mm(flo, eq_ref[2])
          + cq_ref[...])
    ka = (_mm(kn.astype(BF16), pm) + _mm(fh, ek_ref[0]) + _mm(fm, ek_ref[1]) + _mm(flo, ek_ref[2])
          + ck_ref[...])
    qa_ref[...] = qa.astype(BF16)
    ka_ref[...] = ka.astype(BF16)
    vt = _mm_nt(pt_ref[...], v_ref[...].astype(BF16)) + vone_ref[...]
    vt_ref[0, :, 0] = vt.astype(BF16).reshape(H_A, LANES, tm)
    ko_ref[...] = kn


FOX_TILE = 512


def _fox_prep(p_fox, p_fl, q_gain, k_gain, b_f, batch, seq):
    n = p_fox.shape[0]
    tm = min(FOX_TILE, seq)
    nt = seq // tm
    place, eq, ek, cq, ck = _fox_consts()
    place_t = place.T
    hh = jnp.arange(H_A)
    vone = jnp.zeros((H_A * LANES, 1), F32).at[hh * LANES + DH, 0].set(1.0)
    tri = jnp.tril(jnp.ones((tm, tm), F32)).astype(BF16)
    bf = jnp.zeros((1, LANES), F32).at[0, :H_A].set(b_f)
    row = lambda c: pl.BlockSpec((tm, FOX_W), lambda b, t, c=c: (b * nt + t, c))
    wide = pl.BlockSpec((tm, H_A * LANES), lambda b, t: (b * nt + t, 0))
    narrow = pl.BlockSpec((tm, LANES), lambda b, t: (b * nt + t, 0))
    return pl.pallas_call(
        functools.partial(_fox_prep_body, tm=tm),
        grid=(batch, nt),
        in_specs=[row(0), row(1), row(2), narrow,
                  _const_spec((1, FOX_W)), _const_spec((1, FOX_W)), _const_spec((1, LANES)),
                  _const_spec(place.shape), _const_spec(place_t.shape), _const_spec(eq.shape), _const_spec(ek.shape),
                  _const_spec(cq.shape), _const_spec(ck.shape), _const_spec(vone.shape), _const_spec(tri.shape)],
        out_specs=[wide, wide, pl.BlockSpec((1, H_A, 1, LANES, tm), lambda b, t: (b, 0, t, 0, 0)), row(0), narrow],
        out_shape=[jax.ShapeDtypeStruct((n, H_A * LANES), BF16), jax.ShapeDtypeStruct((n, H_A * LANES), BF16),
                   jax.ShapeDtypeStruct((batch, H_A, nt, LANES, tm), BF16), jax.ShapeDtypeStruct((n, FOX_W), F32),
                   jax.ShapeDtypeStruct((n, LANES), F32)],
        scratch_shapes=[pltpu.VMEM((8, LANES), F32)],
        compiler_params=_cparams("arbitrary", "arbitrary"), name="fox_prep",
    )(p_fox, p_fox, p_fox, p_fl, jnp.tile(q_gain, H_A).reshape(1, FOX_W),
      jnp.tile(k_gain, H_A).reshape(1, FOX_W), bf, place, place_t, eq, ek, cq, ck, vone, tri)


def _fox_attn_body(qa_ref, ka_ref, vt_ref, o_ref, *, tq):
    i = pl.program_id(2)
    key = lax.broadcasted_iota(jnp.int32, (tq, tq), 0)
    qry = lax.broadcasted_iota(jnp.int32, (tq, tq), 1)
    causal = key <= qry
    qs = [qa_ref[:, hh * LANES:(hh + 1) * LANES] for hh in range(2)]

    def block(j, carry, masked):
        off = pl.multiple_of(j * tq, tq)
        sts = [_mm_nt(ka_ref[pl.ds(off, tq), hh * LANES:(hh + 1) * LANES], qs[hh]) for hh in range(2)]
        if masked:
            sts = [jnp.where(causal, st, NEG_BIG) for st in sts]
        m2s = [jnp.maximum(m, jnp.max(st, axis=0, keepdims=True)) for (m, _), st in zip(carry, sts)]
        ps = [jnp.exp(st - m2).astype(BF16) for st, m2 in zip(sts, m2s)]
        return tuple((m2, jnp.exp(m - m2) * acc + _mm(vt_ref[0, hh, j], p))
                     for hh, ((m, acc), m2, p) in enumerate(zip(carry, m2s, ps)))

    init = tuple((jnp.full((1, tq), NEG_BIG, F32), jnp.zeros((LANES, tq), F32)) for _ in range(2))
    carry = lax.fori_loop(0, i, lambda j, c: block(j, c, False), init)
    carry = block(i, carry, True)
    halves = [acc[0:DH, :] / acc[DH:DH + 1, :] for _, acc in carry]
    o_ref[...] = jnp.transpose(jnp.concatenate(halves, axis=0))


def _fox_attn(q_aug, k_aug, v_t, batch, seq):
    n = q_aug.shape[0]
    tq = min(FOX_TILE, seq)
    nq = seq // tq
    return pl.pallas_call(
        functools.partial(_fox_attn_body, tq=tq),
        grid=(batch, H_A // 2, nq),
        in_specs=[pl.BlockSpec((tq, 2 * LANES), lambda b, hp, i: (b * nq + i, hp)),
                  pl.BlockSpec((seq, 2 * LANES), lambda b, hp, i: (b, hp)),
                  pl.BlockSpec((1, 2, nq, LANES, tq), lambda b, hp, i: (b, hp, 0, 0, 0))],
        out_specs=pl.BlockSpec((tq, LANES), lambda b, hp, i: (b * nq + i, hp)),
        out_shape=jax.ShapeDtypeStruct((n, FOX_W), F32),
        compiler_params=_cparams("arbitrary", "arbitrary", "arbitrary"), name="fox_attn",
    )(q_aug, k_aug, v_t)


def _fox_decode_body(pt_ref, q_ref, k_ref, v_ref, fl_ref, qg_ref, kg_ref, bf_ref, ms_ref, ps_ref, pa_ref,
                     asel_ref, *rest, pps, seq):
    del pt_ref
    kp, vp, lp = rest[:pps], rest[pps:2 * pps], rest[2 * pps:3 * pps]
    o_ref, ko_ref, lfo_ref = rest[3 * pps:3 * pps + 3]
    qh_ref, cb_ref, m_ref, l_ref, acc_ref, car_ref = rest[3 * pps + 3:]
    j = pl.program_id(1)
    nrow = H_A * seq

    @pl.when(j == 0)
    def _new_tokens():
        q = _rms_heads(q_ref[...], qg_ref[...]) * (DH ** -0.5)
        k = _rms_heads(k_ref[...], kg_ref[...])
        ko_ref[...] = k
        lf = _log_sigmoid(fl_ref[...] + bf_ref[...])
        lfo_ref[...] = lf
        row = lax.broadcasted_iota(jnp.int32, (seq, LANES), 0)
        c = lf
        s = 1
        while s < seq:
            c = c + jnp.where(row >= s, pltpu.roll(c, s, 0), 0.0)
            s *= 2
        v = v_ref[...]
        zpad = jnp.zeros((LANES - seq, DH), F32)
        srows = []
        for h in range(H_A):
            qh = q[:, h * DH:(h + 1) * DH]
            qh_ref[h] = qh
            cb_ref[h * seq:(h + 1) * seq, :] = jnp.broadcast_to(c[:, h:h + 1], (seq, LANES))
            kpad = jnp.concatenate([k[:, h * DH:(h + 1) * DH], zpad], axis=0).astype(BF16)
            srows.append(_mm_nt(qh.astype(BF16), kpad))
        cneg = jnp.concatenate([-c, jnp.zeros((LANES - seq, LANES), F32)], axis=0)
        s_new = jnp.concatenate(srows, axis=0) + cb_ref[...] + _sel_nt(asel_ref[0:nrow, :], cneg)
        keyi = lax.broadcasted_iota(jnp.int32, (nrow, LANES), 1)
        ti = lax.broadcasted_iota(jnp.int32, (nrow, LANES), 0) % seq
        s_new = jnp.where(keyi <= ti, s_new, NEG_BIG)
        m = jnp.max(s_new, axis=-1, keepdims=True)
        p = jnp.exp(s_new - m)
        m_ref[...] = m
        l_ref[...] = jnp.sum(p, axis=-1, keepdims=True)
        accs = []
        for h in range(H_A):
            vpad = jnp.concatenate([v[:, h * DH:(h + 1) * DH], zpad], axis=0).astype(BF16)
            accs.append(_mm(p[h * seq:(h + 1) * seq, :].astype(BF16), vpad))
        acc_ref[...] = jnp.concatenate(accs, axis=0)
        car_ref[...] = jnp.zeros_like(car_ref)

    lft = jnp.concatenate([lp[i][...] for i in range(pps)]
                          + [jnp.zeros((LANES - pps * H_A, LANES), F32)], axis=0)
    tot = jnp.broadcast_to(jnp.sum(lft, axis=1, keepdims=True), (LANES, LANES))
    qhs = [qh_ref[h].astype(BF16) for h in range(H_A)]
    scores = [jnp.concatenate([_mm(qhs[h], kp[i][h].astype(BF16)) for h in range(H_A)], axis=0)
              for i in range(pps)]
    car = car_ref[...]
    rfull = _sel_r(lft, ms_ref[...]) + _sel_l(ps_ref[...], tot) + car
    car_ref[...] = car + _sel_l(pa_ref[...], tot)
    bias = _sel_l(asel_ref[...], rfull)
    cb = cb_ref[...]
    s_all = jnp.concatenate([scores[i] + bias[i * nrow:(i + 1) * nrow, :] + cb for i in range(pps)],
                            axis=1)
    m_old = m_ref[...]
    m_new = jnp.maximum(m_old, jnp.max(s_all, axis=-1, keepdims=True))
    p = jnp.exp(s_all - m_new)
    a = jnp.exp(m_old - m_new)
    m_ref[...] = m_new
    l_ref[...] = a * l_ref[...] + jnp.sum(p, axis=-1, keepdims=True)
    pv = []
    for h in range(H_A):
        vh = jnp.concatenate([vp[i][h] for i in range(pps)], axis=1)
        pv.append(_mm_nt(p[h * seq:(h + 1) * seq, :].astype(BF16), vh.astype(BF16)))
    acc_ref[...] = a * acc_ref[...] + jnp.concatenate(pv, axis=0)

    @pl.when(j == pl.num_programs(1) - 1)
    def _():
        o = acc_ref[...] / l_ref[...]
        for h in range(H_A):
            o_ref[:, h * DH:(h + 1) * DH] = o[h * seq:(h + 1) * seq, :]


def _fox_decode(p_fox, p_fl, q_gain, k_gain, b_f, pool_k, pool_v, pool_lf, page_table, e, batch, seq):
    n = p_fox.shape[0]
    n_pages = page_table.shape[1]
    page = pool_lf.shape[2]
    assert page == LANES and seq == 8
    pps = 8
    while n_pages % pps:
        pps //= 2
    groups = n_pages // pps
    n_layers, n_pool = pool_k.shape[:2]
    pk = jnp.transpose(pool_k, (0, 1, 3, 4, 2))
    pv = jnp.transpose(pool_v, (0, 1, 3, 4, 2))
    plf = jnp.transpose(pool_lf, (0, 1, 3, 2))
    nrow = H_A * seq
    idx = jnp.arange(LANES)
    ms = (idx[:, None] > idx[None, :]).astype(BF16)
    same_h = (idx[:, None] % H_A) == (idx[None, :] % H_A)
    valid = (idx[:, None] < pps * H_A) & (idx[None, :] < pps * H_A)
    ps = (same_h & valid & (idx[None, :] // H_A < idx[:, None] // H_A)).astype(BF16)
    pa = (same_h & valid).astype(BF16)
    r = jnp.arange(pps * nrow)
    asel = jnp.zeros((pps * nrow, LANES), F32).at[r, (r // nrow) * H_A + (r % nrow) // seq].set(1.0).astype(BF16)
    bf = jnp.zeros((1, LANES), F32).at[0, :H_A].set(b_f)

    def page_spec(i, shape):
        def index(b, j, pt):
            return (e, pt[b, n_pages - 1 - (j * pps + i)]) + (0,) * len(shape)
        return pl.BlockSpec((None, None) + shape, index)

    rowspec = lambda c: pl.BlockSpec((seq, FOX_W), lambda b, j, pt, c=c: (b, c))
    cst = lambda shape: pl.BlockSpec(shape, lambda b, j, pt: (0,) * len(shape))
    in_specs = ([rowspec(0), rowspec(1), rowspec(2), pl.BlockSpec((seq, LANES), lambda b, j, pt: (b, 0)),
                 cst((1, FOX_W)), cst((1, FOX_W)), cst((1, LANES)), cst(ms.shape), cst(ps.shape), cst(pa.shape),
                 cst(asel.shape)]
                + [page_spec(i, (H_A, DH, page)) for i in range(pps)]
                + [page_spec(i, (H_A, DH, page)) for i in range(pps)]
                + [page_spec(i, (H_A, page)) for i in range(pps)])
    grid_spec = pltpu.PrefetchScalarGridSpec(
        num_scalar_prefetch=1, grid=(batch, groups), in_specs=in_specs,
        out_specs=[pl.BlockSpec((seq, FOX_W), lambda b, j, pt: (b, 0)),
                   pl.BlockSpec((seq, FOX_W), lambda b, j, pt: (b, 0)),
                   pl.BlockSpec((seq, LANES), lambda b, j, pt: (b, 0))],
        scratch_shapes=[pltpu.VMEM((H_A, seq, DH), F32), pltpu.VMEM((nrow, LANES), F32),
                        pltpu.VMEM((nrow, 1), F32), pltpu.VMEM((nrow, 1), F32),
                        pltpu.VMEM((nrow, DH), F32), pltpu.VMEM((LANES, LANES), F32)])
    return pl.pallas_call(
        functools.partial(_fox_decode_body, pps=pps, seq=seq),
        grid_spec=grid_spec,
        out_shape=[jax.ShapeDtypeStruct((n, FOX_W), F32), jax.ShapeDtypeStruct((n, FOX_W), F32),
                   jax.ShapeDtypeStruct((n, LANES), F32)],
        compiler_params=_cparams("arbitrary", "arbitrary"), name="fox_decode",
    )(page_table, p_fox, p_fox, p_fox, p_fl, jnp.tile(q_gain, H_A).reshape(1, FOX_W),
      jnp.tile(k_gain, H_A).reshape(1, FOX_W), bf, ms, ps, pa, asel,
      *([pk] * pps), *([pv] * pps), *([plf] * pps))


def _rw_prep_body(p_ref, init_ref, mu_ref, w0_ref, w2_ref, a0_ref, a2_ref, g2_ref, kk_ref, ka_ref, *rest,
                  tb, first):
    if first:
        r_o, lw_o, k_o, v_o, a_o, b_o, g_o, sh_o, carry_ref = rest
    else:
        v0_ref, v1_ref, v2_ref, vf_ref, r_o, lw_o, k_o, v_o, a_o, b_o, g_o, sh_o, carry_ref = rest

    @pl.when(pl.program_id(1) == 0)
    def _():
        carry_ref[7:8, :] = init_ref[0]

    p = p_ref[...]
    p_prev = _shift_rows(p, 1, carry_ref[...])
    carry_ref[7:8, :] = p[tb - 1:tb, :]
    sh_o[0] = p[tb - 1:tb, :]
    ps = p + (p_prev - p) * mu_ref[...]
    r = ps[:, 0:RW_W]
    k = ps[:, RW_W:2 * RW_W]
    v = ps[:, 2 * RW_W:3 * RW_W]
    x128 = ps[:, 3 * RW_W:3 * RW_W + LANES]
    gd = ps[:, 3 * RW_W + LANES:3 * RW_W + 2 * LANES]
    lane = lax.broadcasted_iota(jnp.int32, (1, LANES), 1)
    xw = jnp.where(lane < DH, jnp.tanh(x128), 0.0).astype(BF16)
    xa = jnp.where(lane < DH, 0.0, x128).astype(BF16)
    w_log = _log_sigmoid(w0_ref[...] + _mm(xw, w2_ref[...])) - 0.5
    lw_o[...] = -jnp.exp(w_log)
    a = _sigmoid(a0_ref[...] + _mm(xa, a2_ref[...]))
    g_o[...] = _mm(_sigmoid(gd).astype(BF16), g2_ref[...])
    if not first:
        gate = _sigmoid(v0_ref[...] + _mm(_mm(v.astype(BF16), v1_ref[...]).astype(BF16), v2_ref[...]))
        v = v + (vf_ref[...] - v) * gate
    kkx = k * kk_ref[...]
    kk = kkx * _head_scale(kkx * kkx, lambda s: lax.rsqrt(s + L2_EPS))
    r_o[...] = r
    k_o[...] = k * (1.0 + (a - 1.0) * ka_ref[...])
    v_o[...] = v
    a_o[...] = -kk
    b_o[...] = kk * a


def _rw_prep(p_rw, shift_prev, w, e, v_first, batch, seq):
    n, cols = p_rw.shape
    tb = min(256, seq)
    nt = seq // tb
    first = e == 0
    pad_rows = lambda m: jnp.concatenate([m, jnp.zeros((LANES - m.shape[0], m.shape[1]), m.dtype)], axis=0)
    w2p = pad_rows(w['rw_w2'][e]).astype(BF16)
    a2p = jnp.concatenate([jnp.zeros((DH, RW_W), F32), w['rw_a2'][e]], axis=0).astype(BF16)
    vec = lambda x: x.reshape(1, -1)
    args = [p_rw, shift_prev.reshape(batch, 1, cols), vec(w['rw_mu'][e]), vec(w['rw_w0'][e]), w2p,
            vec(w['rw_a0'][e]), a2p, w['rw_g2'][e].astype(BF16), vec(w['rw_k_k'][e]), vec(w['rw_k_a'][e])]
    rowspec = pl.BlockSpec((tb, RW_W), lambda b, t: (b * nt + t, 0))
    in_specs = [pl.BlockSpec((tb, cols), lambda b, t: (b * nt + t, 0)),
                pl.BlockSpec((1, 1, cols), lambda b, t: (b, 0, 0)),
                _const_spec((1, cols)), _const_spec((1, RW_W)), _const_spec((LANES, RW_W)),
                _const_spec((1, RW_W)), _const_spec((LANES, RW_W)), _const_spec((LANES, RW_W)),
                _const_spec((1, RW_W)), _const_spec((1, RW_W))]
    if not first:
        v1p = jnp.concatenate([w['rw_v1'][e - 1], jnp.zeros((RW_W, LANES - w['rw_v1'].shape[2]), F32)], axis=1)
        args += [vec(w['rw_v0'][e - 1]), v1p.astype(BF16), pad_rows(w['rw_v2'][e - 1]).astype(BF16), v_first]
        in_specs += [_const_spec((1, RW_W)), _const_spec((RW_W, LANES)), _const_spec((LANES, RW_W)), rowspec]
    outs = pl.pallas_call(
        functools.partial(_rw_prep_body, tb=tb, first=first),
        grid=(batch, nt), in_specs=in_specs,
        out_specs=[rowspec] * 7 + [pl.BlockSpec((1, 1, cols), lambda b, t: (b, 0, 0))],
        out_shape=[jax.ShapeDtypeStruct((n, RW_W), F32)] * 7 + [jax.ShapeDtypeStruct((batch, 1, cols), F32)],
        scratch_shapes=[pltpu.VMEM((8, cols), F32)],
        compiler_params=_cparams("arbitrary", "arbitrary"), name="rw_prep",
    )(*args)
    return outs[:7], outs[7].reshape(batch, cols)


def _pad_chunk(x, rows):
    if x.shape[0] == rows:
        return x
    return jnp.concatenate([x, jnp.zeros((rows - x.shape[0], x.shape[1]), x.dtype)], axis=0)


def _rw_scan_body(r_ref, lw_ref, k_ref, v_ref, a_ref, b_ref, g_ref, rk_ref, lnw_ref, lnb_ref, s0_ref, tri_ref,
                  bd_ref, o_ref, so_ref, s_ref, *, nb, tb):
    @pl.when(pl.program_id(1) == 0)
    def _():
        s_ref[...] = s0_ref[...]

    c = CHUNK
    lane = lax.broadcasted_iota(jnp.int32, (1, LANES), 1)
    m0 = lane < DH
    r2i = lax.broadcasted_iota(jnp.int32, (2 * c, 2 * c), 0)
    c2i = lax.broadcasted_iota(jnp.int32, (2 * c, 2 * c), 1)
    strict = (r2i % c) > (c2i % c)
    lower = (r2i % c) >= (c2i % c)
    tri = tri_ref[...]
    bd = bd_ref[...]

    def stack2(z):
        return jnp.concatenate([jnp.where(m0, z, 0.0), jnp.where(m0, 0.0, z)], axis=0)

    chains = [(n, hp, slice(hp * LANES, (hp + 1) * LANES)) for n in range(nb) for hp in range(RW_W // LANES)]
    load = lambda ref: [_pad_chunk(ref[n, :, cs], c) for n, _, cs in chains]
    lws, rs, ks, vs, as_, bs = load(lw_ref), load(r_ref), load(k_ref), load(v_ref), load(a_ref), load(b_ref)
    cums = [_sel_l(tri, lw) for lw in lws]
    a2s = [stack2(a * jnp.exp(cum - lw)).astype(BF16) for a, cum, lw in zip(as_, cums, lws)]
    r2s = [stack2(r * jnp.exp(cum)).astype(BF16) for r, cum in zip(rs, cums)]
    b2s = [stack2(b * jnp.exp(-cum)).astype(BF16) for b, cum in zip(bs, cums)]
    k2s = [stack2(k * jnp.exp(-cum)).astype(BF16) for k, cum in zip(ks, cums)]
    v2s = [stack2(v) for v in vs]
    ars_in = [jnp.concatenate([a2, r2], axis=0) for a2, r2 in zip(a2s, r2s)]
    lms = [_mm_nt(ar, jnp.concatenate([b2, k2], axis=0)) for ar, b2, k2 in zip(ars_in, b2s, k2s)]
    xs = _inv_unit_lower([jnp.where(strict, lm[0:2 * c, 0:2 * c], 0.0) for lm in lms])
    e_ends = [jnp.exp(cum[c - 1:c, :] - cum) for cum in cums]
    qs = [_mm3_tn(x, stack2(b * e)) for x, b, e in zip(xs, bs, e_ends)]
    wvs = [_mm(jnp.where(strict, lm[0:2 * c, 2 * c:4 * c], 0.0).astype(BF16), v2.astype(BF16))
           for lm, v2 in zip(lms, v2s)]
    ps = [_mm_tn(a2, q.astype(BF16)) for a2, q in zip(a2s, qs)]
    zs = [_mm_tn(jnp.concatenate([wv, v2], axis=0).astype(BF16),
                 jnp.concatenate([q, stack2(k * e)], axis=0).astype(BF16))
          for wv, v2, q, k, e in zip(wvs, v2s, qs, ks, e_ends)]
    sts = [s_ref[n, hp] for n, hp, _ in chains]
    sbs = [st.astype(BF16) for st in sts]
    for (n, hp, _), st, sb, cum, p, z in zip(chains, sts, sbs, cums, ps, zs):
        s_ref[n, hp] = st * jnp.exp(cum[c - 1:c, :]) + _mm(sb, p.astype(BF16)) + z
    arss = [_mm_nt(ar, sb) for ar, sb in zip(ars_in, sbs)]
    u2s = [_mm3(x, ars[0:2 * c, :] + wv) for x, ars, wv in zip(xs, arss, wvs)]
    y2s = [ars[2 * c:4 * c, :]
           + _mm(jnp.concatenate([jnp.where(lower, lm[2 * c:4 * c, 0:2 * c], 0.0),
                                  jnp.where(lower, lm[2 * c:4 * c, 2 * c:4 * c], 0.0)], axis=1).astype(BF16),
                 jnp.concatenate([u2, v2], axis=0).astype(BF16))
           for ars, lm, u2, v2 in zip(arss, lms, u2s, v2s)]
    ys = [(y2[0:c, :] + y2[c:2 * c, :])[0:tb, :] for y2 in y2s]
    mus = [_sel_r(y, bd) * (1.0 / DH) for y in ys]
    ds = [y - mu for y, mu in zip(ys, mus)]
    vars_ = [_sel_r(d * d, bd) * (1.0 / DH) for d in ds]
    bonus = [_sel_r(r_ref[n, :, cs] * k_ref[n, :, cs] * rk_ref[:, cs], bd) for n, _, cs in chains]
    for (n, _, cs), d, var, bo in zip(chains, ds, vars_, bonus):
        yn = d * lax.rsqrt(var + RW_GN_EPS) * lnw_ref[:, cs] + lnb_ref[:, cs]
        o_ref[n, :, cs] = (yn + bo * v_ref[n, :, cs]) * g_ref[n, :, cs]
    so_ref[...] = s_ref[...]


def _rw_scan(parts, w, e, s0_blk, batch, seq):
    nb = 2
    assert batch % nb == 0
    tb = min(CHUNK, seq)
    nt = seq // tb
    npair = RW_W // LANES
    parts = [x.reshape(batch, seq, RW_W) for x in parts]
    idx = jnp.arange(LANES)
    bd = ((idx[:, None] // DH) == (idx[None, :] // DH)).astype(BF16)
    tri = jnp.tril(jnp.ones((CHUNK, CHUNK), F32)).astype(BF16)
    rowspec = pl.BlockSpec((nb, tb, RW_W), lambda bb, t: (bb, t, 0))
    stspec = pl.BlockSpec((nb, npair, LANES, LANES), lambda bb, t: (bb, 0, 0, 0))
    cst = lambda shape: pl.BlockSpec(shape, lambda bb, t: (0,) * len(shape))
    o_b, s_out = pl.pallas_call(
        functools.partial(_rw_scan_body, nb=nb, tb=tb),
        grid=(batch // nb, nt),
        in_specs=[rowspec] * 7 + [cst((1, RW_W))] * 3 + [stspec, cst(tri.shape), cst(bd.shape)],
        out_specs=[rowspec, stspec],
        out_shape=[jax.ShapeDtypeStruct((batch, seq, RW_W), F32), jax.ShapeDtypeStruct(s0_blk.shape, F32)],
        scratch_shapes=[pltpu.VMEM((nb, npair, LANES, LANES), F32)],
        compiler_params=_cparams("arbitrary", "arbitrary"), name="rw_scan",
    )(*parts, w['rw_r_k'][e].reshape(1, RW_W), w['rw_ln_w'][e].reshape(1, RW_W),
      w['rw_ln_b'][e].reshape(1, RW_W), s0_blk, tri, bd)
    return o_b.reshape(batch * seq, RW_W), s_out


def _rw_state_to_blocks(s):
    bsz = s.shape[0]
    s = s.reshape(bsz, H_A // 2, 2, DH, DH)
    z = jnp.zeros_like(s[:, :, 0])
    top = jnp.concatenate([s[:, :, 0], z], axis=-1)
    bot = jnp.concatenate([z, s[:, :, 1]], axis=-1)
    return jnp.concatenate([top, bot], axis=-2)


def _rw_blocks_to_state(sb):
    bsz = sb.shape[0]
    return jnp.stack([sb[:, :, :DH, :DH], sb[:, :, DH:, DH:]], axis=2).reshape(bsz, H_A, DH, DH)


def _gdn_body(q_ref, k_ref, v_ref, z_ref, ba_ref, qi_ref, ki_ref, vi_ref, cwq_ref, cwk_ref, cwv_ref,
              par_ref, nw_ref, s0_ref, tri_ref, sel_ref, o_ref, so_ref, qc_ref, kc_ref, vc_ref,
              s_ref, carry_ref, *, nb, tb):
    @pl.when(pl.program_id(1) == 0)
    def _():
        carry_ref[:, 0, 5:8, :] = qi_ref[...]
        carry_ref[:, 1, 5:8, :] = ki_ref[...]
        carry_ref[:, 2, 5:8, :] = vi_ref[...]
        s_ref[...] = s0_ref[...]

    def conv(x, w_ref, n, idx, out_ref):
        prev = carry_ref[n, idx]
        y = x * w_ref[3:4, :]
        for kk in range(1, 4):
            y = y + _shift_rows(x, kk, prev) * w_ref[3 - kk:4 - kk, :]
        tail = x[tb - 3:tb, :]
        carry_ref[n, idx, 5:8, :] = tail
        out_ref[n] = tail
        return y * _sigmoid(y)

    lane = lax.broadcasted_iota(jnp.int32, (1, LANES), 1)

    def column(x, idx):
        return jnp.sum(jnp.where(lane == idx, x, 0.0), axis=-1, keepdims=True)

    c = CHUNK
    r2i = lax.broadcasted_iota(jnp.int32, (2 * c, 2 * c), 0)
    c2i = lax.broadcasted_iota(jnp.int32, (2 * c, 2 * c), 1)
    same = (r2i // c) == (c2i // c)
    strict = same & (r2i > c2i)
    lower = same & (r2i >= c2i)
    tri = tri_ref[...]
    sel = sel_ref[...]

    stacked = []
    for n in range(nb):
        q = conv(q_ref[n], cwq_ref, n, 0, qc_ref)
        k = conv(k_ref[n], cwk_ref, n, 1, kc_ref)
        v = conv(v_ref[n], cwv_ref, n, 2, vc_ref)
        ba = ba_ref[n]
        beta_all = _sigmoid(ba)
        z_in = ba + par_ref[1:2, :]
        g_all = -jnp.exp(par_ref[0:1, :]) * (jnp.maximum(z_in, 0.0) + jnp.log1p(jnp.exp(-jnp.abs(z_in))))
        for hp in range(H_C // 2):
            heads = []
            for hh in range(2):
                h = 2 * hp + hh
                sl = slice(h * DK, (h + 1) * DK)
                qh = q[:, sl]
                kh = k[:, sl]
                qn = qh * lax.rsqrt(jnp.sum(qh * qh, axis=-1, keepdims=True) + L2_EPS) * (DK ** -0.5)
                kn = kh * lax.rsqrt(jnp.sum(kh * kh, axis=-1, keepdims=True) + L2_EPS)
                beta = jnp.broadcast_to(column(beta_all, h), (tb, DK))
                g = jnp.broadcast_to(column(g_all, H_C + h), (tb, DK))
                heads.append((qn, kn, v[:, sl], beta, g))
            stacked.append([jnp.concatenate([_pad_chunk(heads[0][i], c), _pad_chunk(heads[1][i], c)], axis=0)
                            for i in range(5)])
    chains = [(n, hp) for n in range(nb) for hp in range(H_C // 2)]
    qn2s, kn2s, v2s, b2s, g2s = [[s[i] for s in stacked] for i in range(5)]
    gcs = [_sel_l(tri, g2) for g2 in g2s]
    grows = [_sel_nt(sel, gc) for gc in gcs]
    gammas = [jnp.where(lower, jnp.exp(jnp.where(lower, gc - grow, 0.0)), 0.0) for gc, grow in zip(gcs, grows)]
    kbs = [kn2 * b2 for kn2, b2 in zip(kn2s, b2s)]
    kkqks = [_mm_nt(jnp.concatenate([kb, qn2], axis=0).astype(BF16), kn2.astype(BF16))
             for kb, qn2, kn2 in zip(kbs, qn2s, kn2s)]
    tinvs = _inv_unit_lower([jnp.where(strict, -kkqk[0:2 * c, :] * gamma, 0.0) for kkqk, gamma in zip(kkqks, gammas)])
    egs = [jnp.exp(gc) for gc in gcs]
    uws = [_mm3(tinv, jnp.concatenate([kb * eg, v2 * b2], axis=1))
           for tinv, kb, eg, v2, b2 in zip(tinvs, kbs, egs, v2s, b2s)]
    heads2 = [(ci, hh) for ci in range(len(chains)) for hh in range(2)]
    rows = lambda hh: slice(hh * c, (hh + 1) * c)
    glasts = [gcs[ci][hh * c + c - 1:hh * c + c, :] for ci, hh in heads2]
    pzs = [_mm_tn((kn2s[ci][rows(hh), :] * jnp.exp(gl - gcs[ci][rows(hh), :])).astype(BF16),
                  uws[ci][rows(hh), :].astype(BF16))
           for (ci, hh), gl in zip(heads2, glasts)]
    sts = [s_ref[chains[ci][0], 2 * chains[ci][1] + hh] for ci, hh in heads2]
    sbs = [st.astype(BF16) for st in sts]
    for (ci, hh), st, sb, gl, pz in zip(heads2, sts, sbs, glasts, pzs):
        s_ref[chains[ci][0], 2 * chains[ci][1] + hh] = (st * jnp.exp(gl[:, 0:1])
                                                        - _mm(pz[:, 0:DK].astype(BF16), sb) + pz[:, DK:2 * DK])
    wqs = [_mm(jnp.concatenate([uws[ci][rows(hh), 0:DK], (qn2s[ci] * egs[ci])[rows(hh), :]], axis=0).astype(BF16), sb)
           for (ci, hh), sb in zip(heads2, sbs)]
    o2s = []
    for ci in range(len(chains)):
        vnew = jnp.concatenate([uws[ci][rows(hh), DK:2 * DK] - wqs[2 * ci + hh][0:c, :] for hh in range(2)], axis=0)
        qs = jnp.concatenate([wqs[2 * ci + hh][c:2 * c, :] for hh in range(2)], axis=0)
        amat = jnp.where(lower, kkqks[ci][2 * c:4 * c, :] * gammas[ci], 0.0)
        o2s.append(qs + _mm(amat.astype(BF16), vnew.astype(BF16)))
    for ci, (n, hp) in enumerate(chains):
        for hh in range(2):
            sl = slice((2 * hp + hh) * DK, (2 * hp + hh + 1) * DK)
            oh = o2s[ci][hh * c:hh * c + tb, :]
            zz = z_ref[n, :, sl]
            on = oh * lax.rsqrt(jnp.mean(oh * oh, axis=-1, keepdims=True) + EPS) * nw_ref[...]
            o_ref[n, :, sl] = on * (zz * _sigmoid(zz))
    so_ref[...] = s_ref[...]


def _gdn(p_qkv, p_z, p_ba, conv_prev, s0, w, o_idx, batch, seq):
    nb = 2
    assert batch % nb == 0 and seq >= 3
    tb = min(CHUNK, seq)
    nt = seq // tb
    gk = H_C * DK
    cw = w['gdn_conv_w'][o_idx]
    par = jnp.zeros((8, LANES), F32)
    par = par.at[0, H_C:2 * H_C].set(w['gdn_A_log'][o_idx]).at[1, H_C:2 * H_C].set(w['gdn_dt_bias'][o_idx])
    idx = jnp.arange(2 * CHUNK)
    tri = (((idx[:, None] // CHUNK) == (idx[None, :] // CHUNK)) & (idx[:, None] >= idx[None, :])).astype(BF16)
    sel = jnp.zeros((2 * CHUNK, LANES), F32).at[:, 0].set(1.0).astype(BF16)
    p_qkv = p_qkv.reshape(batch, seq, 3 * gk)
    col = lambda part: pl.BlockSpec((nb, tb, gk), lambda b, t, part=part: (b, t, part))
    prev = lambda part: pl.BlockSpec((nb, 3, gk), lambda b, t, part=part: (b, 0, part))
    wspec = lambda part: pl.BlockSpec((4, gk), lambda b, t, part=part: (0, part))
    cst = lambda shape: pl.BlockSpec(shape, lambda b, t: (0,) * len(shape))
    stspec = pl.BlockSpec((nb, H_C, DK, DK), lambda b, t: (b, 0, 0, 0))
    outs = pl.pallas_call(
        functools.partial(_gdn_body, nb=nb, tb=tb),
        grid=(batch // nb, nt),
        in_specs=[col(0), col(1), col(2), col(0),
                  pl.BlockSpec((nb, tb, LANES), lambda b, t: (b, t, 0)),
                  prev(0), prev(1), prev(2), wspec(0), wspec(1), wspec(2),
                  cst((8, LANES)), cst((1, DK)), stspec, cst(tri.shape), cst(sel.shape)],
        out_specs=[col(0), stspec, prev(0), prev(0), prev(0)],
        out_shape=[jax.ShapeDtypeStruct((batch, seq, gk), F32), jax.ShapeDtypeStruct(s0.shape, F32)]
                  + [jax.ShapeDtypeStruct((batch, 3, gk), F32)] * 3,
        scratch_shapes=[pltpu.VMEM((nb, H_C, DK, DK), F32), pltpu.VMEM((nb, 3, 8, gk), F32)],
        compiler_params=_cparams("arbitrary", "arbitrary"), name="gdn",
    )(p_qkv, p_qkv, p_qkv, p_z.reshape(batch, seq, gk), p_ba.reshape(batch, seq, LANES),
      conv_prev, conv_prev, conv_prev, cw, cw, cw, par, w['gdn_norm_w'][o_idx].reshape(1, DK), s0, tri, sel)
    o_c, s_out, qc, kc, vc = outs
    return o_c.reshape(batch * seq, gk), jnp.concatenate([qc, kc, vc], axis=-1), s_out


def _run_trunk(x, fox_past, rw_state, rw_shift, gdn_state, gdn_conv, ffn_conv, page_table, w, wb):
    batch, seq, d = x.shape
    n = batch * seq
    depth = w['norm_mix'].shape[0]
    tm = 512 if n % 512 == 0 else n
    x = x.reshape(n, d)
    fk, fv, flf, rws, rwsh, gs, gcv, fcv = [], [], [], [], [], [], [], []
    v_first = None
    for layer in range(depth):
        if layer % 2 == 0:
            e = layer // 2
            p_fox, p_rw, p_fl = _norm_matmul(x, w['norm_mix'][layer], wb['ev_in'][e], (4 * FOX_W, 1792, LANES), tm)
            if fox_past is None:
                q_aug, k_aug, v_bf, k_out, lf = _fox_prep(p_fox, p_fl, w['fox_q_gain'][e], w['fox_k_gain'][e],
                                                          w['fox_b_f'][e], batch, seq)
                o_attn = _fox_attn(q_aug, k_aug, v_bf, batch, seq)
            else:
                o_attn, k_out, lf = _fox_decode(p_fox, p_fl, w['fox_q_gain'][e], w['fox_k_gain'][e], w['fox_b_f'][e],
                                                fox_past[0], fox_past[1], fox_past[2], page_table, e, batch, seq)
            parts, sh = _rw_prep(p_rw, rw_shift[e], w, e, v_first, batch, seq)
            if e == 0:
                v_first = parts[3]
            o_b, s_blk = _rw_scan(parts, w, e, _rw_state_to_blocks(rw_state[e]), batch, seq)
            x = _ev_out(o_attn, p_fox, o_b, wb['ev_out'][e], x, tm)
            fk.append(k_out.reshape(batch, seq, H_A, DH))
            fv.append(p_fox[:, 2 * FOX_W:3 * FOX_W].reshape(batch, seq, H_A, DH))
            flf.append(lf[:, :H_A].reshape(batch, seq, H_A))
            rws.append(_rw_blocks_to_state(s_blk))
            rwsh.append(sh)
        else:
            o_idx = layer // 2
            p_qkv, p_z, p_ba = _norm_matmul(x, w['norm_mix'][layer], wb['od_in'][o_idx],
                                            (3 * H_C * DK, H_C * DK, LANES), tm)
            o_c, cv, s_out = _gdn(p_qkv, p_z, p_ba, gdn_conv[o_idx], gdn_state[o_idx], w, o_idx, batch, seq)
            x = _od_out(o_c, wb['od_out'][o_idx], x, tm)
            gs.append(s_out)
            gcv.append(cv)
        x, buf = _ffn(x, w['norm_ffn'][layer], wb['ffn_up'][layer], w['ffn_conv_w'][layer],
                      w['ffn_conv_b'][layer], wb['ffn_down'][layer], ffn_conv[layer], batch, seq)
        fcv.append(buf)
    y = _final_norm(x, w['norm_out'], tm).reshape(batch, seq, d)
    return (y, jnp.stack(fk), jnp.stack(fv), jnp.stack(flf), jnp.stack(rws), jnp.stack(rwsh),
            jnp.stack(gs), jnp.stack(gcv), jnp.stack(fcv))


def _prep_weights(w):
    fox_cols = 4 * FOX_W + H_A
    ev = w['ev_w_in']
    pad = jnp.zeros(ev.shape[:2] + (LANES - H_A,), ev.dtype)
    ev_in = jnp.concatenate([ev[..., :4 * FOX_W], ev[..., fox_cols:], ev[..., 4 * FOX_W:fox_cols], pad], axis=-1)
    od = w['od_w_in']
    pad2 = jnp.zeros(od.shape[:2] + (LANES - 2 * H_C,), od.dtype)
    od_in = jnp.concatenate([od, pad2], axis=-1)
    return dict(ev_in=ev_in.astype(BF16), ev_out=w['ev_w_out'].astype(BF16), od_in=od_in.astype(BF16),
                od_out=w['od_w_out'].astype(BF16), ffn_up=w['ffn_w_up'].astype(BF16),
                ffn_down=w['ffn_w_down'].astype(BF16))


def kernel(x_prompt, x_sample, cache_fox_k, cache_fox_v, cache_fox_logf, state_rwkv, state_rwkv_shift, state_gdn, state_gdn_conv, state_ffn_conv, page_table, norm_mix, norm_ffn, norm_out, ev_w_in, ev_w_out, fox_b_f, fox_q_gain, fox_k_gain, rw_mu, rw_w0, rw_w2, rw_a0, rw_a2, rw_g2, rw_k_k, rw_k_a, rw_r_k, rw_ln_w, rw_ln_b, rw_v0, rw_v1, rw_v2, od_w_in, od_w_out, gdn_conv_w, gdn_A_log, gdn_dt_bias, gdn_norm_w, ffn_w_up, ffn_conv_w, ffn_conv_b, ffn_w_down):
    w = dict(norm_mix=norm_mix, norm_ffn=norm_ffn, norm_out=norm_out, ev_w_in=ev_w_in, ev_w_out=ev_w_out,
             fox_b_f=fox_b_f, fox_q_gain=fox_q_gain, fox_k_gain=fox_k_gain, rw_mu=rw_mu, rw_w0=rw_w0,
             rw_w2=rw_w2, rw_a0=rw_a0, rw_a2=rw_a2, rw_g2=rw_g2, rw_k_k=rw_k_k, rw_k_a=rw_k_a, rw_r_k=rw_r_k,
             rw_ln_w=rw_ln_w, rw_ln_b=rw_ln_b, rw_v0=rw_v0, rw_v1=rw_v1, rw_v2=rw_v2, od_w_in=od_w_in,
             od_w_out=od_w_out, gdn_conv_w=gdn_conv_w, gdn_A_log=gdn_A_log, gdn_dt_bias=gdn_dt_bias,
             gdn_norm_w=gdn_norm_w, ffn_w_up=ffn_w_up, ffn_conv_w=ffn_conv_w, ffn_conv_b=ffn_conv_b,
             ffn_w_down=ffn_w_down)
    wb = _prep_weights(w)
    bp = x_prompt.shape[0]
    n_even, n_odd, depth = ev_w_in.shape[0], od_w_in.shape[0], norm_mix.shape[0]
    rw_cols = rw_mu.shape[1]
    (y_p, fk_p, fv_p, flf_p, rw_p, rwsh_p, gdn_p, gcv_p, fcv_p) = _run_trunk(
        x_prompt, None,
        jnp.zeros((n_even, bp, H_A, DH, DH), F32), jnp.zeros((n_even, bp, rw_cols), F32),
        jnp.zeros((n_odd, bp, H_C, DK, DK), F32), jnp.zeros((n_odd, bp, 3, 3 * H_C * DK), F32),
        jnp.zeros((depth, bp, 2, ffn_w_up.shape[2]), F32), page_table, w, wb)
    (y_s, fk_s, fv_s, flf_s, rw_s, rwsh_s, gdn_s, gcv_s, fcv_s) = _run_trunk(
        x_sample, (cache_fox_k, cache_fox_v, cache_fox_logf), state_rwkv, state_rwkv_shift,
        state_gdn, state_gdn_conv, state_ffn_conv, page_table, w, wb)
    return (y_p, y_s, fk_p, fv_p, flf_p, fk_s, fv_s, flf_s, rw_p, rw_s, rwsh_p, rwsh_s,
            gdn_p, gdn_s, gcv_p, gcv_s, fcv_p, fcv_s)
```

```python
import functools
import math

import jax
import jax.numpy as jnp
from jax import lax
from jax.experimental import pallas as pl
from jax.experimental.pallas import tpu as pltpu

F32 = jnp.float32
BF16 = jnp.bfloat16

EPS = 1e-6
RW_GN_EPS = 64e-5
L2_EPS = 1e-6
NEG_BIG = -1e30

H_A = 8
DH = 64
FOX_W = 512
RW_W = 512
H_C = 8
DK = 128
CHUNK = 64
LANES = 128
VMEM_LIMIT = 56 * 1024 * 1024


def _cparams(*sem):
    return pltpu.CompilerParams(dimension_semantics=sem, vmem_limit_bytes=VMEM_LIMIT)


def _const_spec(shape):
    nd = len(shape)
    return pl.BlockSpec(shape, lambda *_: (0,) * nd, pipeline_mode=pl.Buffered(1))


def _mm(a, b):
    return jnp.dot(a, b, preferred_element_type=F32)


def _mm_nt(a, b):
    return lax.dot_general(a, b, (((1,), (1,)), ((), ())), preferred_element_type=F32)


def _mm_tn(a, b):
    return lax.dot_general(a, b, (((0,), (0,)), ((), ())), preferred_element_type=F32)


def _split3(x):
    hi = x.astype(BF16)
    r = x - hi.astype(F32)
    mid = r.astype(BF16)
    lo = (r - mid.astype(F32)).astype(BF16)
    return hi, mid, lo


def _sel_l(m01, x):
    hi, mid, lo = _split3(x)
    return _mm(m01, hi) + _mm(m01, mid) + _mm(m01, lo)


def _sel_r(x, m01):
    hi, mid, lo = _split3(x)
    return _mm(hi, m01) + _mm(mid, m01) + _mm(lo, m01)


def _sel_nt(m01, x):
    hi, mid, lo = _split3(x)
    return _mm_nt(m01, hi) + _mm_nt(m01, mid) + _mm_nt(m01, lo)


def _log_sigmoid(z):
    return jnp.minimum(z, 0.0) - jnp.log1p(jnp.exp(-jnp.abs(z)))


def _sigmoid(z):
    return 1.0 / (1.0 + jnp.exp(-z))


def _inv_unit_lower(ns):
    size = ns[0].shape[0]
    r = lax.broadcasted_iota(jnp.int32, (size, size), 0)
    c = lax.broadcasted_iota(jnp.int32, (size, size), 1)
    eye = jnp.where(r == c, 1.0, 0.0)
    ps = [eye + n for n in ns]
    nks = list(ns)
    for _ in range(int(math.log2(CHUNK)) - 1):
        nks = [_mm(nk.astype(BF16), nk.astype(BF16)) for nk in nks]
        ps = [p + _mm(p.astype(BF16), nk.astype(BF16)) for p, nk in zip(ps, nks)]
    return ps


def _head_scale(x2, fn):
    lane_h = lax.broadcasted_iota(jnp.int32, (1, x2.shape[1]), 1) // DH
    out = jnp.zeros_like(x2)
    for h in range(x2.shape[1] // DH):
        s = jnp.sum(x2[:, h * DH:(h + 1) * DH], axis=-1, keepdims=True)
        out = jnp.where(lane_h == h, fn(s), out)
    return out


def _rms_heads(x, gain):
    return x * _head_scale(x * x, lambda s: lax.rsqrt(s * (1.0 / DH) + EPS)) * gain


def _shift_rows(x, k, prev):
    row = lax.broadcasted_iota(jnp.int32, x.shape, 0)
    out = pltpu.roll(x, k, 0)
    nprev = prev.shape[0]
    for j in range(k):
        out = jnp.where(row == j, prev[nprev - k + j:nprev - k + j + 1, :], out)
    return out


def _nm_body(x_ref, g_ref, w_ref, *o_refs, splits):
    x = x_ref[...]
    xn = (x * lax.rsqrt(jnp.mean(x * x, axis=-1, keepdims=True) + EPS) * g_ref[...]).astype(BF16)
    off = 0
    for o_ref, n in zip(o_refs, splits):
        o_ref[...] = _mm(xn, w_ref[:, off:off + n])
        off += n


def _norm_matmul(x, g, w_bf, splits, tm):
    n, d = x.shape
    ntot = w_bf.shape[1]
    assert sum(splits) == ntot and n % tm == 0
    return pl.pallas_call(
        functools.partial(_nm_body, splits=tuple(splits)),
        grid=(n // tm,),
        in_specs=[pl.BlockSpec((tm, d), lambda i: (i, 0)), _const_spec((1, d)), _const_spec((d, ntot))],
        out_specs=[pl.BlockSpec((tm, s), lambda i: (i, 0)) for s in splits],
        out_shape=[jax.ShapeDtypeStruct((n, s), F32) for s in splits],
        compiler_params=_cparams("arbitrary"),
        name="norm_matmul",
    )(x, g.reshape(1, d), w_bf)


def _final_norm_body(x_ref, g_ref, o_ref):
    x = x_ref[...]
    o_ref[...] = x * lax.rsqrt(jnp.mean(x * x, axis=-1, keepdims=True) + EPS) * g_ref[...]


def _final_norm(x, g, tm):
    n, d = x.shape
    return pl.pallas_call(
        _final_norm_body, grid=(n // tm,),
        in_specs=[pl.BlockSpec((tm, d), lambda i: (i, 0)), _const_spec((1, d))],
        out_specs=pl.BlockSpec((tm, d), lambda i: (i, 0)),
        out_shape=jax.ShapeDtypeStruct((n, d), F32),
        compiler_params=_cparams("arbitrary"), name="final_norm",
    )(x, g.reshape(1, d))


def _evout_body(oa_ref, og_ref, ob_ref, w_ref, x_ref, o_ref):
    a = (oa_ref[...] * _sigmoid(og_ref[...])).astype(BF16)
    b = ob_ref[...].astype(BF16)
    o_ref[...] = x_ref[...] + _mm(a, w_ref[0:FOX_W, :]) + _mm(b, w_ref[FOX_W:FOX_W + RW_W, :])


def _ev_out(o_attn, p_fox, o_b, w_bf, x, tm):
    n, d = x.shape
    return pl.pallas_call(
        _evout_body, grid=(n // tm,),
        in_specs=[pl.BlockSpec((tm, FOX_W), lambda i: (i, 0)),
                  pl.BlockSpec((tm, FOX_W), lambda i: (i, 3)),
                  pl.BlockSpec((tm, RW_W), lambda i: (i, 0)),
                  _const_spec((FOX_W + RW_W, d)),
                  pl.BlockSpec((tm, d), lambda i: (i, 0))],
        out_specs=pl.BlockSpec((tm, d), lambda i: (i, 0)),
        out_shape=jax.ShapeDtypeStruct((n, d), F32),
        compiler_params=_cparams("arbitrary"), name="ev_out",
    )(o_attn, p_fox, o_b, w_bf, x)


def _odout_body(oc_ref, w_ref, x_ref, o_ref):
    o_ref[...] = x_ref[...] + _mm(oc_ref[...].astype(BF16), w_ref[...])


def _od_out(o_c, w_bf, x, tm):
    n, d = x.shape
    k = o_c.shape[1]
    return pl.pallas_call(
        _odout_body, grid=(n // tm,),
        in_specs=[pl.BlockSpec((tm, k), lambda i: (i, 0)), _const_spec((k, d)),
                  pl.BlockSpec((tm, d), lambda i: (i, 0))],
        out_specs=pl.BlockSpec((tm, d), lambda i: (i, 0)),
        out_shape=jax.ShapeDtypeStruct((n, d), F32),
        compiler_params=_cparams("arbitrary"), name="od_out",
    )(o_c, w_bf, x)


def _ffn_cols(xn, wup_ref, cw_ref, cb_ref, wdn_ref, acc, prev_fn, tail_fn, f, cwb):
    for c in range(f // cwb):
        ys = []
        for half in (0, 1):
            lo = half * f + c * cwb
            h = _mm(xn, wup_ref[:, lo:lo + cwb])
            hm1, hm2 = prev_fn(h, lo)
            ys.append(hm2 * cw_ref[0:1, lo:lo + cwb] + hm1 * cw_ref[1:2, lo:lo + cwb]
                      + h * cw_ref[2:3, lo:lo + cwb] + cb_ref[:, lo:lo + cwb])
            tail_fn(h, lo)
        u, gt = ys
        act = (gt * _sigmoid(gt) * u).astype(BF16)
        acc = acc + _mm(act, wdn_ref[c * cwb:(c + 1) * cwb, :])
    return acc


def _ffn_norm(x_ref, g_ref):
    x = x_ref[...]
    return x, (x * lax.rsqrt(jnp.mean(x * x, axis=-1, keepdims=True) + EPS) * g_ref[...]).astype(BF16)


def _ffn_seq_body(x_ref, g_ref, wup_ref, cw_ref, cb_ref, wdn_ref, init_ref, o_ref, st_ref, carry_ref,
                  *, tb, f, cwb):
    @pl.when(pl.program_id(1) == 0)
    def _():
        carry_ref[0:2, :] = init_ref[0]

    x, xn = _ffn_norm(x_ref, g_ref)
    row = lax.broadcasted_iota(jnp.int32, (tb, cwb), 0)

    def prev_fn(h, lo):
        c0 = carry_ref[0:1, lo:lo + cwb]
        c1 = carry_ref[1:2, lo:lo + cwb]
        hm1 = jnp.where(row == 0, c1, pltpu.roll(h, 1, 0))
        hm2 = jnp.where(row == 0, c0, jnp.where(row == 1, c1, pltpu.roll(h, 2, 0)))
        return hm1, hm2

    def tail_fn(h, lo):
        carry_ref[0:2, lo:lo + cwb] = h[tb - 2:tb, :]
        st_ref[0, :, lo:lo + cwb] = h[tb - 2:tb, :]

    o_ref[...] = _ffn_cols(xn, wup_ref, cw_ref, cb_ref, wdn_ref, x, prev_fn, tail_fn, f, cwb)


def _ffn_flat_body(x_ref, g_ref, wup_ref, cw_ref, cb_ref, wdn_ref, f1_ref, f2_ref, o_ref, st_ref,
                   *, rows, seq, f, cwb):
    x, xn = _ffn_norm(x_ref, g_ref)
    tmod = lax.broadcasted_iota(jnp.int32, (rows, cwb), 0) % seq

    def prev_fn(h, lo):
        hm1 = jnp.where(tmod == 0, f1_ref[:, lo:lo + cwb], pltpu.roll(h, 1, 0))
        hm2 = jnp.where(tmod < 2, f2_ref[:, lo:lo + cwb], pltpu.roll(h, 2, 0))
        return hm1, hm2

    def tail_fn(h, lo):
        st_ref[:, :, lo:lo + cwb] = h.reshape(rows // seq, seq, cwb)[:, seq - 2:seq, :]

    o_ref[...] = _ffn_cols(xn, wup_ref, cw_ref, cb_ref, wdn_ref, x, prev_fn, tail_fn, f, cwb)


def _ffn(x, g, wup_bf, conv_w, conv_b, wdn_bf, conv_prev, batch, seq):
    n, d = x.shape
    f2 = wup_bf.shape[1]
    f = f2 // 2
    cwb = f // 2
    weights = [_const_spec((1, d)), _const_spec((d, f2)), _const_spec((3, f2)), _const_spec((1, f2)),
               _const_spec((f, d))]
    out_shape = [jax.ShapeDtypeStruct((n, d), F32), jax.ShapeDtypeStruct((batch, 2, f2), F32)]
    args = (x, g.reshape(1, d), wup_bf, conv_w, conv_b.reshape(1, f2), wdn_bf)
    if seq >= 256:
        tb = 512 if seq % 512 == 0 else 256
        nt = seq // tb
        return pl.pallas_call(
            functools.partial(_ffn_seq_body, tb=tb, f=f, cwb=cwb),
            grid=(batch, nt),
            in_specs=[pl.BlockSpec((tb, d), lambda b, t: (b * nt + t, 0))] + weights
                     + [pl.BlockSpec((1, 2, f2), lambda b, t: (b, 0, 0))],
            out_specs=[pl.BlockSpec((tb, d), lambda b, t: (b * nt + t, 0)),
                       pl.BlockSpec((1, 2, f2), lambda b, t: (b, 0, 0))],
            out_shape=out_shape,
            scratch_shapes=[pltpu.VMEM((8, f2), F32)],
            compiler_params=_cparams("arbitrary", "arbitrary"), name="ffn_seq",
        )(*args, conv_prev)
    zeros = jnp.zeros((batch, seq - 2, f2), F32)
    fill2 = jnp.concatenate([conv_prev, zeros], axis=1).reshape(n, f2)
    fill1 = jnp.concatenate([conv_prev[:, 1:2], zeros, zeros[:, :1]], axis=1).reshape(n, f2)
    return pl.pallas_call(
        functools.partial(_ffn_flat_body, rows=n, seq=seq, f=f, cwb=cwb),
        grid=(1,),
        in_specs=[pl.BlockSpec((n, d), lambda i: (0, 0))] + weights
                 + [pl.BlockSpec((n, f2), lambda i: (0, 0)), pl.BlockSpec((n, f2), lambda i: (0, 0))],
        out_specs=[pl.BlockSpec((n, d), lambda i: (0, 0)), pl.BlockSpec((batch, 2, f2), lambda i: (0, 0, 0))],
        out_shape=out_shape,
        compiler_params=_cparams("arbitrary"), name="ffn_flat",
    )(*args, fill1, fill2)


def _fox_consts():
    src = jnp.arange(FOX_W)
    place = jnp.zeros((FOX_W, H_A * LANES), F32).at[src, (src // DH) * LANES + src % DH].set(1.0)
    hh = jnp.arange(H_A)
    eq, ek = [], []
    for piece in range(3):
        eq.append(jnp.zeros((LANES, H_A * LANES), F32).at[hh, hh * LANES + DH + piece].set(1.0))
        ek.append(jnp.zeros((LANES, H_A * LANES), F32).at[hh, hh * LANES + DH + 3 + piece].set(-1.0))
    cq = jnp.zeros((1, H_A * LANES), F32)
    ck = jnp.zeros((1, H_A * LANES), F32)
    for piece in range(3):
        cq = cq.at[0, hh * LANES + DH + 3 + piece].set(1.0)
        ck = ck.at[0, hh * LANES + DH + piece].set(1.0)
    return place.astype(BF16), jnp.stack(eq).astype(BF16), jnp.stack(ek).astype(BF16), cq, ck


def _fox_prep_body(q_ref, k_ref, v_ref, fl_ref, qg_ref, kg_ref, bf_ref, pm_ref, pt_ref, eq_ref, ek_ref, cq_ref,
                   ck_ref, vone_ref, tri_ref, qa_ref, ka_ref, vt_ref, ko_ref, lf_ref, carry_ref, *, tm):
    @pl.when(pl.program_id(1) == 0)
    def _():
        carry_ref[...] = jnp.zeros_like(carry_ref)

    qn = _rms_heads(q_ref[...], qg_ref[...]) * (DH ** -0.5)
    kn = _rms_heads(k_ref[...], kg_ref[...])
    lf = _log_sigmoid(fl_ref[...] + bf_ref[...])
    lf_ref[...] = lf
    fcum = _sel_l(tri_ref[...], lf) + carry_ref[0:1, :]
    carry_ref[0:1, :] = fcum[tm - 1:tm, :]
    fh, fm, flo = _split3(fcum)
    pm = pm_ref[...]
    qa = (_mm(qn.astype(BF16), pm) + _mm(fh, eq_ref[0]) + _mm(fm, eq_ref[1]) + _mm(flo, eq_ref[2])
          + cq_ref[...])
    ka = (_mm(kn.astype(BF16), pm) + _mm(fh, ek_ref[0]) + _mm(fm, ek_ref[1]) + _mm(flo, ek_ref[2])
          + ck_ref[...])
    qa_ref[...] = qa.astype(BF16)
    ka_ref[...] = ka.astype(BF16)
    vt = _mm_nt(pt_ref[...], v_ref[...].astype(BF16)) + vone_ref[...]
    vt_ref[0, :, 0] = vt.astype(BF16).reshape(H_A, LANES, tm)
    ko_ref[...] = kn


FOX_TILE = 512


def _fox_prep(p_fox, p_fl, q_gain, k_gain, b_f, batch, seq):
    n = p_fox.shape[0]
    tm = min(FOX_TILE, seq)
    nt = seq // tm
    place, eq, ek, cq, ck = _fox_consts()
    place_t = place.T
    hh = jnp.arange(H_A)
    vone = jnp.zeros((H_A * LANES, 1), F32).at[hh * LANES + DH, 0].set(1.0)
    tri = jnp.tril(jnp.ones((tm, tm), F32)).astype(BF16)
    bf = jnp.zeros((1, LANES), F32).at[0, :H_A].set(b_f)
    row = lambda c: pl.BlockSpec((tm, FOX_W), lambda b, t, c=c: (b * nt + t, c))
    wide = pl.BlockSpec((tm, H_A * LANES), lambda b, t: (b * nt + t, 0))
    narrow = pl.BlockSpec((tm, LANES), lambda b, t: (b * nt + t, 0))
    return pl.pallas_call(
        functools.partial(_fox_prep_body, tm=tm),
        grid=(batch, nt),
        in_specs=[row(0), row(1), row(2), narrow,
                  _const_spec((1, FOX_W)), _const_spec((1, FOX_W)), _const_spec((1, LANES)),
                  _const_spec(place.shape), _const_spec(place_t.shape), _const_spec(eq.shape), _const_spec(ek.shape),
                  _const_spec(cq.shape), _const_spec(ck.shape), _const_spec(vone.shape), _const_spec(tri.shape)],
        out_specs=[wide, wide, pl.BlockSpec((1, H_A, 1, LANES, tm), lambda b, t: (b, 0, t, 0, 0)), row(0), narrow],
        out_shape=[jax.ShapeDtypeStruct((n, H_A * LANES), BF16), jax.ShapeDtypeStruct((n, H_A * LANES), BF16),
                   jax.ShapeDtypeStruct((batch, H_A, nt, LANES, tm), BF16), jax.ShapeDtypeStruct((n, FOX_W), F32),
                   jax.ShapeDtypeStruct((n, LANES), F32)],
        scratch_shapes=[pltpu.VMEM((8, LANES), F32)],
        compiler_params=_cparams("arbitrary", "arbitrary"), name="fox_prep",
    )(p_fox, p_fox, p_fox, p_fl, jnp.tile(q_gain, H_A).reshape(1, FOX_W),
      jnp.tile(k_gain, H_A).reshape(1, FOX_W), bf, place, place_t, eq, ek, cq, ck, vone, tri)


def _fox_attn_body(qa_ref, ka_ref, vt_ref, o_ref, *, tq):
    i = pl.program_id(2)
    key = lax.broadcasted_iota(jnp.int32, (tq, tq), 0)
    qry = lax.broadcasted_iota(jnp.int32, (tq, tq), 1)
    causal = key <= qry
    qs = [qa_ref[:, hh * LANES:(hh + 1) * LANES] for hh in range(2)]

    def block(j, carry, masked):
        off = pl.multiple_of(j * tq, tq)
        sts = [_mm_nt(ka_ref[pl.ds(off, tq), hh * LANES:(hh + 1) * LANES], qs[hh]) for hh in range(2)]
        if masked:
            sts = [jnp.where(causal, st, NEG_BIG) for st in sts]
        m2s = [jnp.maximum(m, jnp.max(st, axis=0, keepdims=True)) for (m, _), st in zip(carry, sts)]
        ps = [jnp.exp(st - m2).astype(BF16) for st, m2 in zip(sts, m2s)]
        return tuple((m2, jnp.exp(m - m2) * acc + _mm(vt_ref[0, hh, j], p))
                     for hh, ((m, acc), m2, p) in enumerate(zip(carry, m2s, ps)))

    init = tuple((jnp.full((1, tq), NEG_BIG, F32), jnp.zeros((LANES, tq), F32)) for _ in range(2))
    carry = lax.fori_loop(0, i, lambda j, c: block(j, c, False), init)
    carry = block(i, carry, True)
    halves = [acc[0:DH, :] / acc[DH:DH + 1, :] for _, acc in carry]
    o_ref[...] = jnp.transpose(jnp.concatenate(halves, axis=0))


def _fox_attn(q_aug, k_aug, v_t, batch, seq):
    n = q_aug.shape[0]
    tq = min(FOX_TILE, seq)
    nq = seq // tq
    return pl.pallas_call(
        functools.partial(_fox_attn_body, tq=tq),
        grid=(batch, H_A // 2, nq),
        in_specs=[pl.BlockSpec((tq, 2 * LANES), lambda b, hp, i: (b * nq + i, hp)),
                  pl.BlockSpec((seq, 2 * LANES), lambda b, hp, i: (b, hp)),
                  pl.BlockSpec((1, 2, nq, LANES, tq), lambda b, hp, i: (b, hp, 0, 0, 0))],
        out_specs=pl.BlockSpec((tq, LANES), lambda b, hp, i: (b * nq + i, hp)),
        out_shape=jax.ShapeDtypeStruct((n, FOX_W), F32),
        compiler_params=_cparams("arbitrary", "arbitrary", "arbitrary"), name="fox_attn",
    )(q_aug, k_aug, v_t)


def _fox_decode_body(pt_ref, q_ref, k_ref, v_ref, fl_ref, qg_ref, kg_ref, bf_ref, ms_ref, ps_ref, pa_ref,
                     asel_ref, *rest, nb, pps, seq):
    del pt_ref
    npg = nb * pps
    kp, vp, lp = rest[:npg], rest[npg:2 * npg], rest[2 * npg:3 * npg]
    o_ref, ko_ref, lfo_ref = rest[3 * npg:3 * npg + 3]
    qh_ref, cb_ref, m_ref, l_ref, acc_ref, car_ref = rest[3 * npg + 3:]
    j = pl.program_id(1)
    nrow = H_A * seq

    @pl.when(j == 0)
    def _new_tokens():
        q_all = _rms_heads(q_ref[...], qg_ref[...]) * (DH ** -0.5)
        k_all = _rms_heads(k_ref[...], kg_ref[...])
        ko_ref[...] = k_all
        lf_all = _log_sigmoid(fl_ref[...] + bf_ref[...])
        lfo_ref[...] = lf_all
        row = lax.broadcasted_iota(jnp.int32, (seq, LANES), 0)
        zpad = jnp.zeros((LANES - seq, DH), F32)
        keyi = lax.broadcasted_iota(jnp.int32, (nrow, LANES), 1)
        ti = lax.broadcasted_iota(jnp.int32, (nrow, LANES), 0) % seq
        for n in range(nb):
            rs = slice(n * seq, (n + 1) * seq)
            q, k, v = q_all[rs, :], k_all[rs, :], v_ref[rs, :]
            c = lf_all[rs, :]
            s = 1
            while s < seq:
                c = c + jnp.where(row >= s, pltpu.roll(c, s, 0), 0.0)
                s *= 2
            srows = []
            for h in range(H_A):
                qh = q[:, h * DH:(h + 1) * DH]
                qh_ref[n, h] = qh
                cb_ref[n, h * seq:(h + 1) * seq, :] = jnp.broadcast_to(c[:, h:h + 1], (seq, LANES))
                kpad = jnp.concatenate([k[:, h * DH:(h + 1) * DH], zpad], axis=0).astype(BF16)
                srows.append(_mm_nt(qh.astype(BF16), kpad))
            cneg = jnp.concatenate([-c, jnp.zeros((LANES - seq, LANES), F32)], axis=0)
            s_new = jnp.concatenate(srows, axis=0) + cb_ref[n] + _sel_nt(asel_ref[0:nrow, :], cneg)
            s_new = jnp.where(keyi <= ti, s_new, NEG_BIG)
            m = jnp.max(s_new, axis=-1, keepdims=True)
            p = jnp.exp(s_new - m)
            m_ref[n] = m
            l_ref[n] = jnp.sum(p, axis=-1, keepdims=True)
            accs = []
            for h in range(H_A):
                vpad = jnp.concatenate([v[:, h * DH:(h + 1) * DH], zpad], axis=0).astype(BF16)
                accs.append(_mm(p[h * seq:(h + 1) * seq, :].astype(BF16), vpad))
            acc_ref[n] = jnp.concatenate(accs, axis=0)
        car_ref[...] = jnp.zeros_like(car_ref)

    seqs = range(nb)
    lfts = [jnp.concatenate([lp[n * pps + i][...] for i in range(pps)]
                            + [jnp.zeros((LANES - pps * H_A, LANES), F32)], axis=0) for n in seqs]
    tots = [jnp.broadcast_to(jnp.sum(lft, axis=1, keepdims=True), (LANES, LANES)) for lft in lfts]
    scores = [[jnp.concatenate([_mm(qh_ref[n, h].astype(BF16), kp[n * pps + i][h].astype(BF16))
                                for h in range(H_A)], axis=0) for i in range(pps)] for n in seqs]
    cars = [car_ref[n] for n in seqs]
    rfulls = [_sel_r(lft, ms_ref[...]) + _sel_l(ps_ref[...], tot) + car for lft, tot, car in zip(lfts, tots, cars)]
    for n in seqs:
        car_ref[n] = cars[n] + _sel_l(pa_ref[...], tots[n])
    biases = [_sel_l(asel_ref[...], rfull) for rfull in rfulls]
    s_alls = [jnp.concatenate([scores[n][i] + biases[n][i * nrow:(i + 1) * nrow, :] + cb_ref[n]
                               for i in range(pps)], axis=1) for n in seqs]
    m_olds = [m_ref[n] for n in seqs]
    m_news = [jnp.maximum(m_old, jnp.max(s_all, axis=-1, keepdims=True)) for m_old, s_all in zip(m_olds, s_alls)]
    probs = [jnp.exp(s_all - m_new) for s_all, m_new in zip(s_alls, m_news)]
    alphas = [jnp.exp(m_old - m_new) for m_old, m_new in zip(m_olds, m_news)]
    pvs = [jnp.concatenate(
        [_mm_nt(probs[n][h * seq:(h + 1) * seq, :].astype(BF16),
                jnp.concatenate([vp[n * pps + i][h] for i in range(pps)], axis=1).astype(BF16))
         for h in range(H_A)], axis=0) for n in seqs]
    for n in seqs:
        m_ref[n] = m_news[n]
        l_ref[n] = alphas[n] * l_ref[n] + jnp.sum(probs[n], axis=-1, keepdims=True)
        acc_ref[n] = alphas[n] * acc_ref[n] + pvs[n]

    @pl.when(j == pl.num_programs(1) - 1)
    def _():
        for n in seqs:
            o = acc_ref[n] / l_ref[n]
            for h in range(H_A):
                o_ref[n * seq:(n + 1) * seq, h * DH:(h + 1) * DH] = o[h * seq:(h + 1) * seq, :]


def _fox_decode(p_fox, p_fl, q_gain, k_gain, b_f, pool_k, pool_v, pool_lf, page_table, e, batch, seq):
    n = p_fox.shape[0]
    n_pages = page_table.shape[1]
    page = pool_lf.shape[2]
    nb = 2
    assert page == LANES and seq == 8 and batch % nb == 0
    pps = 8
    while n_pages % pps:
        pps //= 2
    groups = n_pages // pps
    pk = jnp.transpose(pool_k, (0, 1, 3, 4, 2))
    pv = jnp.transpose(pool_v, (0, 1, 3, 4, 2))
    plf = jnp.transpose(pool_lf, (0, 1, 3, 2))
    nrow = H_A * seq
    idx = jnp.arange(LANES)
    ms = (idx[:, None] > idx[None, :]).astype(BF16)
    same_h = (idx[:, None] % H_A) == (idx[None, :] % H_A)
    valid = (idx[:, None] < pps * H_A) & (idx[None, :] < pps * H_A)
    ps = (same_h & valid & (idx[None, :] // H_A < idx[:, None] // H_A)).astype(BF16)
    pa = (same_h & valid).astype(BF16)
    r = jnp.arange(pps * nrow)
    asel = jnp.zeros((pps * nrow, LANES), F32).at[r, (r // nrow) * H_A + (r % nrow) // seq].set(1.0).astype(BF16)
    bf = jnp.zeros((1, LANES), F32).at[0, :H_A].set(b_f)

    def page_spec(s, i, shape):
        def index(b, j, pt):
            return (e, pt[b * nb + s, n_pages - 1 - (j * pps + i)]) + (0,) * len(shape)
        return pl.BlockSpec((None, None) + shape, index)

    pages = lambda shape: [page_spec(s, i, shape) for s in range(nb) for i in range(pps)]
    rowspec = lambda c: pl.BlockSpec((nb * seq, FOX_W), lambda b, j, pt, c=c: (b, c))
    narrow = pl.BlockSpec((nb * seq, LANES), lambda b, j, pt: (b, 0))
    cst = lambda shape: pl.BlockSpec(shape, lambda b, j, pt: (0,) * len(shape))
    in_specs = ([rowspec(0), rowspec(1), rowspec(2), narrow,
                 cst((1, FOX_W)), cst((1, FOX_W)), cst((1, LANES)), cst(ms.shape), cst(ps.shape), cst(pa.shape),
                 cst(asel.shape)]
                + pages((H_A, DH, page)) + pages((H_A, DH, page)) + pages((H_A, page)))
    grid_spec = pltpu.PrefetchScalarGridSpec(
        num_scalar_prefetch=1, grid=(batch // nb, groups), in_specs=in_specs,
        out_specs=[rowspec(0), rowspec(0), narrow],
        scratch_shapes=[pltpu.VMEM((nb, H_A, seq, DH), F32), pltpu.VMEM((nb, nrow, LANES), F32),
                        pltpu.VMEM((nb, nrow, 1), F32), pltpu.VMEM((nb, nrow, 1), F32),
                        pltpu.VMEM((nb, nrow, DH), F32), pltpu.VMEM((nb, LANES, LANES), F32)])
    return pl.pallas_call(
        functools.partial(_fox_decode_body, nb=nb, pps=pps, seq=seq),
        grid_spec=grid_spec,
        out_shape=[jax.ShapeDtypeStruct((n, FOX_W), F32), jax.ShapeDtypeStruct((n, FOX_W), F32),
                   jax.ShapeDtypeStruct((n, LANES), F32)],
        compiler_params=_cparams("arbitrary", "arbitrary"), name="fox_decode",
    )(page_table, p_fox, p_fox, p_fox, p_fl, jnp.tile(q_gain, H_A).reshape(1, FOX_W),
      jnp.tile(k_gain, H_A).reshape(1, FOX_W), bf, ms, ps, pa, asel,
      *([pk] * (nb * pps)), *([pv] * (nb * pps)), *([plf] * (nb * pps)))


def _rw_prep_body(p_ref, init_ref, mu_ref, w0_ref, w2_ref, a0_ref, a2_ref, g2_ref, kk_ref, ka_ref, *rest,
                  tb, first):
    if first:
        r_o, lw_o, k_o, v_o, a_o, b_o, g_o, sh_o, carry_ref = rest
    else:
        v0_ref, v1_ref, v2_ref, vf_ref, r_o, lw_o, k_o, v_o, a_o, b_o, g_o, sh_o, carry_ref = rest

    @pl.when(pl.program_id(1) == 0)
    def _():
        carry_ref[7:8, :] = init_ref[0]

    p = p_ref[...]
    p_prev = _shift_rows(p, 1, carry_ref[...])
    carry_ref[7:8, :] = p[tb - 1:tb, :]
    sh_o[0] = p[tb - 1:tb, :]
    ps = p + (p_prev - p) * mu_ref[...]
    r = ps[:, 0:RW_W]
    k = ps[:, RW_W:2 * RW_W]
    v = ps[:, 2 * RW_W:3 * RW_W]
    x128 = ps[:, 3 * RW_W:3 * RW_W + LANES]
    gd = ps[:, 3 * RW_W + LANES:3 * RW_W + 2 * LANES]
    lane = lax.broadcasted_iota(jnp.int32, (1, LANES), 1)
    xw = jnp.where(lane < DH, jnp.tanh(x128), 0.0).astype(BF16)
    xa = jnp.where(lane < DH, 0.0, x128).astype(BF16)
    w_log = _log_sigmoid(w0_ref[...] + _mm(xw, w2_ref[...])) - 0.5
    lw_o[...] = -jnp.exp(w_log)
    a = _sigmoid(a0_ref[...] + _mm(xa, a2_ref[...]))
    g_o[...] = _mm(_sigmoid(gd).astype(BF16), g2_ref[...])
    if not first:
        gate = _sigmoid(v0_ref[...] + _mm(_mm(v.astype(BF16), v1_ref[...]).astype(BF16), v2_ref[...]))
        v = v + (vf_ref[...] - v) * gate
    kkx = k * kk_ref[...]
    kk = kkx * _head_scale(kkx * kkx, lambda s: lax.rsqrt(s + L2_EPS))
    r_o[...] = r
    k_o[...] = k * (1.0 + (a - 1.0) * ka_ref[...])
    v_o[...] = v
    a_o[...] = -kk
    b_o[...] = kk * a


def _rw_prep(p_rw, shift_prev, w, e, v_first, batch, seq):
    n, cols = p_rw.shape
    tb = min(256, seq)
    nt = seq // tb
    first = e == 0
    pad_rows = lambda m: jnp.concatenate([m, jnp.zeros((LANES - m.shape[0], m.shape[1]), m.dtype)], axis=0)
    w2p = pad_rows(w['rw_w2'][e]).astype(BF16)
    a2p = jnp.concatenate([jnp.zeros((DH, RW_W), F32), w['rw_a2'][e]], axis=0).astype(BF16)
    vec = lambda x: x.reshape(1, -1)
    args = [p_rw, shift_prev.reshape(batch, 1, cols), vec(w['rw_mu'][e]), vec(w['rw_w0'][e]), w2p,
            vec(w['rw_a0'][e]), a2p, w['rw_g2'][e].astype(BF16), vec(w['rw_k_k'][e]), vec(w['rw_k_a'][e])]
    rowspec = pl.BlockSpec((tb, RW_W), lambda b, t: (b * nt + t, 0))
    in_specs = [pl.BlockSpec((tb, cols), lambda b, t: (b * nt + t, 0)),
                pl.BlockSpec((1, 1, cols), lambda b, t: (b, 0, 0)),
                _const_spec((1, cols)), _const_spec((1, RW_W)), _const_spec((LANES, RW_W)),
                _const_spec((1, RW_W)), _const_spec((LANES, RW_W)), _const_spec((LANES, RW_W)),
                _const_spec((1, RW_W)), _const_spec((1, RW_W))]
    if not first:
        v1p = jnp.concatenate([w['rw_v1'][e - 1], jnp.zeros((RW_W, LANES - w['rw_v1'].shape[2]), F32)], axis=1)
        args += [vec(w['rw_v0'][e - 1]), v1p.astype(BF16), pad_rows(w['rw_v2'][e - 1]).astype(BF16), v_first]
        in_specs += [_const_spec((1, RW_W)), _const_spec((RW_W, LANES)), _const_spec((LANES, RW_W)), rowspec]
    outs = pl.pallas_call(
        functools.partial(_rw_prep_body, tb=tb, first=first),
        grid=(batch, nt), in_specs=in_specs,
        out_specs=[rowspec] * 7 + [pl.BlockSpec((1, 1, cols), lambda b, t: (b, 0, 0))],
        out_shape=[jax.ShapeDtypeStruct((n, RW_W), F32)] * 7 + [jax.ShapeDtypeStruct((batch, 1, cols), F32)],
        scratch_shapes=[pltpu.VMEM((8, cols), F32)],
        compiler_params=_cparams("arbitrary", "arbitrary"), name="rw_prep",
    )(*args)
    return outs[:7], outs[7].reshape(batch, cols)


def _pad_chunk(x, rows):
    if x.shape[0] == rows:
        return x
    return jnp.concatenate([x, jnp.zeros((rows - x.shape[0], x.shape[1]), x.dtype)], axis=0)


def _rw_scan_body(r_ref, lw_ref, k_ref, v_ref, a_ref, b_ref, g_ref, rk_ref, lnw_ref, lnb_ref, s0_ref, tri_ref,
                  o_ref, so_ref, s_ref, *, nb, tb):
    @pl.when(pl.program_id(1) == 0)
    def _():
        s_ref[...] = s0_ref[...]

    c = CHUNK
    lane = lax.broadcasted_iota(jnp.int32, (1, LANES), 1)
    m0 = lane < DH
    r2i = lax.broadcasted_iota(jnp.int32, (2 * c, 2 * c), 0)
    c2i = lax.broadcasted_iota(jnp.int32, (2 * c, 2 * c), 1)
    strict = (r2i % c) > (c2i % c)
    lower = (r2i % c) >= (c2i % c)
    tri = tri_ref[...]

    def stack2(z):
        return jnp.concatenate([jnp.where(m0, z, 0.0), jnp.where(m0, 0.0, z)], axis=0)

    chains = [(n, hp, slice(hp * LANES, (hp + 1) * LANES)) for n in range(nb) for hp in range(RW_W // LANES)]
    load = lambda ref: [_pad_chunk(ref[n, :, cs], c) for n, _, cs in chains]
    lws, rs, ks, vs, as_, bs = load(lw_ref), load(r_ref), load(k_ref), load(v_ref), load(a_ref), load(b_ref)
    cums = [_sel_l(tri, lw) for lw in lws]
    a2s = [stack2(a * jnp.exp(cum - lw)).astype(BF16) for a, cum, lw in zip(as_, cums, lws)]
    r2s = [stack2(r * jnp.exp(cum)).astype(BF16) for r, cum in zip(rs, cums)]
    b2s = [stack2(b * jnp.exp(-cum)).astype(BF16) for b, cum in zip(bs, cums)]
    k2s = [stack2(k * jnp.exp(-cum)).astype(BF16) for k, cum in zip(ks, cums)]
    v2s = [stack2(v) for v in vs]
    ars_in = [jnp.concatenate([a2, r2], axis=0) for a2, r2 in zip(a2s, r2s)]
    lms = [_mm_nt(ar, jnp.concatenate([b2, k2], axis=0)) for ar, b2, k2 in zip(ars_in, b2s, k2s)]
    xs = _inv_unit_lower([jnp.where(strict, lm[0:2 * c, 0:2 * c], 0.0) for lm in lms])
    e_ends = [jnp.exp(cum[c - 1:c, :] - cum) for cum in cums]
    xbs = [x.astype(BF16) for x in xs]
    qs = [_mm_tn(xb, stack2(b * e).astype(BF16)) for xb, b, e in zip(xbs, bs, e_ends)]
    wvs = [_mm(jnp.where(strict, lm[0:2 * c, 2 * c:4 * c], 0.0).astype(BF16), v2.astype(BF16))
           for lm, v2 in zip(lms, v2s)]
    ps = [_mm_tn(a2, q.astype(BF16)) for a2, q in zip(a2s, qs)]
    zs = [_mm_tn(jnp.concatenate([wv, v2], axis=0).astype(BF16),
                 jnp.concatenate([q, stack2(k * e)], axis=0).astype(BF16))
          for wv, v2, q, k, e in zip(wvs, v2s, qs, ks, e_ends)]
    sts = [s_ref[n, hp] for n, hp, _ in chains]
    sbs = [st.astype(BF16) for st in sts]
    for (n, hp, _), st, sb, cum, p, z in zip(chains, sts, sbs, cums, ps, zs):
        s_ref[n, hp] = st * jnp.exp(cum[c - 1:c, :]) + _mm(sb, p.astype(BF16)) + z
    arss = [_mm_nt(ar, sb) for ar, sb in zip(ars_in, sbs)]
    u2s = [_mm(xb, (ars[0:2 * c, :] + wv).astype(BF16)) for xb, ars, wv in zip(xbs, arss, wvs)]
    y2s = [ars[2 * c:4 * c, :]
           + _mm(jnp.concatenate([jnp.where(lower, lm[2 * c:4 * c, 0:2 * c], 0.0),
                                  jnp.where(lower, lm[2 * c:4 * c, 2 * c:4 * c], 0.0)], axis=1).astype(BF16),
                 jnp.concatenate([u2, v2], axis=0).astype(BF16))
           for ars, lm, u2, v2 in zip(arss, lms, u2s, v2s)]
    ys = [(y2[0:c, :] + y2[c:2 * c, :])[0:tb, :] for y2 in y2s]
    def head_sum(x):
        s0 = jnp.sum(jnp.where(m0, x, 0.0), axis=-1, keepdims=True)
        return jnp.where(m0, s0, jnp.sum(x, axis=-1, keepdims=True) - s0)

    mus = [head_sum(y) * (1.0 / DH) for y in ys]
    ds = [y - mu for y, mu in zip(ys, mus)]
    vars_ = [head_sum(d * d) * (1.0 / DH) for d in ds]
    bonus = [head_sum(r_ref[n, :, cs] * k_ref[n, :, cs] * rk_ref[:, cs]) for n, _, cs in chains]
    for (n, _, cs), d, var, bo in zip(chains, ds, vars_, bonus):
        yn = d * lax.rsqrt(var + RW_GN_EPS) * lnw_ref[:, cs] + lnb_ref[:, cs]
        o_ref[n, :, cs] = (yn + bo * v_ref[n, :, cs]) * g_ref[n, :, cs]
    so_ref[...] = s_ref[...]


def _rw_scan(parts, w, e, s0_blk, batch, seq):
    nb = 2
    assert batch % nb == 0
    tb = min(CHUNK, seq)
    nt = seq // tb
    npair = RW_W // LANES
    parts = [x.reshape(batch, seq, RW_W) for x in parts]
    tri = jnp.tril(jnp.ones((CHUNK, CHUNK), F32)).astype(BF16)
    rowspec = pl.BlockSpec((nb, tb, RW_W), lambda bb, t: (bb, t, 0))
    stspec = pl.BlockSpec((nb, npair, LANES, LANES), lambda bb, t: (bb, 0, 0, 0))
    cst = lambda shape: pl.BlockSpec(shape, lambda bb, t: (0,) * len(shape))
    o_b, s_out = pl.pallas_call(
        functools.partial(_rw_scan_body, nb=nb, tb=tb),
        grid=(batch // nb, nt),
        in_specs=[rowspec] * 7 + [cst((1, RW_W))] * 3 + [stspec, cst(tri.shape)],
        out_specs=[rowspec, stspec],
        out_shape=[jax.ShapeDtypeStruct((batch, seq, RW_W), F32), jax.ShapeDtypeStruct(s0_blk.shape, F32)],
        scratch_shapes=[pltpu.VMEM((nb, npair, LANES, LANES), F32)],
        compiler_params=_cparams("arbitrary", "arbitrary"), name="rw_scan",
    )(*parts, w['rw_r_k'][e].reshape(1, RW_W), w['rw_ln_w'][e].reshape(1, RW_W),
      w['rw_ln_b'][e].reshape(1, RW_W), s0_blk, tri)
    return o_b.reshape(batch * seq, RW_W), s_out


def _rw_state_to_blocks(s):
    bsz = s.shape[0]
    s = s.reshape(bsz, H_A // 2, 2, DH, DH)
    z = jnp.zeros_like(s[:, :, 0])
    top = jnp.concatenate([s[:, :, 0], z], axis=-1)
    bot = jnp.concatenate([z, s[:, :, 1]], axis=-1)
    return jnp.concatenate([top, bot], axis=-2)


def _rw_blocks_to_state(sb):
    bsz = sb.shape[0]
    return jnp.stack([sb[:, :, :DH, :DH], sb[:, :, DH:, DH:]], axis=2).reshape(bsz, H_A, DH, DH)


def _gdn_body(q_ref, k_ref, v_ref, z_ref, ba_ref, qi_ref, ki_ref, vi_ref, cwq_ref, cwk_ref, cwv_ref,
              par_ref, nw_ref, s0_ref, tri_ref, o_ref, so_ref, qc_ref, kc_ref, vc_ref,
              s_ref, carry_ref, *, nb, tb):
    @pl.when(pl.program_id(1) == 0)
    def _():
        carry_ref[:, 0, 5:8, :] = qi_ref[...]
        carry_ref[:, 1, 5:8, :] = ki_ref[...]
        carry_ref[:, 2, 5:8, :] = vi_ref[...]
        s_ref[...] = s0_ref[...]

    def conv(x, w_ref, n, idx, out_ref):
        prev = carry_ref[n, idx]
        y = x * w_ref[3:4, :]
        for kk in range(1, 4):
            y = y + _shift_rows(x, kk, prev) * w_ref[3 - kk:4 - kk, :]
        tail = x[tb - 3:tb, :]
        carry_ref[n, idx, 5:8, :] = tail
        out_ref[n] = tail
        return y * _sigmoid(y)

    lane = lax.broadcasted_iota(jnp.int32, (1, LANES), 1)

    def column(x, idx):
        return jnp.sum(jnp.where(lane == idx, x, 0.0), axis=-1, keepdims=True)

    c = CHUNK
    r2i = lax.broadcasted_iota(jnp.int32, (2 * c, 2 * c), 0)
    c2i = lax.broadcasted_iota(jnp.int32, (2 * c, 2 * c), 1)
    same = (r2i // c) == (c2i // c)
    strict = same & (r2i > c2i)
    lower = same & (r2i >= c2i)
    tri = tri_ref[...]

    stacked = []
    for n in range(nb):
        q = conv(q_ref[n], cwq_ref, n, 0, qc_ref)
        k = conv(k_ref[n], cwk_ref, n, 1, kc_ref)
        v = conv(v_ref[n], cwv_ref, n, 2, vc_ref)
        ba = ba_ref[n]
        beta_all = _sigmoid(ba)
        z_in = ba + par_ref[1:2, :]
        g_all = -jnp.exp(par_ref[0:1, :]) * (jnp.maximum(z_in, 0.0) + jnp.log1p(jnp.exp(-jnp.abs(z_in))))
        for hp in range(H_C // 2):
            heads = []
            for hh in range(2):
                h = 2 * hp + hh
                sl = slice(h * DK, (h + 1) * DK)
                qh = q[:, sl]
                kh = k[:, sl]
                qn = qh * lax.rsqrt(jnp.sum(qh * qh, axis=-1, keepdims=True) + L2_EPS) * (DK ** -0.5)
                kn = kh * lax.rsqrt(jnp.sum(kh * kh, axis=-1, keepdims=True) + L2_EPS)
                beta = jnp.broadcast_to(column(beta_all, h), (tb, DK))
                g = jnp.broadcast_to(column(g_all, H_C + h), (tb, DK))
                heads.append((qn, kn, v[:, sl], beta, g))
            stacked.append([jnp.concatenate([_pad_chunk(heads[0][i], c), _pad_chunk(heads[1][i], c)], axis=0)
                            for i in range(5)])
    chains = [(n, hp) for n in range(nb) for hp in range(H_C // 2)]
    qn2s, kn2s, v2s, b2s, g2s = [[s[i] for s in stacked] for i in range(5)]
    gcs = [_sel_l(tri, g2) for g2 in g2s]
    grows = [jnp.transpose(gc) for gc in gcs]
    gammas = [jnp.where(lower, jnp.exp(jnp.where(lower, gc - grow, 0.0)), 0.0) for gc, grow in zip(gcs, grows)]
    kbs = [kn2 * b2 for kn2, b2 in zip(kn2s, b2s)]
    kkqks = [_mm_nt(jnp.concatenate([kb, qn2], axis=0).astype(BF16), kn2.astype(BF16))
             for kb, qn2, kn2 in zip(kbs, qn2s, kn2s)]
    tinvs = _inv_unit_lower([jnp.where(strict, -kkqk[0:2 * c, :] * gamma, 0.0) for kkqk, gamma in zip(kkqks, gammas)])
    egs = [jnp.exp(gc) for gc in gcs]
    uws = [_mm(tinv.astype(BF16), jnp.concatenate([kb * eg, v2 * b2], axis=1).astype(BF16))
           for tinv, kb, eg, v2, b2 in zip(tinvs, kbs, egs, v2s, b2s)]
    heads2 = [(ci, hh) for ci in range(len(chains)) for hh in range(2)]
    rows = lambda hh: slice(hh * c, (hh + 1) * c)
    glasts = [gcs[ci][hh * c + c - 1:hh * c + c, :] for ci, hh in heads2]
    pzs = [_mm_tn((kn2s[ci][rows(hh), :] * jnp.exp(gl - gcs[ci][rows(hh), :])).astype(BF16),
                  uws[ci][rows(hh), :].astype(BF16))
           for (ci, hh), gl in zip(heads2, glasts)]
    sts = [s_ref[chains[ci][0], 2 * chains[ci][1] + hh] for ci, hh in heads2]
    sbs = [st.astype(BF16) for st in sts]
    for (ci, hh), st, sb, gl, pz in zip(heads2, sts, sbs, glasts, pzs):
        s_ref[chains[ci][0], 2 * chains[ci][1] + hh] = (st * jnp.exp(gl[:, 0:1])
                                                        - _mm(pz[:, 0:DK].astype(BF16), sb) + pz[:, DK:2 * DK])
    wqs = [_mm(jnp.concatenate([uws[ci][rows(hh), 0:DK], (qn2s[ci] * egs[ci])[rows(hh), :]], axis=0).astype(BF16), sb)
           for (ci, hh), sb in zip(heads2, sbs)]
    o2s = []
    for ci in range(len(chains)):
        vnew = jnp.concatenate([uws[ci][rows(hh), DK:2 * DK] - wqs[2 * ci + hh][0:c, :] for hh in range(2)], axis=0)
        qs = jnp.concatenate([wqs[2 * ci + hh][c:2 * c, :] for hh in range(2)], axis=0)
        amat = jnp.where(lower, kkqks[ci][2 * c:4 * c, :] * gammas[ci], 0.0)
        o2s.append(qs + _mm(amat.astype(BF16), vnew.astype(BF16)))
    for ci, (n, hp) in enumerate(chains):
        for hh in range(2):
            sl = slice((2 * hp + hh) * DK, (2 * hp + hh + 1) * DK)
            oh = o2s[ci][hh * c:hh * c + tb, :]
            zz = z_ref[n, :, sl]
            on = oh * lax.rsqrt(jnp.mean(oh * oh, axis=-1, keepdims=True) + EPS) * nw_ref[...]
            o_ref[n, :, sl] = on * (zz * _sigmoid(zz))
    so_ref[...] = s_ref[...]


def _gdn(p_qkv, p_z, p_ba, conv_prev, s0, w, o_idx, batch, seq):
    nb = 2
    assert batch % nb == 0 and seq >= 3
    tb = min(CHUNK, seq)
    nt = seq // tb
    gk = H_C * DK
    cw = w['gdn_conv_w'][o_idx]
    par = jnp.zeros((8, LANES), F32)
    par = par.at[0, H_C:2 * H_C].set(w['gdn_A_log'][o_idx]).at[1, H_C:2 * H_C].set(w['gdn_dt_bias'][o_idx])
    idx = jnp.arange(2 * CHUNK)
    tri = (((idx[:, None] // CHUNK) == (idx[None, :] // CHUNK)) & (idx[:, None] >= idx[None, :])).astype(BF16)
    p_qkv = p_qkv.reshape(batch, seq, 3 * gk)
    col = lambda part: pl.BlockSpec((nb, tb, gk), lambda b, t, part=part: (b, t, part))
    prev = lambda part: pl.BlockSpec((nb, 3, gk), lambda b, t, part=part: (b, 0, part))
    wspec = lambda part: pl.BlockSpec((4, gk), lambda b, t, part=part: (0, part))
    cst = lambda shape: pl.BlockSpec(shape, lambda b, t: (0,) * len(shape))
    stspec = pl.BlockSpec((nb, H_C, DK, DK), lambda b, t: (b, 0, 0, 0))
    outs = pl.pallas_call(
        functools.partial(_gdn_body, nb=nb, tb=tb),
        grid=(batch // nb, nt),
        in_specs=[col(0), col(1), col(2), col(0),
                  pl.BlockSpec((nb, tb, LANES), lambda b, t: (b, t, 0)),
                  prev(0), prev(1), prev(2), wspec(0), wspec(1), wspec(2),
                  cst((8, LANES)), cst((1, DK)), stspec, cst(tri.shape)],
        out_specs=[col(0), stspec, prev(0), prev(0), prev(0)],
        out_shape=[jax.ShapeDtypeStruct((batch, seq, gk), F32), jax.ShapeDtypeStruct(s0.shape, F32)]
                  + [jax.ShapeDtypeStruct((batch, 3, gk), F32)] * 3,
        scratch_shapes=[pltpu.VMEM((nb, H_C, DK, DK), F32), pltpu.VMEM((nb, 3, 8, gk), F32)],
        compiler_params=_cparams("arbitrary", "arbitrary"), name="gdn",
    )(p_qkv, p_qkv, p_qkv, p_z.reshape(batch, seq, gk), p_ba.reshape(batch, seq, LANES),
      conv_prev, conv_prev, conv_prev, cw, cw, cw, par, w['gdn_norm_w'][o_idx].reshape(1, DK), s0, tri)
    o_c, s_out, qc, kc, vc = outs
    return o_c.reshape(batch * seq, gk), jnp.concatenate([qc, kc, vc], axis=-1), s_out


def _run_trunk(x, fox_past, rw_state, rw_shift, gdn_state, gdn_conv, ffn_conv, page_table, w, wb):
    batch, seq, d = x.shape
    n = batch * seq
    depth = w['norm_mix'].shape[0]
    tm = 512 if n % 512 == 0 else n
    x = x.reshape(n, d)
    fk, fv, flf, rws, rwsh, gs, gcv, fcv = [], [], [], [], [], [], [], []
    v_first = None
    for layer in range(depth):
        if layer % 2 == 0:
            e = layer // 2
            p_fox, p_rw, p_fl = _norm_matmul(x, w['norm_mix'][layer], wb['ev_in'][e], (4 * FOX_W, 1792, LANES), tm)
            if fox_past is None:
                q_aug, k_aug, v_bf, k_out, lf = _fox_prep(p_fox, p_fl, w['fox_q_gain'][e], w['fox_k_gain'][e],
                                                          w['fox_b_f'][e], batch, seq)
                o_attn = _fox_attn(q_aug, k_aug, v_bf, batch, seq)
            else:
                o_attn, k_out, lf = _fox_decode(p_fox, p_fl, w['fox_q_gain'][e], w['fox_k_gain'][e], w['fox_b_f'][e],
                                                fox_past[0], fox_past[1], fox_past[2], page_table, e, batch, seq)
            parts, sh = _rw_prep(p_rw, rw_shift[e], w, e, v_first, batch, seq)
            if e == 0:
                v_first = parts[3]
            o_b, s_blk = _rw_scan(parts, w, e, _rw_state_to_blocks(rw_state[e]), batch, seq)
            x = _ev_out(o_attn, p_fox, o_b, wb['ev_out'][e], x, tm)
            fk.append(k_out.reshape(batch, seq, H_A, DH))
            fv.append(p_fox[:, 2 * FOX_W:3 * FOX_W].reshape(batch, seq, H_A, DH))
            flf.append(lf[:, :H_A].reshape(batch, seq, H_A))
            rws.append(_rw_blocks_to_state(s_blk))
            rwsh.append(sh)
        else:
            o_idx = layer // 2
            p_qkv, p_z, p_ba = _norm_matmul(x, w['norm_mix'][layer], wb['od_in'][o_idx],
                                            (3 * H_C * DK, H_C * DK, LANES), tm)
            o_c, cv, s_out = _gdn(p_qkv, p_z, p_ba, gdn_conv[o_idx], gdn_state[o_idx], w, o_idx, batch, seq)
            x = _od_out(o_c, wb['od_out'][o_idx], x, tm)
            gs.append(s_out)
            gcv.append(cv)
        x, buf = _ffn(x, w['norm_ffn'][layer], wb['ffn_up'][layer], w['ffn_conv_w'][layer],
                      w['ffn_conv_b'][layer], wb['ffn_down'][layer], ffn_conv[layer], batch, seq)
        fcv.append(buf)
    y = _final_norm(x, w['norm_out'], tm).reshape(batch, seq, d)
    return (y, jnp.stack(fk), jnp.stack(fv), jnp.stack(flf), jnp.stack(rws), jnp.stack(rwsh),
            jnp.stack(gs), jnp.stack(gcv), jnp.stack(fcv))


def _prep_weights(w):
    fox_cols = 4 * FOX_W + H_A
    ev = w['ev_w_in']
    pad = jnp.zeros(ev.shape[:2] + (LANES - H_A,), ev.dtype)
    ev_in = jnp.concatenate([ev[..., :4 * FOX_W], ev[..., fox_cols:], ev[..., 4 * FOX_W:fox_cols], pad], axis=-1)
    od = w['od_w_in']
    pad2 = jnp.zeros(od.shape[:2] + (LANES - 2 * H_C,), od.dtype)
    od_in = jnp.concatenate([od, pad2], axis=-1)
    return dict(ev_in=ev_in.astype(BF16), ev_out=w['ev_w_out'].astype(BF16), od_in=od_in.astype(BF16),
                od_out=w['od_w_out'].astype(BF16), ffn_up=w['ffn_w_up'].astype(BF16),
                ffn_down=w['ffn_w_down'].astype(BF16))


def kernel(x_prompt, x_sample, cache_fox_k, cache_fox_v, cache_fox_logf, state_rwkv, state_rwkv_shift, state_gdn, state_gdn_conv, state_ffn_conv, page_table, norm_mix, norm_ffn, norm_out, ev_w_in, ev_w_out, fox_b_f, fox_q_gain, fox_k_gain, rw_mu, rw_w0, rw_w2, rw_a0, rw_a2, rw_g2, rw_k_k, rw_k_a, rw_r_k, rw_ln_w, rw_ln_b, rw_v0, rw_v1, rw_v2, od_w_in, od_w_out, gdn_conv_w, gdn_A_log, gdn_dt_bias, gdn_norm_w, ffn_w_up, ffn_conv_w, ffn_conv_b, ffn_w_down):
    w = dict(norm_mix=norm_mix, norm_ffn=norm_ffn, norm_out=norm_out, ev_w_in=ev_w_in, ev_w_out=ev_w_out,
             fox_b_f=fox_b_f, fox_q_gain=fox_q_gain, fox_k_gain=fox_k_gain, rw_mu=rw_mu, rw_w0=rw_w0,
             rw_w2=rw_w2, rw_a0=rw_a0, rw_a2=rw_a2, rw_g2=rw_g2, rw_k_k=rw_k_k, rw_k_a=rw_k_a, rw_r_k=rw_r_k,
             rw_ln_w=rw_ln_w, rw_ln_b=rw_ln_b, rw_v0=rw_v0, rw_v1=rw_v1, rw_v2=rw_v2, od_w_in=od_w_in,
             od_w_out=od_w_out, gdn_conv_w=gdn_conv_w, gdn_A_log=gdn_A_log, gdn_dt_bias=gdn_dt_bias,
             gdn_norm_w=gdn_norm_w, ffn_w_up=ffn_w_up, ffn_conv_w=ffn_conv_w, ffn_conv_b=ffn_conv_b,
             ffn_w_down=ffn_w_down)
    wb = _prep_weights(w)
    bp = x_prompt.shape[0]
    n_even, n_odd, depth = ev_w_in.shape[0], od_w_in.shape[0], norm_mix.shape[0]
    rw_cols = rw_mu.shape[1]
    (y_p, fk_p, fv_p, flf_p, rw_p, rwsh_p, gdn_p, gcv_p, fcv_p) = _run_trunk(
        x_prompt, None,
        jnp.zeros((n_even, bp, H_A, DH, DH), F32), jnp.zeros((n_even, bp, rw_cols), F32),
        jnp.zeros((n_odd, bp, H_C, DK, DK), F32), jnp.zeros((n_odd, bp, 3, 3 * H_C * DK), F32),
        jnp.zeros((depth, bp, 2, ffn_w_up.shape[2]), F32), page_table, w, wb)
    (y_s, fk_s, fv_s, flf_s, rw_s, rwsh_s, gdn_s, gcv_s, fcv_s) = _run_trunk(
        x_sample, (cache_fox_k, cache_fox_v, cache_fox_logf), state_rwkv, state_rwkv_shift,
        state_gdn, state_gdn_conv, state_ffn_conv, page_table, w, wb)
    return (y_p, y_s, fk_p, fv_p, flf_p, fk_s, fv_s, flf_s, rw_p, rw_s, rwsh_p, rwsh_s,
            gdn_p, gdn_s, gcv_p, gcv_s, fcv_p, fcv_s)
```

```python
import functools
import math

import jax
import jax.numpy as jnp
from jax import lax
from jax.experimental import pallas as pl
from jax.experimental.pallas import tpu as pltpu

F32 = jnp.float32
BF16 = jnp.bfloat16

EPS = 1e-6
RW_GN_EPS = 64e-5
L2_EPS = 1e-6
NEG_BIG = -1e30

H_A = 8
DH = 64
FOX_W = 512
RW_W = 512
H_C = 8
DK = 128
CHUNK = 64
LANES = 128
VMEM_LIMIT = 56 * 1024 * 1024


def _cparams(*sem):
    return pltpu.CompilerParams(dimension_semantics=sem, vmem_limit_bytes=VMEM_LIMIT)


def _const_spec(shape):
    nd = len(shape)
    return pl.BlockSpec(shape, lambda *_: (0,) * nd, pipeline_mode=pl.Buffered(1))


def _mm(a, b):
    return jnp.dot(a, b, preferred_element_type=F32)


def _mm_nt(a, b):
    return lax.dot_general(a, b, (((1,), (1,)), ((), ())), preferred_element_type=F32)


def _mm_tn(a, b):
    return lax.dot_general(a, b, (((0,), (0,)), ((), ())), preferred_element_type=F32)


def _split3(x):
    hi = x.astype(BF16)
    r = x - hi.astype(F32)
    mid = r.astype(BF16)
    lo = (r - mid.astype(F32)).astype(BF16)
    return hi, mid, lo


def _sel_l(m01, x):
    hi, mid, lo = _split3(x)
    return _mm(m01, hi) + _mm(m01, mid) + _mm(m01, lo)


def _sel_r(x, m01):
    hi, mid, lo = _split3(x)
    return _mm(hi, m01) + _mm(mid, m01) + _mm(lo, m01)


def _sel_nt(m01, x):
    hi, mid, lo = _split3(x)
    return _mm_nt(m01, hi) + _mm_nt(m01, mid) + _mm_nt(m01, lo)


def _log_sigmoid(z):
    return jnp.minimum(z, 0.0) - jnp.log1p(jnp.exp(-jnp.abs(z)))


def _sigmoid(z):
    return 1.0 / (1.0 + jnp.exp(-z))


def _inv_unit_lower(ns):
    size = ns[0].shape[0]
    r = lax.broadcasted_iota(jnp.int32, (size, size), 0)
    c = lax.broadcasted_iota(jnp.int32, (size, size), 1)
    eye = jnp.where(r == c, 1.0, 0.0)
    ps = [eye + n for n in ns]
    nks = list(ns)
    for _ in range(int(math.log2(CHUNK)) - 1):
        nks = [_mm(nk.astype(BF16), nk.astype(BF16)) for nk in nks]
        ps = [p + _mm(p.astype(BF16), nk.astype(BF16)) for p, nk in zip(ps, nks)]
    return ps


def _head_scale(x2, fn):
    lane_h = lax.broadcasted_iota(jnp.int32, (1, x2.shape[1]), 1) // DH
    out = jnp.zeros_like(x2)
    for h in range(x2.shape[1] // DH):
        s = jnp.sum(x2[:, h * DH:(h + 1) * DH], axis=-1, keepdims=True)
        out = jnp.where(lane_h == h, fn(s), out)
    return out


def _rms_heads(x, gain):
    return x * _head_scale(x * x, lambda s: lax.rsqrt(s * (1.0 / DH) + EPS)) * gain


def _shift_rows(x, k, prev):
    row = lax.broadcasted_iota(jnp.int32, x.shape, 0)
    out = pltpu.roll(x, k, 0)
    nprev = prev.shape[0]
    for j in range(k):
        out = jnp.where(row == j, prev[nprev - k + j:nprev - k + j + 1, :], out)
    return out


def _nm_body(x_ref, g_ref, w_ref, *o_refs, splits):
    x = x_ref[...]
    xn = (x * lax.rsqrt(jnp.mean(x * x, axis=-1, keepdims=True) + EPS) * g_ref[...]).astype(BF16)
    off = 0
    for o_ref, n in zip(o_refs, splits):
        o_ref[...] = _mm(xn, w_ref[:, off:off + n])
        off += n


def _norm_matmul(x, g, w_bf, splits, tm):
    n, d = x.shape
    ntot = w_bf.shape[1]
    assert sum(splits) == ntot and n % tm == 0
    return pl.pallas_call(
        functools.partial(_nm_body, splits=tuple(splits)),
        grid=(n // tm,),
        in_specs=[pl.BlockSpec((tm, d), lambda i: (i, 0)), _const_spec((1, d)), _const_spec((d, ntot))],
        out_specs=[pl.BlockSpec((tm, s), lambda i: (i, 0)) for s in splits],
        out_shape=[jax.ShapeDtypeStruct((n, s), F32) for s in splits],
        compiler_params=_cparams("arbitrary"),
        name="norm_matmul",
    )(x, g.reshape(1, d), w_bf)


def _final_norm_body(x_ref, g_ref, o_ref):
    x = x_ref[...]
    o_ref[...] = x * lax.rsqrt(jnp.mean(x * x, axis=-1, keepdims=True) + EPS) * g_ref[...]


def _final_norm(x, g, tm):
    n, d = x.shape
    return pl.pallas_call(
        _final_norm_body, grid=(n // tm,),
        in_specs=[pl.BlockSpec((tm, d), lambda i: (i, 0)), _const_spec((1, d))],
        out_specs=pl.BlockSpec((tm, d), lambda i: (i, 0)),
        out_shape=jax.ShapeDtypeStruct((n, d), F32),
        compiler_params=_cparams("arbitrary"), name="final_norm",
    )(x, g.reshape(1, d))


def _evout_body(oa_ref, og_ref, ob_ref, w_ref, x_ref, o_ref):
    a = (oa_ref[...] * _sigmoid(og_ref[...])).astype(BF16)
    b = ob_ref[...].astype(BF16)
    o_ref[...] = x_ref[...] + _mm(a, w_ref[0:FOX_W, :]) + _mm(b, w_ref[FOX_W:FOX_W + RW_W, :])


def _ev_out(o_attn, p_fox, o_b, w_bf, x, tm):
    n, d = x.shape
    return pl.pallas_call(
        _evout_body, grid=(n // tm,),
        in_specs=[pl.BlockSpec((tm, FOX_W), lambda i: (i, 0)),
                  pl.BlockSpec((tm, FOX_W), lambda i: (i, 3)),
                  pl.BlockSpec((tm, RW_W), lambda i: (i, 0)),
                  _const_spec((FOX_W + RW_W, d)),
                  pl.BlockSpec((tm, d), lambda i: (i, 0))],
        out_specs=pl.BlockSpec((tm, d), lambda i: (i, 0)),
        out_shape=jax.ShapeDtypeStruct((n, d), F32),
        compiler_params=_cparams("arbitrary"), name="ev_out",
    )(o_attn, p_fox, o_b, w_bf, x)


def _odout_body(oc_ref, w_ref, x_ref, o_ref):
    o_ref[...] = x_ref[...] + _mm(oc_ref[...].astype(BF16), w_ref[...])


def _od_out(o_c, w_bf, x, tm):
    n, d = x.shape
    k = o_c.shape[1]
    return pl.pallas_call(
        _odout_body, grid=(n // tm,),
        in_specs=[pl.BlockSpec((tm, k), lambda i: (i, 0)), _const_spec((k, d)),
                  pl.BlockSpec((tm, d), lambda i: (i, 0))],
        out_specs=pl.BlockSpec((tm, d), lambda i: (i, 0)),
        out_shape=jax.ShapeDtypeStruct((n, d), F32),
        compiler_params=_cparams("arbitrary"), name="od_out",
    )(o_c, w_bf, x)


def _ffn_cols(xn, wup_ref, cw_ref, cb_ref, wdn_ref, acc, prev_fn, tail_fn, f, cwb):
    for c in range(f // cwb):
        ys = []
        for half in (0, 1):
            lo = half * f + c * cwb
            h = _mm(xn, wup_ref[:, lo:lo + cwb])
            hm1, hm2 = prev_fn(h, lo)
            ys.append(hm2 * cw_ref[0:1, lo:lo + cwb] + hm1 * cw_ref[1:2, lo:lo + cwb]
                      + h * cw_ref[2:3, lo:lo + cwb] + cb_ref[:, lo:lo + cwb])
            tail_fn(h, lo)
        u, gt = ys
        act = (gt * _sigmoid(gt) * u).astype(BF16)
        acc = acc + _mm(act, wdn_ref[c * cwb:(c + 1) * cwb, :])
    return acc


def _ffn_norm(x_ref, g_ref):
    x = x_ref[...]
    return x, (x * lax.rsqrt(jnp.mean(x * x, axis=-1, keepdims=True) + EPS) * g_ref[...]).astype(BF16)


def _ffn_seq_body(x_ref, g_ref, wup_ref, cw_ref, cb_ref, wdn_ref, init_ref, o_ref, st_ref, carry_ref,
                  *, tb, f, cwb):
    @pl.when(pl.program_id(1) == 0)
    def _():
        carry_ref[0:2, :] = init_ref[0]

    x, xn = _ffn_norm(x_ref, g_ref)
    row = lax.broadcasted_iota(jnp.int32, (tb, cwb), 0)

    def prev_fn(h, lo):
        c0 = carry_ref[0:1, lo:lo + cwb]
        c1 = carry_ref[1:2, lo:lo + cwb]
        hm1 = jnp.where(row == 0, c1, pltpu.roll(h, 1, 0))
        hm2 = jnp.where(row == 0, c0, jnp.where(row == 1, c1, pltpu.roll(h, 2, 0)))
        return hm1, hm2

    def tail_fn(h, lo):
        carry_ref[0:2, lo:lo + cwb] = h[tb - 2:tb, :]
        st_ref[0, :, lo:lo + cwb] = h[tb - 2:tb, :]

    o_ref[...] = _ffn_cols(xn, wup_ref, cw_ref, cb_ref, wdn_ref, x, prev_fn, tail_fn, f, cwb)


def _ffn_flat_body(x_ref, g_ref, wup_ref, cw_ref, cb_ref, wdn_ref, f1_ref, f2_ref, o_ref, st_ref,
                   *, rows, seq, f, cwb):
    x, xn = _ffn_norm(x_ref, g_ref)
    tmod = lax.broadcasted_iota(jnp.int32, (rows, cwb), 0) % seq

    def prev_fn(h, lo):
        hm1 = jnp.where(tmod == 0, f1_ref[:, lo:lo + cwb], pltpu.roll(h, 1, 0))
        hm2 = jnp.where(tmod < 2, f2_ref[:, lo:lo + cwb], pltpu.roll(h, 2, 0))
        return hm1, hm2

    def tail_fn(h, lo):
        st_ref[:, :, lo:lo + cwb] = h.reshape(rows // seq, seq, cwb)[:, seq - 2:seq, :]

    o_ref[...] = _ffn_cols(xn, wup_ref, cw_ref, cb_ref, wdn_ref, x, prev_fn, tail_fn, f, cwb)


def _ffn(x, g, wup_bf, conv_w, conv_b, wdn_bf, conv_prev, batch, seq):
    n, d = x.shape
    f2 = wup_bf.shape[1]
    f = f2 // 2
    cwb = f // 2
    weights = [_const_spec((1, d)), _const_spec((d, f2)), _const_spec((3, f2)), _const_spec((1, f2)),
               _const_spec((f, d))]
    out_shape = [jax.ShapeDtypeStruct((n, d), F32), jax.ShapeDtypeStruct((batch, 2, f2), F32)]
    args = (x, g.reshape(1, d), wup_bf, conv_w, conv_b.reshape(1, f2), wdn_bf)
    if seq >= 256:
        tb = 512 if seq % 512 == 0 else 256
        nt = seq // tb
        return pl.pallas_call(
            functools.partial(_ffn_seq_body, tb=tb, f=f, cwb=cwb),
            grid=(batch, nt),
            in_specs=[pl.BlockSpec((tb, d), lambda b, t: (b * nt + t, 0))] + weights
                     + [pl.BlockSpec((1, 2, f2), lambda b, t: (b, 0, 0))],
            out_specs=[pl.BlockSpec((tb, d), lambda b, t: (b * nt + t, 0)),
                       pl.BlockSpec((1, 2, f2), lambda b, t: (b, 0, 0))],
            out_shape=out_shape,
            scratch_shapes=[pltpu.VMEM((8, f2), F32)],
            compiler_params=_cparams("arbitrary", "arbitrary"), name="ffn_seq",
        )(*args, conv_prev)
    zeros = jnp.zeros((batch, seq - 2, f2), F32)
    fill2 = jnp.concatenate([conv_prev, zeros], axis=1).reshape(n, f2)
    fill1 = jnp.concatenate([conv_prev[:, 1:2], zeros, zeros[:, :1]], axis=1).reshape(n, f2)
    return pl.pallas_call(
        functools.partial(_ffn_flat_body, rows=n, seq=seq, f=f, cwb=cwb),
        grid=(1,),
        in_specs=[pl.BlockSpec((n, d), lambda i: (0, 0))] + weights
                 + [pl.BlockSpec((n, f2), lambda i: (0, 0)), pl.BlockSpec((n, f2), lambda i: (0, 0))],
        out_specs=[pl.BlockSpec((n, d), lambda i: (0, 0)), pl.BlockSpec((batch, 2, f2), lambda i: (0, 0, 0))],
        out_shape=out_shape,
        compiler_params=_cparams("arbitrary"), name="ffn_flat",
    )(*args, fill1, fill2)


def _fox_consts():
    src = jnp.arange(FOX_W)
    place = jnp.zeros((FOX_W, H_A * LANES), F32).at[src, (src // DH) * LANES + src % DH].set(1.0)
    hh = jnp.arange(H_A)
    eq, ek = [], []
    for piece in range(3):
        eq.append(jnp.zeros((LANES, H_A * LANES), F32).at[hh, hh * LANES + DH + piece].set(1.0))
        ek.append(jnp.zeros((LANES, H_A * LANES), F32).at[hh, hh * LANES + DH + 3 + piece].set(-1.0))
    cq = jnp.zeros((1, H_A * LANES), F32)
    ck = jnp.zeros((1, H_A * LANES), F32)
    for piece in range(3):
        cq = cq.at[0, hh * LANES + DH + 3 + piece].set(1.0)
        ck = ck.at[0, hh * LANES + DH + piece].set(1.0)
    return place.astype(BF16), jnp.stack(eq).astype(BF16), jnp.stack(ek).astype(BF16), cq, ck


def _fox_prep_body(q_ref, k_ref, v_ref, fl_ref, qg_ref, kg_ref, bf_ref, pm_ref, pt_ref, eq_ref, ek_ref, cq_ref,
                   ck_ref, vone_ref, tri_ref, qa_ref, ka_ref, vt_ref, ko_ref, lf_ref, carry_ref, *, tm):
    @pl.when(pl.program_id(1) == 0)
    def _():
        carry_ref[...] = jnp.zeros_like(carry_ref)

    qn = _rms_heads(q_ref[...], qg_ref[...]) * (DH ** -0.5)
    kn = _rms_heads(k_ref[...], kg_ref[...])
    lf = _log_sigmoid(fl_ref[...] + bf_ref[...])
    lf_ref[...] = lf
    fcum = _sel_l(tri_ref[...], lf) + carry_ref[0:1, :]
    carry_ref[0:1, :] = fcum[tm - 1:tm, :]
    fh, fm, flo = _split3(fcum)
    pm = pm_ref[...]
    qa = (_mm(qn.astype(BF16), pm) + _mm(fh, eq_ref[0]) + _mm(fm, eq_ref[1]) + _mm(flo, eq_ref[2])
          + cq_ref[...])
    ka = (_mm(kn.astype(BF16), pm) + _mm(fh, ek_ref[0]) + _mm(fm, ek_ref[1]) + _mm(flo, ek_ref[2])
          + ck_ref[...])
    qa_ref[...] = qa.astype(BF16)
    ka_ref[...] = ka.astype(BF16)
    vt = _mm_nt(pt_ref[...], v_ref[...].astype(BF16)) + vone_ref[...]
    vt_ref[0, :, 0] = vt.astype(BF16).reshape(H_A, LANES, tm)
    ko_ref[...] = kn


FOX_TILE = 512


def _fox_tile(seq):
    return min(FOX_TILE, seq // 2)


def _fox_prep(p_fox, p_fl, q_gain, k_gain, b_f, batch, seq):
    n = p_fox.shape[0]
    tm = _fox_tile(seq)
    nt = seq // tm
    place, eq, ek, cq, ck = _fox_consts()
    place_t = place.T
    hh = jnp.arange(H_A)
    vone = jnp.zeros((H_A * LANES, 1), F32).at[hh * LANES + DH, 0].set(1.0)
    tri = jnp.tril(jnp.ones((tm, tm), F32)).astype(BF16)
    bf = jnp.zeros((1, LANES), F32).at[0, :H_A].set(b_f)
    row = lambda c: pl.BlockSpec((tm, FOX_W), lambda b, t, c=c: (b * nt + t, c))
    wide = pl.BlockSpec((tm, H_A * LANES), lambda b, t: (b * nt + t, 0))
    narrow = pl.BlockSpec((tm, LANES), lambda b, t: (b * nt + t, 0))
    return pl.pallas_call(
        functools.partial(_fox_prep_body, tm=tm),
        grid=(batch, nt),
        in_specs=[row(0), row(1), row(2), narrow,
                  _const_spec((1, FOX_W)), _const_spec((1, FOX_W)), _const_spec((1, LANES)),
                  _const_spec(place.shape), _const_spec(place_t.shape), _const_spec(eq.shape), _const_spec(ek.shape),
                  _const_spec(cq.shape), _const_spec(ck.shape), _const_spec(vone.shape), _const_spec(tri.shape)],
        out_specs=[wide, wide, pl.BlockSpec((1, H_A, 1, LANES, tm), lambda b, t: (b, 0, t, 0, 0)), row(0), narrow],
        out_shape=[jax.ShapeDtypeStruct((n, H_A * LANES), BF16), jax.ShapeDtypeStruct((n, H_A * LANES), BF16),
                   jax.ShapeDtypeStruct((batch, H_A, nt, LANES, tm), BF16), jax.ShapeDtypeStruct((n, FOX_W), F32),
                   jax.ShapeDtypeStruct((n, LANES), F32)],
        scratch_shapes=[pltpu.VMEM((8, LANES), F32)],
        compiler_params=_cparams("arbitrary", "arbitrary"), name="fox_prep",
    )(p_fox, p_fox, p_fox, p_fl, jnp.tile(q_gain, H_A).reshape(1, FOX_W),
      jnp.tile(k_gain, H_A).reshape(1, FOX_W), bf, place, place_t, eq, ek, cq, ck, vone, tri)


def _fox_attn_body(qa_ref, ka_ref, vt_ref, o_ref, *, tq):
    i = pl.program_id(2)
    key = lax.broadcasted_iota(jnp.int32, (tq, tq), 0)
    qry = lax.broadcasted_iota(jnp.int32, (tq, tq), 1)
    causal = key <= qry
    chains = [(sb, hh) for sb in range(2) for hh in range(2)]
    qs = [qa_ref[sb * tq:(sb + 1) * tq, hh * LANES:(hh + 1) * LANES] for sb, hh in chains]

    def block(j, carry, active):
        off = pl.multiple_of(j * tq, tq)
        ks = [ka_ref[pl.ds(off, tq), hh * LANES:(hh + 1) * LANES] for hh in range(2)]
        vts = [vt_ref[0, hh, j] for hh in range(2)]
        sts = {c: _mm_nt(ks[chains[c][1]], qs[c]) for c in active}
        sts = {c: jnp.where(causal, st, NEG_BIG) if active[c] else st for c, st in sts.items()}
        m2s = {c: jnp.maximum(carry[c][0], jnp.max(st, axis=0, keepdims=True)) for c, st in sts.items()}
        ps = {c: jnp.exp(st - m2s[c]).astype(BF16) for c, st in sts.items()}
        return tuple((m2s[c], jnp.exp(carry[c][0] - m2s[c]) * carry[c][1] + _mm(vts[chains[c][1]], ps[c]))
                     if c in active else carry[c] for c in range(len(chains)))

    init = tuple((jnp.full((1, tq), NEG_BIG, F32), jnp.zeros((LANES, tq), F32)) for _ in chains)
    carry = lax.fori_loop(0, 2 * i, lambda j, c: block(j, c, {0: False, 1: False, 2: False, 3: False}), init)
    carry = block(2 * i, carry, {0: True, 1: True, 2: False, 3: False})
    carry = block(2 * i + 1, carry, {2: True, 3: True})
    for sb in range(2):
        halves = [acc[0:DH, :] / acc[DH:DH + 1, :] for _, acc in carry[2 * sb:2 * sb + 2]]
        o_ref[sb * tq:(sb + 1) * tq, :] = jnp.transpose(jnp.concatenate(halves, axis=0))


def _fox_attn(q_aug, k_aug, v_t, batch, seq):
    n = q_aug.shape[0]
    tq = _fox_tile(seq)
    nq = seq // (2 * tq)
    return pl.pallas_call(
        functools.partial(_fox_attn_body, tq=tq),
        grid=(batch, H_A // 2, nq),
        in_specs=[pl.BlockSpec((2 * tq, 2 * LANES), lambda b, hp, i: (b * nq + i, hp)),
                  pl.BlockSpec((seq, 2 * LANES), lambda b, hp, i: (b, hp)),
                  pl.BlockSpec((1, 2, seq // tq, LANES, tq), lambda b, hp, i: (b, hp, 0, 0, 0))],
        out_specs=pl.BlockSpec((2 * tq, LANES), lambda b, hp, i: (b * nq + i, hp)),
        out_shape=jax.ShapeDtypeStruct((n, FOX_W), F32),
        compiler_params=_cparams("arbitrary", "arbitrary", "arbitrary"), name="fox_attn",
    )(q_aug, k_aug, v_t)


def _fox_decode_body(pt_ref, q_ref, k_ref, v_ref, fl_ref, qg_ref, kg_ref, bf_ref, ms_ref, ps_ref, pa_ref,
                     asel_ref, *rest, nb, pps, seq):
    del pt_ref
    npg = nb * pps
    kp, vp, lp = rest[:npg], rest[npg:2 * npg], rest[2 * npg:3 * npg]
    o_ref, ko_ref, lfo_ref = rest[3 * npg:3 * npg + 3]
    qh_ref, cb_ref, m_ref, l_ref, acc_ref, car_ref = rest[3 * npg + 3:]
    j = pl.program_id(1)
    nrow = H_A * seq

    @pl.when(j == 0)
    def _new_tokens():
        q_all = _rms_heads(q_ref[...], qg_ref[...]) * (DH ** -0.5)
        k_all = _rms_heads(k_ref[...], kg_ref[...])
        ko_ref[...] = k_all
        lf_all = _log_sigmoid(fl_ref[...] + bf_ref[...])
        lfo_ref[...] = lf_all
        row = lax.broadcasted_iota(jnp.int32, (seq, LANES), 0)
        zpad = jnp.zeros((LANES - seq, DH), F32)
        keyi = lax.broadcasted_iota(jnp.int32, (nrow, LANES), 1)
        ti = lax.broadcasted_iota(jnp.int32, (nrow, LANES), 0) % seq
        for n in range(nb):
            rs = slice(n * seq, (n + 1) * seq)
            q, k, v = q_all[rs, :], k_all[rs, :], v_ref[rs, :]
            c = lf_all[rs, :]
            s = 1
            while s < seq:
                c = c + jnp.where(row >= s, pltpu.roll(c, s, 0), 0.0)
                s *= 2
            srows = []
            for h in range(H_A):
                qh = q[:, h * DH:(h + 1) * DH]
                qh_ref[n, h] = qh
                cb_ref[n, h * seq:(h + 1) * seq, :] = jnp.broadcast_to(c[:, h:h + 1], (seq, LANES))
                kpad = jnp.concatenate([k[:, h * DH:(h + 1) * DH], zpad], axis=0).astype(BF16)
                srows.append(_mm_nt(qh.astype(BF16), kpad))
            cneg = jnp.concatenate([-c, jnp.zeros((LANES - seq, LANES), F32)], axis=0)
            s_new = jnp.concatenate(srows, axis=0) + cb_ref[n] + _sel_nt(asel_ref[0:nrow, :], cneg)
            s_new = jnp.where(keyi <= ti, s_new, NEG_BIG)
            m = jnp.max(s_new, axis=-1, keepdims=True)
            p = jnp.exp(s_new - m)
            m_ref[n] = m
            l_ref[n] = jnp.sum(p, axis=-1, keepdims=True)
            accs = []
            for h in range(H_A):
                vpad = jnp.concatenate([v[:, h * DH:(h + 1) * DH], zpad], axis=0).astype(BF16)
                accs.append(_mm(p[h * seq:(h + 1) * seq, :].astype(BF16), vpad))
            acc_ref[n] = jnp.concatenate(accs, axis=0)
        car_ref[...] = jnp.zeros_like(car_ref)

    seqs = range(nb)
    lfts = [jnp.concatenate([lp[n * pps + i][...] for i in range(pps)]
                            + [jnp.zeros((LANES - pps * H_A, LANES), F32)], axis=0) for n in seqs]
    tots = [jnp.broadcast_to(jnp.sum(lft, axis=1, keepdims=True), (LANES, LANES)) for lft in lfts]
    scores = [[jnp.concatenate([_mm(qh_ref[n, h].astype(BF16), kp[n * pps + i][h].astype(BF16))
                                for h in range(H_A)], axis=0) for i in range(pps)] for n in seqs]
    cars = [car_ref[n] for n in seqs]
    rfulls = [_sel_r(lft, ms_ref[...]) + _sel_l(ps_ref[...], tot) + car for lft, tot, car in zip(lfts, tots, cars)]
    for n in seqs:
        car_ref[n] = cars[n] + _sel_l(pa_ref[...], tots[n])
    biases = [_sel_l(asel_ref[...], rfull) for rfull in rfulls]
    s_alls = [jnp.concatenate([scores[n][i] + biases[n][i * nrow:(i + 1) * nrow, :] + cb_ref[n]
                               for i in range(pps)], axis=1) for n in seqs]
    m_olds = [m_ref[n] for n in seqs]
    m_news = [jnp.maximum(m_old, jnp.max(s_all, axis=-1, keepdims=True)) for m_old, s_all in zip(m_olds, s_alls)]
    probs = [jnp.exp(s_all - m_new) for s_all, m_new in zip(s_alls, m_news)]
    alphas = [jnp.exp(m_old - m_new) for m_old, m_new in zip(m_olds, m_news)]
    pvs = [jnp.concatenate(
        [_mm_nt(probs[n][h * seq:(h + 1) * seq, :].astype(BF16),
                jnp.concatenate([vp[n * pps + i][h] for i in range(pps)], axis=1).astype(BF16))
         for h in range(H_A)], axis=0) for n in seqs]
    for n in seqs:
        m_ref[n] = m_news[n]
        l_ref[n] = alphas[n] * l_ref[n] + jnp.sum(probs[n], axis=-1, keepdims=True)
        acc_ref[n] = alphas[n] * acc_ref[n] + pvs[n]

    @pl.when(j == pl.num_programs(1) - 1)
    def _():
        for n in seqs:
            o = acc_ref[n] / l_ref[n]
            for h in range(H_A):
                o_ref[n * seq:(n + 1) * seq, h * DH:(h + 1) * DH] = o[h * seq:(h + 1) * seq, :]


def _fox_decode(p_fox, p_fl, q_gain, k_gain, b_f, pool_k, pool_v, pool_lf, page_table, e, batch, seq):
    n = p_fox.shape[0]
    n_pages = page_table.shape[1]
    page = pool_lf.shape[2]
    nb = 2
    assert page == LANES and seq == 8 and batch % nb == 0
    pps = 8
    while n_pages % pps:
        pps //= 2
    groups = n_pages // pps
    pk = jnp.transpose(pool_k, (0, 1, 3, 4, 2))
    pv = jnp.transpose(pool_v, (0, 1, 3, 4, 2))
    plf = jnp.transpose(pool_lf, (0, 1, 3, 2))
    nrow = H_A * seq
    idx = jnp.arange(LANES)
    ms = (idx[:, None] > idx[None, :]).astype(BF16)
    same_h = (idx[:, None] % H_A) == (idx[None, :] % H_A)
    valid = (idx[:, None] < pps * H_A) & (idx[None, :] < pps * H_A)
    ps = (same_h & valid & (idx[None, :] // H_A < idx[:, None] // H_A)).astype(BF16)
    pa = (same_h & valid).astype(BF16)
    r = jnp.arange(pps * nrow)
    asel = jnp.zeros((pps * nrow, LANES), F32).at[r, (r // nrow) * H_A + (r % nrow) // seq].set(1.0).astype(BF16)
    bf = jnp.zeros((1, LANES), F32).at[0, :H_A].set(b_f)

    def page_spec(s, i, shape):
        def index(b, j, pt):
            return (e, pt[b * nb + s, n_pages - 1 - (j * pps + i)]) + (0,) * len(shape)
        return pl.BlockSpec((None, None) + shape, index)

    pages = lambda shape: [page_spec(s, i, shape) for s in range(nb) for i in range(pps)]
    rowspec = lambda c: pl.BlockSpec((nb * seq, FOX_W), lambda b, j, pt, c=c: (b, c))
    narrow = pl.BlockSpec((nb * seq, LANES), lambda b, j, pt: (b, 0))
    cst = lambda shape: pl.BlockSpec(shape, lambda b, j, pt: (0,) * len(shape))
    in_specs = ([rowspec(0), rowspec(1), rowspec(2), narrow,
                 cst((1, FOX_W)), cst((1, FOX_W)), cst((1, LANES)), cst(ms.shape), cst(ps.shape), cst(pa.shape),
                 cst(asel.shape)]
                + pages((H_A, DH, page)) + pages((H_A, DH, page)) + pages((H_A, page)))
    grid_spec = pltpu.PrefetchScalarGridSpec(
        num_scalar_prefetch=1, grid=(batch // nb, groups), in_specs=in_specs,
        out_specs=[rowspec(0), rowspec(0), narrow],
        scratch_shapes=[pltpu.VMEM((nb, H_A, seq, DH), F32), pltpu.VMEM((nb, nrow, LANES), F32),
                        pltpu.VMEM((nb, nrow, 1), F32), pltpu.VMEM((nb, nrow, 1), F32),
                        pltpu.VMEM((nb, nrow, DH), F32), pltpu.VMEM((nb, LANES, LANES), F32)])
    return pl.pallas_call(
        functools.partial(_fox_decode_body, nb=nb, pps=pps, seq=seq),
        grid_spec=grid_spec,
        out_shape=[jax.ShapeDtypeStruct((n, FOX_W), F32), jax.ShapeDtypeStruct((n, FOX_W), F32),
                   jax.ShapeDtypeStruct((n, LANES), F32)],
        compiler_params=_cparams("arbitrary", "arbitrary"), name="fox_decode",
    )(page_table, p_fox, p_fox, p_fox, p_fl, jnp.tile(q_gain, H_A).reshape(1, FOX_W),
      jnp.tile(k_gain, H_A).reshape(1, FOX_W), bf, ms, ps, pa, asel,
      *([pk] * (nb * pps)), *([pv] * (nb * pps)), *([plf] * (nb * pps)))


def _rw_prep_body(p_ref, init_ref, mu_ref, w0_ref, w2_ref, a0_ref, a2_ref, g2_ref, kk_ref, ka_ref, *rest,
                  tb, first):
    if first:
        r_o, lw_o, k_o, v_o, a_o, b_o, g_o, sh_o, carry_ref = rest
    else:
        v0_ref, v1_ref, v2_ref, vf_ref, r_o, lw_o, k_o, v_o, a_o, b_o, g_o, sh_o, carry_ref = rest

    @pl.when(pl.program_id(1) == 0)
    def _():
        carry_ref[7:8, :] = init_ref[0]

    p = p_ref[...]
    p_prev = _shift_rows(p, 1, carry_ref[...])
    carry_ref[7:8, :] = p[tb - 1:tb, :]
    sh_o[0] = p[tb - 1:tb, :]
    ps = p + (p_prev - p) * mu_ref[...]
    r = ps[:, 0:RW_W]
    k = ps[:, RW_W:2 * RW_W]
    v = ps[:, 2 * RW_W:3 * RW_W]
    x128 = ps[:, 3 * RW_W:3 * RW_W + LANES]
    gd = ps[:, 3 * RW_W + LANES:3 * RW_W + 2 * LANES]
    lane = lax.broadcasted_iota(jnp.int32, (1, LANES), 1)
    xw = jnp.where(lane < DH, jnp.tanh(x128), 0.0).astype(BF16)
    xa = jnp.where(lane < DH, 0.0, x128).astype(BF16)
    w_log = _log_sigmoid(w0_ref[...] + _mm(xw, w2_ref[...])) - 0.5
    lw_o[...] = -jnp.exp(w_log)
    a = _sigmoid(a0_ref[...] + _mm(xa, a2_ref[...]))
    g_o[...] = _mm(_sigmoid(gd).astype(BF16), g2_ref[...])
    if not first:
        gate = _sigmoid(v0_ref[...] + _mm(_mm(v.astype(BF16), v1_ref[...]).astype(BF16), v2_ref[...]))
        v = v + (vf_ref[...] - v) * gate
    kkx = k * kk_ref[...]
    kk = kkx * _head_scale(kkx * kkx, lambda s: lax.rsqrt(s + L2_EPS))
    r_o[...] = r
    k_o[...] = k * (1.0 + (a - 1.0) * ka_ref[...])
    v_o[...] = v
    a_o[...] = -kk
    b_o[...] = kk * a


def _rw_prep(p_rw, shift_prev, w, e, v_first, batch, seq):
    n, cols = p_rw.shape
    tb = min(256, seq)
    nt = seq // tb
    first = e == 0
    pad_rows = lambda m: jnp.concatenate([m, jnp.zeros((LANES - m.shape[0], m.shape[1]), m.dtype)], axis=0)
    w2p = pad_rows(w['rw_w2'][e]).astype(BF16)
    a2p = jnp.concatenate([jnp.zeros((DH, RW_W), F32), w['rw_a2'][e]], axis=0).astype(BF16)
    vec = lambda x: x.reshape(1, -1)
    args = [p_rw, shift_prev.reshape(batch, 1, cols), vec(w['rw_mu'][e]), vec(w['rw_w0'][e]), w2p,
            vec(w['rw_a0'][e]), a2p, w['rw_g2'][e].astype(BF16), vec(w['rw_k_k'][e]), vec(w['rw_k_a'][e])]
    rowspec = pl.BlockSpec((tb, RW_W), lambda b, t: (b * nt + t, 0))
    in_specs = [pl.BlockSpec((tb, cols), lambda b, t: (b * nt + t, 0)),
                pl.BlockSpec((1, 1, cols), lambda b, t: (b, 0, 0)),
                _const_spec((1, cols)), _const_spec((1, RW_W)), _const_spec((LANES, RW_W)),
                _const_spec((1, RW_W)), _const_spec((LANES, RW_W)), _const_spec((LANES, RW_W)),
                _const_spec((1, RW_W)), _const_spec((1, RW_W))]
    if not first:
        v1p = jnp.concatenate([w['rw_v1'][e - 1], jnp.zeros((RW_W, LANES - w['rw_v1'].shape[2]), F32)], axis=1)
        args += [vec(w['rw_v0'][e - 1]), v1p.astype(BF16), pad_rows(w['rw_v2'][e - 1]).astype(BF16), v_first]
        in_specs += [_const_spec((1, RW_W)), _const_spec((RW_W, LANES)), _const_spec((LANES, RW_W)), rowspec]
    outs = pl.pallas_call(
        functools.partial(_rw_prep_body, tb=tb, first=first),
        grid=(batch, nt), in_specs=in_specs,
        out_specs=[rowspec] * 7 + [pl.BlockSpec((1, 1, cols), lambda b, t: (b, 0, 0))],
        out_shape=[jax.ShapeDtypeStruct((n, RW_W), F32)] * 7 + [jax.ShapeDtypeStruct((batch, 1, cols), F32)],
        scratch_shapes=[pltpu.VMEM((8, cols), F32)],
        compiler_params=_cparams("arbitrary", "arbitrary"), name="rw_prep",
    )(*args)
    return outs[:7], outs[7].reshape(batch, cols)


def _pad_chunk(x, rows):
    if x.shape[0] == rows:
        return x
    return jnp.concatenate([x, jnp.zeros((rows - x.shape[0], x.shape[1]), x.dtype)], axis=0)


def _rw_scan_body(r_ref, lw_ref, k_ref, v_ref, a_ref, b_ref, g_ref, rk_ref, lnw_ref, lnb_ref, s0_ref, tri_ref,
                  o_ref, so_ref, s_ref, *, nb, tb):
    @pl.when(pl.program_id(1) == 0)
    def _():
        s_ref[...] = s0_ref[...]

    c = CHUNK
    lane = lax.broadcasted_iota(jnp.int32, (1, LANES), 1)
    m0 = lane < DH
    r2i = lax.broadcasted_iota(jnp.int32, (2 * c, 2 * c), 0)
    c2i = lax.broadcasted_iota(jnp.int32, (2 * c, 2 * c), 1)
    strict = (r2i % c) > (c2i % c)
    lower = (r2i % c) >= (c2i % c)
    tri = tri_ref[...]

    def stack2(z):
        return jnp.concatenate([jnp.where(m0, z, 0.0), jnp.where(m0, 0.0, z)], axis=0)

    chains = [(n, hp, slice(hp * LANES, (hp + 1) * LANES)) for n in range(nb) for hp in range(RW_W // LANES)]
    load = lambda ref: [_pad_chunk(ref[n, :, cs], c) for n, _, cs in chains]
    lws, rs, ks, vs, as_, bs = load(lw_ref), load(r_ref), load(k_ref), load(v_ref), load(a_ref), load(b_ref)
    cums = [_sel_l(tri, lw) for lw in lws]
    a2s = [stack2(a * jnp.exp(cum - lw)).astype(BF16) for a, cum, lw in zip(as_, cums, lws)]
    r2s = [stack2(r * jnp.exp(cum)).astype(BF16) for r, cum in zip(rs, cums)]
    b2s = [stack2(b * jnp.exp(-cum)).astype(BF16) for b, cum in zip(bs, cums)]
    k2s = [stack2(k * jnp.exp(-cum)).astype(BF16) for k, cum in zip(ks, cums)]
    v2s = [stack2(v) for v in vs]
    ars_in = [jnp.concatenate([a2, r2], axis=0) for a2, r2 in zip(a2s, r2s)]
    lms = [_mm_nt(ar, jnp.concatenate([b2, k2], axis=0)) for ar, b2, k2 in zip(ars_in, b2s, k2s)]
    xs = _inv_unit_lower([jnp.where(strict, lm[0:2 * c, 0:2 * c], 0.0) for lm in lms])
    e_ends = [jnp.exp(cum[c - 1:c, :] - cum) for cum in cums]
    xbs = [x.astype(BF16) for x in xs]
    qs = [_mm_tn(xb, stack2(b * e).astype(BF16)) for xb, b, e in zip(xbs, bs, e_ends)]
    wvs = [_mm(jnp.where(strict, lm[0:2 * c, 2 * c:4 * c], 0.0).astype(BF16), v2.astype(BF16))
           for lm, v2 in zip(lms, v2s)]
    ps = [_mm_tn(a2, q.astype(BF16)) for a2, q in zip(a2s, qs)]
    zs = [_mm_tn(jnp.concatenate([wv, v2], axis=0).astype(BF16),
                 jnp.concatenate([q, stack2(k * e)], axis=0).astype(BF16))
          for wv, v2, q, k, e in zip(wvs, v2s, qs, ks, e_ends)]
    sts = [s_ref[n, hp] for n, hp, _ in chains]
    sbs = [st.astype(BF16) for st in sts]
    for (n, hp, _), st, sb, cum, p, z in zip(chains, sts, sbs, cums, ps, zs):
        s_ref[n, hp] = st * jnp.exp(cum[c - 1:c, :]) + _mm(sb, p.astype(BF16)) + z
    arss = [_mm_nt(ar, sb) for ar, sb in zip(ars_in, sbs)]
    u2s = [_mm(xb, (ars[0:2 * c, :] + wv).astype(BF16)) for xb, ars, wv in zip(xbs, arss, wvs)]
    y2s = [ars[2 * c:4 * c, :]
           + _mm(jnp.concatenate([jnp.where(lower, lm[2 * c:4 * c, 0:2 * c], 0.0),
                                  jnp.where(lower, lm[2 * c:4 * c, 2 * c:4 * c], 0.0)], axis=1).astype(BF16),
                 jnp.concatenate([u2, v2], axis=0).astype(BF16))
           for ars, lm, u2, v2 in zip(arss, lms, u2s, v2s)]
    ys = [(y2[0:c, :] + y2[c:2 * c, :])[0:tb, :] for y2 in y2s]
    def head_sum(x):
        s0 = jnp.sum(jnp.where(m0, x, 0.0), axis=-1, keepdims=True)
        return jnp.where(m0, s0, jnp.sum(x, axis=-1, keepdims=True) - s0)

    mus = [head_sum(y) * (1.0 / DH) for y in ys]
    ds = [y - mu for y, mu in zip(ys, mus)]
    vars_ = [head_sum(d * d) * (1.0 / DH) for d in ds]
    bonus = [head_sum(r_ref[n, :, cs] * k_ref[n, :, cs] * rk_ref[:, cs]) for n, _, cs in chains]
    for (n, _, cs), d, var, bo in zip(chains, ds, vars_, bonus):
        yn = d * lax.rsqrt(var + RW_GN_EPS) * lnw_ref[:, cs] + lnb_ref[:, cs]
        o_ref[n, :, cs] = (yn + bo * v_ref[n, :, cs]) * g_ref[n, :, cs]
    so_ref[...] = s_ref[...]


def _rw_scan(parts, w, e, s0_blk, batch, seq):
    nb = 2
    assert batch % nb == 0
    tb = min(CHUNK, seq)
    nt = seq // tb
    npair = RW_W // LANES
    parts = [x.reshape(batch, seq, RW_W) for x in parts]
    tri = jnp.tril(jnp.ones((CHUNK, CHUNK), F32)).astype(BF16)
    rowspec = pl.BlockSpec((nb, tb, RW_W), lambda bb, t: (bb, t, 0))
    stspec = pl.BlockSpec((nb, npair, LANES, LANES), lambda bb, t: (bb, 0, 0, 0))
    cst = lambda shape: pl.BlockSpec(shape, lambda bb, t: (0,) * len(shape))
    o_b, s_out = pl.pallas_call(
        functools.partial(_rw_scan_body, nb=nb, tb=tb),
        grid=(batch // nb, nt),
        in_specs=[rowspec] * 7 + [cst((1, RW_W))] * 3 + [stspec, cst(tri.shape)],
        out_specs=[rowspec, stspec],
        out_shape=[jax.ShapeDtypeStruct((batch, seq, RW_W), F32), jax.ShapeDtypeStruct(s0_blk.shape, F32)],
        scratch_shapes=[pltpu.VMEM((nb, npair, LANES, LANES), F32)],
        compiler_params=_cparams("arbitrary", "arbitrary"), name="rw_scan",
    )(*parts, w['rw_r_k'][e].reshape(1, RW_W), w['rw_ln_w'][e].reshape(1, RW_W),
      w['rw_ln_b'][e].reshape(1, RW_W), s0_blk, tri)
    return o_b.reshape(batch * seq, RW_W), s_out


def _rw_state_to_blocks(s):
    bsz = s.shape[0]
    s = s.reshape(bsz, H_A // 2, 2, DH, DH)
    z = jnp.zeros_like(s[:, :, 0])
    top = jnp.concatenate([s[:, :, 0], z], axis=-1)
    bot = jnp.concatenate([z, s[:, :, 1]], axis=-1)
    return jnp.concatenate([top, bot], axis=-2)


def _rw_blocks_to_state(sb):
    bsz = sb.shape[0]
    return jnp.stack([sb[:, :, :DH, :DH], sb[:, :, DH:, DH:]], axis=2).reshape(bsz, H_A, DH, DH)


def _gdn_body(q_ref, k_ref, v_ref, z_ref, ba_ref, qi_ref, ki_ref, vi_ref, cwq_ref, cwk_ref, cwv_ref,
              par_ref, nw_ref, s0_ref, tri_ref, o_ref, so_ref, qc_ref, kc_ref, vc_ref,
              s_ref, carry_ref, *, nb, tb):
    @pl.when(pl.program_id(1) == 0)
    def _():
        carry_ref[:, 0, 5:8, :] = qi_ref[...]
        carry_ref[:, 1, 5:8, :] = ki_ref[...]
        carry_ref[:, 2, 5:8, :] = vi_ref[...]
        s_ref[...] = s0_ref[...]

    def conv(x, w_ref, n, idx, out_ref):
        prev = carry_ref[n, idx]
        y = x * w_ref[3:4, :]
        for kk in range(1, 4):
            y = y + _shift_rows(x, kk, prev) * w_ref[3 - kk:4 - kk, :]
        tail = x[tb - 3:tb, :]
        carry_ref[n, idx, 5:8, :] = tail
        out_ref[n] = tail
        return y * _sigmoid(y)

    lane = lax.broadcasted_iota(jnp.int32, (1, LANES), 1)

    def column(x, idx):
        return jnp.sum(jnp.where(lane == idx, x, 0.0), axis=-1, keepdims=True)

    c = CHUNK
    r2i = lax.broadcasted_iota(jnp.int32, (2 * c, 2 * c), 0)
    c2i = lax.broadcasted_iota(jnp.int32, (2 * c, 2 * c), 1)
    same = (r2i // c) == (c2i // c)
    strict = same & (r2i > c2i)
    lower = same & (r2i >= c2i)
    tri = tri_ref[...]

    stacked = []
    for n in range(nb):
        q = conv(q_ref[n], cwq_ref, n, 0, qc_ref)
        k = conv(k_ref[n], cwk_ref, n, 1, kc_ref)
        v = conv(v_ref[n], cwv_ref, n, 2, vc_ref)
        ba = ba_ref[n]
        beta_all = _sigmoid(ba)
        z_in = ba + par_ref[1:2, :]
        g_all = -jnp.exp(par_ref[0:1, :]) * (jnp.maximum(z_in, 0.0) + jnp.log1p(jnp.exp(-jnp.abs(z_in))))
        for hp in range(H_C // 2):
            heads = []
            for hh in range(2):
                h = 2 * hp + hh
                sl = slice(h * DK, (h + 1) * DK)
                qh = q[:, sl]
                kh = k[:, sl]
                qn = qh * lax.rsqrt(jnp.sum(qh * qh, axis=-1, keepdims=True) + L2_EPS) * (DK ** -0.5)
                kn = kh * lax.rsqrt(jnp.sum(kh * kh, axis=-1, keepdims=True) + L2_EPS)
                beta = jnp.broadcast_to(column(beta_all, h), (tb, DK))
                g = jnp.broadcast_to(column(g_all, H_C + h), (tb, DK))
                heads.append((qn, kn, v[:, sl], beta, g))
            stacked.append([jnp.concatenate([_pad_chunk(heads[0][i], c), _pad_chunk(heads[1][i], c)], axis=0)
                            for i in range(5)])
    chains = [(n, hp) for n in range(nb) for hp in range(H_C // 2)]
    qn2s, kn2s, v2s, b2s, g2s = [[s[i] for s in stacked] for i in range(5)]
    gcs = [_sel_l(tri, g2) for g2 in g2s]
    grows = [jnp.transpose(gc) for gc in gcs]
    gammas = [jnp.where(lower, jnp.exp(jnp.where(lower, gc - grow, 0.0)), 0.0) for gc, grow in zip(gcs, grows)]
    kbs = [kn2 * b2 for kn2, b2 in zip(kn2s, b2s)]
    kkqks = [_mm_nt(jnp.concatenate([kb, qn2], axis=0).astype(BF16), kn2.astype(BF16))
             for kb, qn2, kn2 in zip(kbs, qn2s, kn2s)]
    tinvs = _inv_unit_lower([jnp.where(strict, -kkqk[0:2 * c, :] * gamma, 0.0) for kkqk, gamma in zip(kkqks, gammas)])
    egs = [jnp.exp(gc) for gc in gcs]
    uws = [_mm(tinv.astype(BF16), jnp.concatenate([kb * eg, v2 * b2], axis=1).astype(BF16))
           for tinv, kb, eg, v2, b2 in zip(tinvs, kbs, egs, v2s, b2s)]
    heads2 = [(ci, hh) for ci in range(len(chains)) for hh in range(2)]
    rows = lambda hh: slice(hh * c, (hh + 1) * c)
    glasts = [gcs[ci][hh * c + c - 1:hh * c + c, :] for ci, hh in heads2]
    pzs = [_mm_tn((kn2s[ci][rows(hh), :] * jnp.exp(gl - gcs[ci][rows(hh), :])).astype(BF16),
                  uws[ci][rows(hh), :].astype(BF16))
           for (ci, hh), gl in zip(heads2, glasts)]
    sts = [s_ref[chains[ci][0], 2 * chains[ci][1] + hh] for ci, hh in heads2]
    sbs = [st.astype(BF16) for st in sts]
    for (ci, hh), st, sb, gl, pz in zip(heads2, sts, sbs, glasts, pzs):
        s_ref[chains[ci][0], 2 * chains[ci][1] + hh] = (st * jnp.exp(gl[:, 0:1])
                                                        - _mm(pz[:, 0:DK].astype(BF16), sb) + pz[:, DK:2 * DK])
    wqs = [_mm(jnp.concatenate([uws[ci][rows(hh), 0:DK], (qn2s[ci] * egs[ci])[rows(hh), :]], axis=0).astype(BF16), sb)
           for (ci, hh), sb in zip(heads2, sbs)]
    o2s = []
    for ci in range(len(chains)):
        vnew = jnp.concatenate([uws[ci][rows(hh), DK:2 * DK] - wqs[2 * ci + hh][0:c, :] for hh in range(2)], axis=0)
        qs = jnp.concatenate([wqs[2 * ci + hh][c:2 * c, :] for hh in range(2)], axis=0)
        amat = jnp.where(lower, kkqks[ci][2 * c:4 * c, :] * gammas[ci], 0.0)
        o2s.append(qs + _mm(amat.astype(BF16), vnew.astype(BF16)))
    for ci, (n, hp) in enumerate(chains):
        for hh in range(2):
            sl = slice((2 * hp + hh) * DK, (2 * hp + hh + 1) * DK)
            oh = o2s[ci][hh * c:hh * c + tb, :]
            zz = z_ref[n, :, sl]
            on = oh * lax.rsqrt(jnp.mean(oh * oh, axis=-1, keepdims=True) + EPS) * nw_ref[...]
            o_ref[n, :, sl] = on * (zz * _sigmoid(zz))
    so_ref[...] = s_ref[...]


def _gdn(p_qkv, p_z, p_ba, conv_prev, s0, w, o_idx, batch, seq):
    nb = 2
    assert batch % nb == 0 and seq >= 3
    tb = min(CHUNK, seq)
    nt = seq // tb
    gk = H_C * DK
    cw = w['gdn_conv_w'][o_idx]
    par = jnp.zeros((8, LANES), F32)
    par = par.at[0, H_C:2 * H_C].set(w['gdn_A_log'][o_idx]).at[1, H_C:2 * H_C].set(w['gdn_dt_bias'][o_idx])
    idx = jnp.arange(2 * CHUNK)
    tri = (((idx[:, None] // CHUNK) == (idx[None, :] // CHUNK)) & (idx[:, None] >= idx[None, :])).astype(BF16)
    p_qkv = p_qkv.reshape(batch, seq, 3 * gk)
    col = lambda part: pl.BlockSpec((nb, tb, gk), lambda b, t, part=part: (b, t, part))
    prev = lambda part: pl.BlockSpec((nb, 3, gk), lambda b, t, part=part: (b, 0, part))
    wspec = lambda part: pl.BlockSpec((4, gk), lambda b, t, part=part: (0, part))
    cst = lambda shape: pl.BlockSpec(shape, lambda b, t: (0,) * len(shape))
    stspec = pl.BlockSpec((nb, H_C, DK, DK), lambda b, t: (b, 0, 0, 0))
    outs = pl.pallas_call(
        functools.partial(_gdn_body, nb=nb, tb=tb),
        grid=(batch // nb, nt),
        in_specs=[col(0), col(1), col(2), col(0),
                  pl.BlockSpec((nb, tb, LANES), lambda b, t: (b, t, 0)),
                  prev(0), prev(1), prev(2), wspec(0), wspec(1), wspec(2),
                  cst((8, LANES)), cst((1, DK)), stspec, cst(tri.shape)],
        out_specs=[col(0), stspec, prev(0), prev(0), prev(0)],
        out_shape=[jax.ShapeDtypeStruct((batch, seq, gk), F32), jax.ShapeDtypeStruct(s0.shape, F32)]
                  + [jax.ShapeDtypeStruct((batch, 3, gk), F32)] * 3,
        scratch_shapes=[pltpu.VMEM((nb, H_C, DK, DK), F32), pltpu.VMEM((nb, 3, 8, gk), F32)],
        compiler_params=_cparams("arbitrary", "arbitrary"), name="gdn",
    )(p_qkv, p_qkv, p_qkv, p_z.reshape(batch, seq, gk), p_ba.reshape(batch, seq, LANES),
      conv_prev, conv_prev, conv_prev, cw, cw, cw, par, w['gdn_norm_w'][o_idx].reshape(1, DK), s0, tri)
    o_c, s_out, qc, kc, vc = outs
    return o_c.reshape(batch * seq, gk), jnp.concatenate([qc, kc, vc], axis=-1), s_out


def _run_trunk(x, fox_past, rw_state, rw_shift, gdn_state, gdn_conv, ffn_conv, page_table, w, wb):
    batch, seq, d = x.shape
    n = batch * seq
    depth = w['norm_mix'].shape[0]
    tm = 512 if n % 512 == 0 else n
    x = x.reshape(n, d)
    fk, fv, flf, rws, rwsh, gs, gcv, fcv = [], [], [], [], [], [], [], []
    v_first = None
    for layer in range(depth):
        if layer % 2 == 0:
            e = layer // 2
            p_fox, p_rw, p_fl = _norm_matmul(x, w['norm_mix'][layer], wb['ev_in'][e], (4 * FOX_W, 1792, LANES), tm)
            if fox_past is None:
                q_aug, k_aug, v_bf, k_out, lf = _fox_prep(p_fox, p_fl, w['fox_q_gain'][e], w['fox_k_gain'][e],
                                                          w['fox_b_f'][e], batch, seq)
                o_attn = _fox_attn(q_aug, k_aug, v_bf, batch, seq)
            else:
                o_attn, k_out, lf = _fox_decode(p_fox, p_fl, w['fox_q_gain'][e], w['fox_k_gain'][e], w['fox_b_f'][e],
                                                fox_past[0], fox_past[1], fox_past[2], page_table, e, batch, seq)
            parts, sh = _rw_prep(p_rw, rw_shift[e], w, e, v_first, batch, seq)
            if e == 0:
                v_first = parts[3]
            o_b, s_blk = _rw_scan(parts, w, e, _rw_state_to_blocks(rw_state[e]), batch, seq)
            x = _ev_out(o_attn, p_fox, o_b, wb['ev_out'][e], x, tm)
            fk.append(k_out.reshape(batch, seq, H_A, DH))
            fv.append(p_fox[:, 2 * FOX_W:3 * FOX_W].reshape(batch, seq, H_A, DH))
            flf.append(lf[:, :H_A].reshape(batch, seq, H_A))
            rws.append(_rw_blocks_to_state(s_blk))
            rwsh.append(sh)
        else:
            o_idx = layer // 2
            p_qkv, p_z, p_ba = _norm_matmul(x, w['norm_mix'][layer], wb['od_in'][o_idx],
                                            (3 * H_C * DK, H_C * DK, LANES), tm)
            o_c, cv, s_out = _gdn(p_qkv, p_z, p_ba, gdn_conv[o_idx], gdn_state[o_idx], w, o_idx, batch, seq)
            x = _od_out(o_c, wb['od_out'][o_idx], x, tm)
            gs.append(s_out)
            gcv.append(cv)
        x, buf = _ffn(x, w['norm_ffn'][layer], wb['ffn_up'][layer], w['ffn_conv_w'][layer],
                      w['ffn_conv_b'][layer], wb['ffn_down'][layer], ffn_conv[layer], batch, seq)
        fcv.append(buf)
    y = _final_norm(x, w['norm_out'], tm).reshape(batch, seq, d)
    return (y, jnp.stack(fk), jnp.stack(fv), jnp.stack(flf), jnp.stack(rws), jnp.stack(rwsh),
            jnp.stack(gs), jnp.stack(gcv), jnp.stack(fcv))


def _prep_weights(w):
    fox_cols = 4 * FOX_W + H_A
    ev = w['ev_w_in']
    pad = jnp.zeros(ev.shape[:2] + (LANES - H_A,), ev.dtype)
    ev_in = jnp.concatenate([ev[..., :4 * FOX_W], ev[..., fox_cols:], ev[..., 4 * FOX_W:fox_cols], pad], axis=-1)
    od = w['od_w_in']
    pad2 = jnp.zeros(od.shape[:2] + (LANES - 2 * H_C,), od.dtype)
    od_in = jnp.concatenate([od, pad2], axis=-1)
    return dict(ev_in=ev_in.astype(BF16), ev_out=w['ev_w_out'].astype(BF16), od_in=od_in.astype(BF16),
                od_out=w['od_w_out'].astype(BF16), ffn_up=w['ffn_w_up'].astype(BF16),
                ffn_down=w['ffn_w_down'].astype(BF16))


def kernel(x_prompt, x_sample, cache_fox_k, cache_fox_v, cache_fox_logf, state_rwkv, state_rwkv_shift, state_gdn, state_gdn_conv, state_ffn_conv, page_table, norm_mix, norm_ffn, norm_out, ev_w_in, ev_w_out, fox_b_f, fox_q_gain, fox_k_gain, rw_mu, rw_w0, rw_w2, rw_a0, rw_a2, rw_g2, rw_k_k, rw_k_a, rw_r_k, rw_ln_w, rw_ln_b, rw_v0, rw_v1, rw_v2, od_w_in, od_w_out, gdn_conv_w, gdn_A_log, gdn_dt_bias, gdn_norm_w, ffn_w_up, ffn_conv_w, ffn_conv_b, ffn_w_down):
    w = dict(norm_mix=norm_mix, norm_ffn=norm_ffn, norm_out=norm_out, ev_w_in=ev_w_in, ev_w_out=ev_w_out,
             fox_b_f=fox_b_f, fox_q_gain=fox_q_gain, fox_k_gain=fox_k_gain, rw_mu=rw_mu, rw_w0=rw_w0,
             rw_w2=rw_w2, rw_a0=rw_a0, rw_a2=rw_a2, rw_g2=rw_g2, rw_k_k=rw_k_k, rw_k_a=rw_k_a, rw_r_k=rw_r_k,
             rw_ln_w=rw_ln_w, rw_ln_b=rw_ln_b, rw_v0=rw_v0, rw_v1=rw_v1, rw_v2=rw_v2, od_w_in=od_w_in,
             od_w_out=od_w_out, gdn_conv_w=gdn_conv_w, gdn_A_log=gdn_A_log, gdn_dt_bias=gdn_dt_bias,
             gdn_norm_w=gdn_norm_w, ffn_w_up=ffn_w_up, ffn_conv_w=ffn_conv_w, ffn_conv_b=ffn_conv_b,
             ffn_w_down=ffn_w_down)
    wb = _prep_weights(w)
    bp = x_prompt.shape[0]
    n_even, n_odd, depth = ev_w_in.shape[0], od_w_in.shape[0], norm_mix.shape[0]
    rw_cols = rw_mu.shape[1]
    (y_p, fk_p, fv_p, flf_p, rw_p, rwsh_p, gdn_p, gcv_p, fcv_p) = _run_trunk(
        x_prompt, None,
        jnp.zeros((n_even, bp, H_A, DH, DH), F32), jnp.zeros((n_even, bp, rw_cols), F32),
        jnp.zeros((n_odd, bp, H_C, DK, DK), F32), jnp.zeros((n_odd, bp, 3, 3 * H_C * DK), F32),
        jnp.zeros((depth, bp, 2, ffn_w_up.shape[2]), F32), page_table, w, wb)
    (y_s, fk_s, fv_s, flf_s, rw_s, rwsh_s, gdn_s, gcv_s, fcv_s) = _run_trunk(
        x_sample, (cache_fox_k, cache_fox_v, cache_fox_logf), state_rwkv, state_rwkv_shift,
        state_gdn, state_gdn_conv, state_ffn_conv, page_table, w, wb)
    return (y_p, y_s, fk_p, fv_p, flf_p, fk_s, fv_s, flf_s, rw_p, rw_s, rwsh_p, rwsh_s,
            gdn_p, gdn_s, gcv_p, gcv_s, fcv_p, fcv_s)
```

```python
import functools
import math

import jax
import jax.numpy as jnp
import numpy as np
from jax import lax
from jax.experimental import pallas as pl
from jax.experimental.pallas import tpu as pltpu

F32 = jnp.float32
BF16 = jnp.bfloat16

EPS = 1e-6
RW_GN_EPS = 64e-5
L2_EPS = 1e-6
NEG_BIG = -1e30

H_A = 8
DH = 64
FOX_W = 512
RW_W = 512
H_C = 8
DK = 128
CHUNK = 64
LANES = 128
VMEM_LIMIT = 56 * 1024 * 1024


def _cparams(*sem):
    return pltpu.CompilerParams(dimension_semantics=sem, vmem_limit_bytes=VMEM_LIMIT)


def _const_spec(shape):
    nd = len(shape)
    return pl.BlockSpec(shape, lambda *_: (0,) * nd, pipeline_mode=pl.Buffered(1))


def _mm(a, b):
    return jnp.dot(a, b, preferred_element_type=F32)


def _mm_nt(a, b):
    return lax.dot_general(a, b, (((1,), (1,)), ((), ())), preferred_element_type=F32)


def _mm_tn(a, b):
    return lax.dot_general(a, b, (((0,), (0,)), ((), ())), preferred_element_type=F32)


def _split3(x):
    hi = x.astype(BF16)
    r = x - hi.astype(F32)
    mid = r.astype(BF16)
    lo = (r - mid.astype(F32)).astype(BF16)
    return hi, mid, lo


def _sel_l(m01, x):
    hi, mid, lo = _split3(x)
    return _mm(m01, hi) + _mm(m01, mid) + _mm(m01, lo)


def _sel_r(x, m01):
    hi, mid, lo = _split3(x)
    return _mm(hi, m01) + _mm(mid, m01) + _mm(lo, m01)


def _sel_nt(m01, x):
    hi, mid, lo = _split3(x)
    return _mm_nt(m01, hi) + _mm_nt(m01, mid) + _mm_nt(m01, lo)


def _log_sigmoid(z):
    return jnp.minimum(z, 0.0) - jnp.log1p(jnp.exp(-jnp.abs(z)))


def _sigmoid(z):
    return 1.0 / (1.0 + jnp.exp(-z))


def _inv_unit_lower(ns):
    size = ns[0].shape[0]
    r = lax.broadcasted_iota(jnp.int32, (size, size), 0)
    c = lax.broadcasted_iota(jnp.int32, (size, size), 1)
    eye = jnp.where(r == c, 1.0, 0.0)
    ps = [eye + n for n in ns]
    nks = list(ns)
    for _ in range(int(math.log2(CHUNK)) - 1):
        nks = [_mm(nk.astype(BF16), nk.astype(BF16)) for nk in nks]
        ps = [p + _mm(p.astype(BF16), nk.astype(BF16)) for p, nk in zip(ps, nks)]
    return ps


def _head_scale(x2, fn):
    lane_h = lax.broadcasted_iota(jnp.int32, (1, x2.shape[1]), 1) // DH
    out = jnp.zeros_like(x2)
    for h in range(x2.shape[1] // DH):
        s = jnp.sum(x2[:, h * DH:(h + 1) * DH], axis=-1, keepdims=True)
        out = jnp.where(lane_h == h, fn(s), out)
    return out


def _rms_heads(x, gain):
    return x * _head_scale(x * x, lambda s: lax.rsqrt(s * (1.0 / DH) + EPS)) * gain


def _shift_rows(x, k, prev):
    rolled = pltpu.roll(x, k, 0)
    sub = 8
    row = lax.broadcasted_iota(jnp.int32, (sub, x.shape[1]), 0)
    head = rolled[0:sub, :]
    nprev = prev.shape[0]
    for j in range(k):
        head = jnp.where(row == j, prev[nprev - k + j:nprev - k + j + 1, :], head)
    return head if x.shape[0] == sub else jnp.concatenate([head, rolled[sub:, :]], axis=0)


def _nm_body(x_ref, g_ref, w_ref, *o_refs, splits):
    x = x_ref[...]
    xn = (x * lax.rsqrt(jnp.mean(x * x, axis=-1, keepdims=True) + EPS) * g_ref[...]).astype(BF16)
    off = 0
    for o_ref, n in zip(o_refs, splits):
        o_ref[...] = _mm(xn, w_ref[:, off:off + n])
        off += n


def _norm_matmul(x, g, w_bf, splits, tm):
    n, d = x.shape
    ntot = w_bf.shape[1]
    assert sum(splits) == ntot and n % tm == 0
    return pl.pallas_call(
        functools.partial(_nm_body, splits=tuple(splits)),
        grid=(n // tm,),
        in_specs=[pl.BlockSpec((tm, d), lambda i: (i, 0)), _const_spec((1, d)), _const_spec((d, ntot))],
        out_specs=[pl.BlockSpec((tm, s), lambda i: (i, 0)) for s in splits],
        out_shape=[jax.ShapeDtypeStruct((n, s), F32) for s in splits],
        compiler_params=_cparams("arbitrary"),
        name="norm_matmul",
    )(x, g.reshape(1, d), w_bf)


def _final_norm_body(x_ref, g_ref, o_ref):
    x = x_ref[...]
    o_ref[...] = x * lax.rsqrt(jnp.mean(x * x, axis=-1, keepdims=True) + EPS) * g_ref[...]


def _final_norm(x, g, tm):
    n, d = x.shape
    return pl.pallas_call(
        _final_norm_body, grid=(n // tm,),
        in_specs=[pl.BlockSpec((tm, d), lambda i: (i, 0)), _const_spec((1, d))],
        out_specs=pl.BlockSpec((tm, d), lambda i: (i, 0)),
        out_shape=jax.ShapeDtypeStruct((n, d), F32),
        compiler_params=_cparams("arbitrary"), name="final_norm",
    )(x, g.reshape(1, d))


def _evout_body(oa_ref, og_ref, ob_ref, w_ref, x_ref, o_ref):
    a = (oa_ref[...] * _sigmoid(og_ref[...])).astype(BF16)
    b = ob_ref[...].astype(BF16)
    o_ref[...] = x_ref[...] + _mm(a, w_ref[0:FOX_W, :]) + _mm(b, w_ref[FOX_W:FOX_W + RW_W, :])


def _ev_out(o_attn, p_fox, o_b, w_bf, x, tm):
    n, d = x.shape
    return pl.pallas_call(
        _evout_body, grid=(n // tm,),
        in_specs=[pl.BlockSpec((tm, FOX_W), lambda i: (i, 0)),
                  pl.BlockSpec((tm, FOX_W), lambda i: (i, 3)),
                  pl.BlockSpec((tm, RW_W), lambda i: (i, 0)),
                  _const_spec((FOX_W + RW_W, d)),
                  pl.BlockSpec((tm, d), lambda i: (i, 0))],
        out_specs=pl.BlockSpec((tm, d), lambda i: (i, 0)),
        out_shape=jax.ShapeDtypeStruct((n, d), F32),
        compiler_params=_cparams("arbitrary"), name="ev_out",
    )(o_attn, p_fox, o_b, w_bf, x)


def _odout_body(oc_ref, w_ref, x_ref, o_ref):
    o_ref[...] = x_ref[...] + _mm(oc_ref[...].astype(BF16), w_ref[...])


def _od_out(o_c, w_bf, x, tm):
    n, d = x.shape
    k = o_c.shape[1]
    return pl.pallas_call(
        _odout_body, grid=(n // tm,),
        in_specs=[pl.BlockSpec((tm, k), lambda i: (i, 0)), _const_spec((k, d)),
                  pl.BlockSpec((tm, d), lambda i: (i, 0))],
        out_specs=pl.BlockSpec((tm, d), lambda i: (i, 0)),
        out_shape=jax.ShapeDtypeStruct((n, d), F32),
        compiler_params=_cparams("arbitrary"), name="od_out",
    )(o_c, w_bf, x)


def _ffn_cols(xn, wup_ref, cw_ref, cb_ref, wdn_ref, acc, prev_fn, tail_fn, f, cwb):
    for c in range(f // cwb):
        ys = []
        for half in (0, 1):
            lo = half * f + c * cwb
            h = _mm(xn, wup_ref[:, lo:lo + cwb])
            hm1, hm2 = prev_fn(h, lo)
            ys.append(hm2 * cw_ref[0:1, lo:lo + cwb] + hm1 * cw_ref[1:2, lo:lo + cwb]
                      + h * cw_ref[2:3, lo:lo + cwb] + cb_ref[:, lo:lo + cwb])
            tail_fn(h, lo)
        u, gt = ys
        act = (gt * _sigmoid(gt) * u).astype(BF16)
        acc = acc + _mm(act, wdn_ref[c * cwb:(c + 1) * cwb, :])
    return acc


def _ffn_norm(x_ref, g_ref):
    x = x_ref[...]
    return x, (x * lax.rsqrt(jnp.mean(x * x, axis=-1, keepdims=True) + EPS) * g_ref[...]).astype(BF16)


def _ffn_seq_body(x_ref, g_ref, wup_ref, cw_ref, cb_ref, wdn_ref, init_ref, o_ref, st_ref, carry_ref,
                  *, tb, f, cwb):
    @pl.when(pl.program_id(1) == 0)
    def _():
        carry_ref[0:2, :] = init_ref[0]

    x, xn = _ffn_norm(x_ref, g_ref)

    def prev_fn(h, lo):
        prev = carry_ref[0:2, lo:lo + cwb]
        return _shift_rows(h, 1, prev), _shift_rows(h, 2, prev)

    def tail_fn(h, lo):
        carry_ref[0:2, lo:lo + cwb] = h[tb - 2:tb, :]
        st_ref[0, :, lo:lo + cwb] = h[tb - 2:tb, :]

    o_ref[...] = _ffn_cols(xn, wup_ref, cw_ref, cb_ref, wdn_ref, x, prev_fn, tail_fn, f, cwb)


def _ffn_flat_body(x_ref, g_ref, wup_ref, cw_ref, cb_ref, wdn_ref, f1_ref, f2_ref, o_ref, st_ref,
                   *, rows, seq, f, cwb):
    x, xn = _ffn_norm(x_ref, g_ref)
    tmod = lax.broadcasted_iota(jnp.int32, (rows, cwb), 0) % seq

    def prev_fn(h, lo):
        hm1 = jnp.where(tmod == 0, f1_ref[:, lo:lo + cwb], pltpu.roll(h, 1, 0))
        hm2 = jnp.where(tmod < 2, f2_ref[:, lo:lo + cwb], pltpu.roll(h, 2, 0))
        return hm1, hm2

    def tail_fn(h, lo):
        st_ref[:, :, lo:lo + cwb] = h.reshape(rows // seq, seq, cwb)[:, seq - 2:seq, :]

    o_ref[...] = _ffn_cols(xn, wup_ref, cw_ref, cb_ref, wdn_ref, x, prev_fn, tail_fn, f, cwb)


def _ffn(x, g, wup_bf, conv_w, conv_b, wdn_bf, conv_prev, batch, seq):
    n, d = x.shape
    f2 = wup_bf.shape[1]
    f = f2 // 2
    cwb = f
    weights = [_const_spec((1, d)), _const_spec((d, f2)), _const_spec((3, f2)), _const_spec((1, f2)),
               _const_spec((f, d))]
    out_shape = [jax.ShapeDtypeStruct((n, d), F32), jax.ShapeDtypeStruct((batch, 2, f2), F32)]
    args = (x, g.reshape(1, d), wup_bf, conv_w, conv_b.reshape(1, f2), wdn_bf)
    if seq >= 256:
        tb = 512 if seq % 512 == 0 else 256
        nt = seq // tb
        return pl.pallas_call(
            functools.partial(_ffn_seq_body, tb=tb, f=f, cwb=cwb),
            grid=(batch, nt),
            in_specs=[pl.BlockSpec((tb, d), lambda b, t: (b * nt + t, 0))] + weights
                     + [pl.BlockSpec((1, 2, f2), lambda b, t: (b, 0, 0))],
            out_specs=[pl.BlockSpec((tb, d), lambda b, t: (b * nt + t, 0)),
                       pl.BlockSpec((1, 2, f2), lambda b, t: (b, 0, 0))],
            out_shape=out_shape,
            scratch_shapes=[pltpu.VMEM((8, f2), F32)],
            compiler_params=_cparams("arbitrary", "arbitrary"), name="ffn_seq",
        )(*args, conv_prev)
    zeros = jnp.zeros((batch, seq - 2, f2), F32)
    fill2 = jnp.concatenate([conv_prev, zeros], axis=1).reshape(n, f2)
    fill1 = jnp.concatenate([conv_prev[:, 1:2], zeros, zeros[:, :1]], axis=1).reshape(n, f2)
    return pl.pallas_call(
        functools.partial(_ffn_flat_body, rows=n, seq=seq, f=f, cwb=cwb),
        grid=(1,),
        in_specs=[pl.BlockSpec((n, d), lambda i: (0, 0))] + weights
                 + [pl.BlockSpec((n, f2), lambda i: (0, 0)), pl.BlockSpec((n, f2), lambda i: (0, 0))],
        out_specs=[pl.BlockSpec((n, d), lambda i: (0, 0)), pl.BlockSpec((batch, 2, f2), lambda i: (0, 0, 0))],
        out_shape=out_shape,
        compiler_params=_cparams("arbitrary"), name="ffn_flat",
    )(*args, fill1, fill2)


def _fox_consts():
    src = np.arange(FOX_W)
    place = np.zeros((FOX_W, H_A * LANES), np.float32)
    place[src, (src // DH) * LANES + src % DH] = 1.0
    hh = np.arange(H_A)
    eq = np.zeros((3, LANES, H_A * LANES), np.float32)
    ek = np.zeros((3, LANES, H_A * LANES), np.float32)
    cq = np.zeros((1, H_A * LANES), np.float32)
    ck = np.zeros((1, H_A * LANES), np.float32)
    for piece in range(3):
        eq[piece, hh, hh * LANES + DH + piece] = 1.0
        ek[piece, hh, hh * LANES + DH + 3 + piece] = -1.0
        cq[0, hh * LANES + DH + 3 + piece] = 1.0
        ck[0, hh * LANES + DH + piece] = 1.0
    as_bf = lambda a: jnp.asarray(a, BF16)
    return as_bf(place), as_bf(place.T), as_bf(eq), as_bf(ek), jnp.asarray(cq), jnp.asarray(ck)


def _fox_prep_body(q_ref, k_ref, v_ref, fl_ref, qg_ref, kg_ref, bf_ref, pm_ref, pt_ref, eq_ref, ek_ref, cq_ref,
                   ck_ref, vone_ref, tri_ref, qa_ref, ka_ref, vt_ref, ko_ref, lf_ref, carry_ref, *, tm):
    @pl.when(pl.program_id(1) == 0)
    def _():
        carry_ref[...] = jnp.zeros_like(carry_ref)

    qn = _rms_heads(q_ref[...], qg_ref[...]) * (DH ** -0.5)
    kn = _rms_heads(k_ref[...], kg_ref[...])
    lf = _log_sigmoid(fl_ref[...] + bf_ref[...])
    lf_ref[...] = lf
    fcum = _sel_l(tri_ref[...], lf) + carry_ref[0:1, :]
    carry_ref[0:1, :] = fcum[tm - 1:tm, :]
    fh, fm, flo = _split3(fcum)
    pm = pm_ref[...]
    qa = (_mm(qn.astype(BF16), pm) + _mm(fh, eq_ref[0]) + _mm(fm, eq_ref[1]) + _mm(flo, eq_ref[2])
          + cq_ref[...])
    ka = (_mm(kn.astype(BF16), pm) + _mm(fh, ek_ref[0]) + _mm(fm, ek_ref[1]) + _mm(flo, ek_ref[2])
          + ck_ref[...])
    qa_ref[...] = qa.astype(BF16)
    ka_ref[...] = ka.astype(BF16)
    vt = _mm_nt(pt_ref[...], v_ref[...].astype(BF16)) + vone_ref[...]
    vt_ref[0, :, 0] = vt.astype(BF16).reshape(H_A, LANES, tm)
    ko_ref[...] = kn


FOX_TILE = 512


def _fox_tile(seq):
    return min(FOX_TILE, seq // 2)


def _pad_lanes(v, offset=0):
    return jnp.concatenate([jnp.zeros((offset,), F32), v.astype(F32),
                            jnp.zeros((LANES - offset - v.shape[0],), F32)]).reshape(1, LANES)


def _fox_prep(p_fox, p_fl, q_gain, k_gain, b_f, batch, seq):
    n = p_fox.shape[0]
    tm = _fox_tile(seq)
    nt = seq // tm
    place, place_t, eq, ek, cq, ck = _fox_consts()
    vone = np.zeros((H_A * LANES, 1), np.float32)
    vone[np.arange(H_A) * LANES + DH, 0] = 1.0
    vone = jnp.asarray(vone)
    tri = jnp.asarray(np.tril(np.ones((tm, tm), np.float32)), BF16)
    bf = _pad_lanes(b_f)
    row = lambda c: pl.BlockSpec((tm, FOX_W), lambda b, t, c=c: (b * nt + t, c))
    wide = pl.BlockSpec((tm, H_A * LANES), lambda b, t: (b * nt + t, 0))
    narrow = pl.BlockSpec((tm, LANES), lambda b, t: (b * nt + t, 0))
    return pl.pallas_call(
        functools.partial(_fox_prep_body, tm=tm),
        grid=(batch, nt),
        in_specs=[row(0), row(1), row(2), narrow,
                  _const_spec((1, FOX_W)), _const_spec((1, FOX_W)), _const_spec((1, LANES)),
                  _const_spec(place.shape), _const_spec(place_t.shape), _const_spec(eq.shape), _const_spec(ek.shape),
                  _const_spec(cq.shape), _const_spec(ck.shape), _const_spec(vone.shape), _const_spec(tri.shape)],
        out_specs=[wide, wide, pl.BlockSpec((1, H_A, 1, LANES, tm), lambda b, t: (b, 0, t, 0, 0)), row(0), narrow],
        out_shape=[jax.ShapeDtypeStruct((n, H_A * LANES), BF16), jax.ShapeDtypeStruct((n, H_A * LANES), BF16),
                   jax.ShapeDtypeStruct((batch, H_A, nt, LANES, tm), BF16), jax.ShapeDtypeStruct((n, FOX_W), F32),
                   jax.ShapeDtypeStruct((n, LANES), F32)],
        scratch_shapes=[pltpu.VMEM((8, LANES), F32)],
        compiler_params=_cparams("arbitrary", "arbitrary"), name="fox_prep",
    )(p_fox, p_fox, p_fox, p_fl, jnp.tile(q_gain, H_A).reshape(1, FOX_W),
      jnp.tile(k_gain, H_A).reshape(1, FOX_W), bf, place, place_t, eq, ek, cq, ck, vone, tri)


def _fox_attn_body(qa_ref, ka_ref, vt_ref, o_ref, *, tq):
    i = pl.program_id(2)
    key = lax.broadcasted_iota(jnp.int32, (tq, tq), 0)
    qry = lax.broadcasted_iota(jnp.int32, (tq, tq), 1)
    causal = key <= qry
    chains = [(sb, hh) for sb in range(2) for hh in range(2)]
    qs = [qa_ref[sb * tq:(sb + 1) * tq, hh * LANES:(hh + 1) * LANES] for sb, hh in chains]

    def block(j, carry, active):
        off = pl.multiple_of(j * tq, tq)
        ks = [ka_ref[pl.ds(off, tq), hh * LANES:(hh + 1) * LANES] for hh in range(2)]
        vts = [vt_ref[0, hh, j] for hh in range(2)]
        sts = {c: _mm_nt(ks[chains[c][1]], qs[c]) for c in active}
        sts = {c: jnp.where(causal, st, NEG_BIG) if active[c] else st for c, st in sts.items()}
        m2s = {c: jnp.maximum(carry[c][0], jnp.max(st, axis=0, keepdims=True)) for c, st in sts.items()}
        ps = {c: jnp.exp(st - m2s[c]).astype(BF16) for c, st in sts.items()}
        return tuple((m2s[c], jnp.exp(carry[c][0] - m2s[c]) * carry[c][1] + _mm(vts[chains[c][1]], ps[c]))
                     if c in active else carry[c] for c in range(len(chains)))

    init = tuple((jnp.full((1, tq), NEG_BIG, F32), jnp.zeros((LANES, tq), F32)) for _ in chains)
    carry = lax.fori_loop(0, 2 * i, lambda j, c: block(j, c, {0: False, 1: False, 2: False, 3: False}), init)
    carry = block(2 * i, carry, {0: True, 1: True, 2: False, 3: False})
    carry = block(2 * i + 1, carry, {2: True, 3: True})
    for sb in range(2):
        halves = [acc[0:DH, :] / acc[DH:DH + 1, :] for _, acc in carry[2 * sb:2 * sb + 2]]
        o_ref[sb * tq:(sb + 1) * tq, :] = jnp.transpose(jnp.concatenate(halves, axis=0))


def _fox_attn(q_aug, k_aug, v_t, batch, seq):
    n = q_aug.shape[0]
    tq = _fox_tile(seq)
    nq = seq // (2 * tq)
    return pl.pallas_call(
        functools.partial(_fox_attn_body, tq=tq),
        grid=(batch, H_A // 2, nq),
        in_specs=[pl.BlockSpec((2 * tq, 2 * LANES), lambda b, hp, i: (b * nq + i, hp)),
                  pl.BlockSpec((seq, 2 * LANES), lambda b, hp, i: (b, hp)),
                  pl.BlockSpec((1, 2, seq // tq, LANES, tq), lambda b, hp, i: (b, hp, 0, 0, 0))],
        out_specs=pl.BlockSpec((2 * tq, LANES), lambda b, hp, i: (b * nq + i, hp)),
        out_shape=jax.ShapeDtypeStruct((n, FOX_W), F32),
        compiler_params=_cparams("arbitrary", "arbitrary", "arbitrary"), name="fox_attn",
    )(q_aug, k_aug, v_t)


def _fox_decode_body(pt_ref, q_ref, k_ref, v_ref, fl_ref, qg_ref, kg_ref, bf_ref, ms_ref, ps_ref, pa_ref,
                     asel_ref, *rest, nb, pps, seq):
    del pt_ref
    npg = nb * pps
    kp, vp, lp = rest[:npg], rest[npg:2 * npg], rest[2 * npg:3 * npg]
    o_ref, ko_ref, lfo_ref = rest[3 * npg:3 * npg + 3]
    qh_ref, cb_ref, m_ref, l_ref, acc_ref, car_ref = rest[3 * npg + 3:]
    j = pl.program_id(1)
    nrow = H_A * seq

    @pl.when(j == 0)
    def _new_tokens():
        q_all = _rms_heads(q_ref[...], qg_ref[...]) * (DH ** -0.5)
        k_all = _rms_heads(k_ref[...], kg_ref[...])
        ko_ref[...] = k_all
        lf_all = _log_sigmoid(fl_ref[...] + bf_ref[...])
        lfo_ref[...] = lf_all
        row = lax.broadcasted_iota(jnp.int32, (seq, LANES), 0)
        zpad = jnp.zeros((LANES - seq, DH), F32)
        keyi = lax.broadcasted_iota(jnp.int32, (nrow, LANES), 1)
        ti = lax.broadcasted_iota(jnp.int32, (nrow, LANES), 0) % seq
        for n in range(nb):
            rs = slice(n * seq, (n + 1) * seq)
            q, k, v = q_all[rs, :], k_all[rs, :], v_ref[rs, :]
            c = lf_all[rs, :]
            s = 1
            while s < seq:
                c = c + jnp.where(row >= s, pltpu.roll(c, s, 0), 0.0)
                s *= 2
            srows = []
            for h in range(H_A):
                qh = q[:, h * DH:(h + 1) * DH]
                qh_ref[n, h] = qh
                cb_ref[n, h * seq:(h + 1) * seq, :] = jnp.broadcast_to(c[:, h:h + 1], (seq, LANES))
                kpad = jnp.concatenate([k[:, h * DH:(h + 1) * DH], zpad], axis=0).astype(BF16)
                srows.append(_mm_nt(qh.astype(BF16), kpad))
            cneg = jnp.concatenate([-c, jnp.zeros((LANES - seq, LANES), F32)], axis=0)
            s_new = jnp.concatenate(srows, axis=0) + cb_ref[n] + _sel_nt(asel_ref[0:nrow, :], cneg)
            s_new = jnp.where(keyi <= ti, s_new, NEG_BIG)
            m = jnp.max(s_new, axis=-1, keepdims=True)
            p = jnp.exp(s_new - m)
            m_ref[n] = m
            l_ref[n] = jnp.sum(p, axis=-1, keepdims=True)
            accs = []
            for h in range(H_A):
                vpad = jnp.concatenate([v[:, h * DH:(h + 1) * DH], zpad], axis=0).astype(BF16)
                accs.append(_mm(p[h * seq:(h + 1) * seq, :].astype(BF16), vpad))
            acc_ref[n] = jnp.concatenate(accs, axis=0)
        car_ref[...] = jnp.zeros_like(car_ref)

    seqs = range(nb)
    lfts = [jnp.concatenate([lp[n * pps + i][...] for i in range(pps)]
                            + [jnp.zeros((LANES - pps * H_A, LANES), F32)], axis=0) for n in seqs]
    tots = [jnp.broadcast_to(jnp.sum(lft, axis=1, keepdims=True), (LANES, LANES)) for lft in lfts]
    scores = [[jnp.concatenate([_mm(qh_ref[n, h].astype(BF16), kp[n * pps + i][h].astype(BF16))
                                for h in range(H_A)], axis=0) for i in range(pps)] for n in seqs]
    cars = [car_ref[n] for n in seqs]
    rfulls = [_sel_r(lft, ms_ref[...]) + _sel_l(ps_ref[...], tot) + car for lft, tot, car in zip(lfts, tots, cars)]
    for n in seqs:
        car_ref[n] = cars[n] + _sel_l(pa_ref[...], tots[n])
    biases = [_sel_l(asel_ref[...], rfull) for rfull in rfulls]
    s_alls = [jnp.concatenate([scores[n][i] + biases[n][i * nrow:(i + 1) * nrow, :] + cb_ref[n]
                               for i in range(pps)], axis=1) for n in seqs]
    m_olds = [m_ref[n] for n in seqs]
    m_news = [jnp.maximum(m_old, jnp.max(s_all, axis=-1, keepdims=True)) for m_old, s_all in zip(m_olds, s_alls)]
    probs = [jnp.exp(s_all - m_new) for s_all, m_new in zip(s_alls, m_news)]
    alphas = [jnp.exp(m_old - m_new) for m_old, m_new in zip(m_olds, m_news)]
    pvs = [jnp.concatenate(
        [_mm_nt(probs[n][h * seq:(h + 1) * seq, :].astype(BF16),
                jnp.concatenate([vp[n * pps + i][h] for i in range(pps)], axis=1).astype(BF16))
         for h in range(H_A)], axis=0) for n in seqs]
    for n in seqs:
        m_ref[n] = m_news[n]
        l_ref[n] = alphas[n] * l_ref[n] + jnp.sum(probs[n], axis=-1, keepdims=True)
        acc_ref[n] = alphas[n] * acc_ref[n] + pvs[n]

    @pl.when(j == pl.num_programs(1) - 1)
    def _():
        for n in seqs:
            o = acc_ref[n] / l_ref[n]
            for h in range(H_A):
                o_ref[n * seq:(n + 1) * seq, h * DH:(h + 1) * DH] = o[h * seq:(h + 1) * seq, :]


def _fox_decode(p_fox, p_fl, q_gain, k_gain, b_f, pool_k, pool_v, pool_lf, page_table, e, batch, seq):
    n = p_fox.shape[0]
    n_pages = page_table.shape[1]
    page = pool_lf.shape[2]
    nb = 2
    assert page == LANES and seq == 8 and batch % nb == 0
    pps = 8
    while n_pages % pps:
        pps //= 2
    groups = n_pages // pps
    pk = jnp.transpose(pool_k, (0, 1, 3, 4, 2))
    pv = jnp.transpose(pool_v, (0, 1, 3, 4, 2))
    plf = jnp.transpose(pool_lf, (0, 1, 3, 2))
    nrow = H_A * seq
    idx = np.arange(LANES)
    as_bf = lambda a: jnp.asarray(a.astype(np.float32), BF16)
    ms = as_bf(idx[:, None] > idx[None, :])
    same_h = (idx[:, None] % H_A) == (idx[None, :] % H_A)
    valid = (idx[:, None] < pps * H_A) & (idx[None, :] < pps * H_A)
    ps = as_bf(same_h & valid & (idx[None, :] // H_A < idx[:, None] // H_A))
    pa = as_bf(same_h & valid)
    r = np.arange(pps * nrow)
    asel = np.zeros((pps * nrow, LANES), np.float32)
    asel[r, (r // nrow) * H_A + (r % nrow) // seq] = 1.0
    asel = as_bf(asel)
    bf = _pad_lanes(b_f)

    def page_spec(s, i, shape):
        def index(b, j, pt):
            return (e, pt[b * nb + s, n_pages - 1 - (j * pps + i)]) + (0,) * len(shape)
        return pl.BlockSpec((None, None) + shape, index)

    pages = lambda shape: [page_spec(s, i, shape) for s in range(nb) for i in range(pps)]
    rowspec = lambda c: pl.BlockSpec((nb * seq, FOX_W), lambda b, j, pt, c=c: (b, c))
    narrow = pl.BlockSpec((nb * seq, LANES), lambda b, j, pt: (b, 0))
    cst = lambda shape: pl.BlockSpec(shape, lambda b, j, pt: (0,) * len(shape))
    in_specs = ([rowspec(0), rowspec(1), rowspec(2), narrow,
                 cst((1, FOX_W)), cst((1, FOX_W)), cst((1, LANES)), cst(ms.shape), cst(ps.shape), cst(pa.shape),
                 cst(asel.shape)]
                + pages((H_A, DH, page)) + pages((H_A, DH, page)) + pages((H_A, page)))
    grid_spec = pltpu.PrefetchScalarGridSpec(
        num_scalar_prefetch=1, grid=(batch // nb, groups), in_specs=in_specs,
        out_specs=[rowspec(0), rowspec(0), narrow],
        scratch_shapes=[pltpu.VMEM((nb, H_A, seq, DH), F32), pltpu.VMEM((nb, nrow, LANES), F32),
                        pltpu.VMEM((nb, nrow, 1), F32), pltpu.VMEM((nb, nrow, 1), F32),
                        pltpu.VMEM((nb, nrow, DH), F32), pltpu.VMEM((nb, LANES, LANES), F32)])
    return pl.pallas_call(
        functools.partial(_fox_decode_body, nb=nb, pps=pps, seq=seq),
        grid_spec=grid_spec,
        out_shape=[jax.ShapeDtypeStruct((n, FOX_W), F32), jax.ShapeDtypeStruct((n, FOX_W), F32),
                   jax.ShapeDtypeStruct((n, LANES), F32)],
        compiler_params=_cparams("arbitrary", "arbitrary"), name="fox_decode",
    )(page_table, p_fox, p_fox, p_fox, p_fl, jnp.tile(q_gain, H_A).reshape(1, FOX_W),
      jnp.tile(k_gain, H_A).reshape(1, FOX_W), bf, ms, ps, pa, asel,
      *([pk] * (nb * pps)), *([pv] * (nb * pps)), *([plf] * (nb * pps)))


def _rw_prep_body(p_ref, init_ref, mu_ref, w0_ref, w2_ref, a0_ref, a2_ref, g2_ref, kk_ref, ka_ref, *rest,
                  tb, first):
    if first:
        r_o, lw_o, k_o, v_o, a_o, b_o, g_o, sh_o, carry_ref = rest
    else:
        v0_ref, v1_ref, v2_ref, vf_ref, r_o, lw_o, k_o, v_o, a_o, b_o, g_o, sh_o, carry_ref = rest

    @pl.when(pl.program_id(1) == 0)
    def _():
        carry_ref[7:8, :] = init_ref[0]

    p = p_ref[...]
    p_prev = _shift_rows(p, 1, carry_ref[...])
    carry_ref[7:8, :] = p[tb - 1:tb, :]
    sh_o[0] = p[tb - 1:tb, :]
    ps = p + (p_prev - p) * mu_ref[...]
    r = ps[:, 0:RW_W]
    k = ps[:, RW_W:2 * RW_W]
    v = ps[:, 2 * RW_W:3 * RW_W]
    x128 = ps[:, 3 * RW_W:3 * RW_W + LANES]
    gd = ps[:, 3 * RW_W + LANES:3 * RW_W + 2 * LANES]
    lane = lax.broadcasted_iota(jnp.int32, (1, LANES), 1)
    xw = jnp.where(lane < DH, jnp.tanh(x128), 0.0).astype(BF16)
    xa = jnp.where(lane < DH, 0.0, x128).astype(BF16)
    w_log = _log_sigmoid(w0_ref[...] + _mm(xw, w2_ref[...])) - 0.5
    lw_o[...] = -jnp.exp(w_log)
    a = _sigmoid(a0_ref[...] + _mm(xa, a2_ref[...]))
    g_o[...] = _mm(_sigmoid(gd).astype(BF16), g2_ref[...])
    if not first:
        gate = _sigmoid(v0_ref[...] + _mm(_mm(v.astype(BF16), v1_ref[...]).astype(BF16), v2_ref[...]))
        v = v + (vf_ref[...] - v) * gate
    kkx = k * kk_ref[...]
    kk = kkx * _head_scale(kkx * kkx, lambda s: lax.rsqrt(s + L2_EPS))
    r_o[...] = r
    k_o[...] = k * (1.0 + (a - 1.0) * ka_ref[...])
    v_o[...] = v
    a_o[...] = -kk
    b_o[...] = kk * a


def _rw_prep(p_rw, shift_prev, w, e, v_first, batch, seq):
    n, cols = p_rw.shape
    tb = min(256, seq)
    nt = seq // tb
    first = e == 0
    pad_rows = lambda m: jnp.concatenate([m, jnp.zeros((LANES - m.shape[0], m.shape[1]), m.dtype)], axis=0)
    w2p = pad_rows(w['rw_w2'][e]).astype(BF16)
    a2p = jnp.concatenate([jnp.zeros((DH, RW_W), F32), w['rw_a2'][e]], axis=0).astype(BF16)
    vec = lambda x: x.reshape(1, -1)
    args = [p_rw, shift_prev.reshape(batch, 1, cols), vec(w['rw_mu'][e]), vec(w['rw_w0'][e]), w2p,
            vec(w['rw_a0'][e]), a2p, w['rw_g2'][e].astype(BF16), vec(w['rw_k_k'][e]), vec(w['rw_k_a'][e])]
    rowspec = pl.BlockSpec((tb, RW_W), lambda b, t: (b * nt + t, 0))
    in_specs = [pl.BlockSpec((tb, cols), lambda b, t: (b * nt + t, 0)),
                pl.BlockSpec((1, 1, cols), lambda b, t: (b, 0, 0)),
                _const_spec((1, cols)), _const_spec((1, RW_W)), _const_spec((LANES, RW_W)),
                _const_spec((1, RW_W)), _const_spec((LANES, RW_W)), _const_spec((LANES, RW_W)),
                _const_spec((1, RW_W)), _const_spec((1, RW_W))]
    if not first:
        v1p = jnp.concatenate([w['rw_v1'][e - 1], jnp.zeros((RW_W, LANES - w['rw_v1'].shape[2]), F32)], axis=1)
        args += [vec(w['rw_v0'][e - 1]), v1p.astype(BF16), pad_rows(w['rw_v2'][e - 1]).astype(BF16), v_first]
        in_specs += [_const_spec((1, RW_W)), _const_spec((RW_W, LANES)), _const_spec((LANES, RW_W)), rowspec]
    outs = pl.pallas_call(
        functools.partial(_rw_prep_body, tb=tb, first=first),
        grid=(batch, nt), in_specs=in_specs,
        out_specs=[rowspec] * 7 + [pl.BlockSpec((1, 1, cols), lambda b, t: (b, 0, 0))],
        out_shape=[jax.ShapeDtypeStruct((n, RW_W), F32)] * 7 + [jax.ShapeDtypeStruct((batch, 1, cols), F32)],
        scratch_shapes=[pltpu.VMEM((8, cols), F32)],
        compiler_params=_cparams("arbitrary", "arbitrary"), name="rw_prep",
    )(*args)
    return outs[:7], outs[7].reshape(batch, cols)


def _pad_chunk(x, rows):
    if x.shape[0] == rows:
        return x
    return jnp.concatenate([x, jnp.zeros((rows - x.shape[0], x.shape[1]), x.dtype)], axis=0)


def _rw_scan_body(r_ref, lw_ref, k_ref, v_ref, a_ref, b_ref, g_ref, rk_ref, lnw_ref, lnb_ref, s0_ref, tri_ref,
                  o_ref, so_ref, s_ref, *, nb, tb):
    @pl.when(pl.program_id(1) == 0)
    def _():
        s_ref[...] = s0_ref[...]

    c = CHUNK
    lane = lax.broadcasted_iota(jnp.int32, (1, LANES), 1)
    m0 = lane < DH
    r2i = lax.broadcasted_iota(jnp.int32, (2 * c, 2 * c), 0)
    c2i = lax.broadcasted_iota(jnp.int32, (2 * c, 2 * c), 1)
    strict = (r2i % c) > (c2i % c)
    lower = (r2i % c) >= (c2i % c)
    tri = tri_ref[...]

    def stack2(z):
        return jnp.concatenate([jnp.where(m0, z, 0.0), jnp.where(m0, 0.0, z)], axis=0)

    chains = [(n, hp, slice(hp * LANES, (hp + 1) * LANES)) for n in range(nb) for hp in range(RW_W // LANES)]
    load = lambda ref: [_pad_chunk(ref[n, :, cs], c) for n, _, cs in chains]
    lws, rs, ks, vs, as_, bs = load(lw_ref), load(r_ref), load(k_ref), load(v_ref), load(a_ref), load(b_ref)
    cums = [_sel_l(tri, lw) for lw in lws]
    a2s = [stack2(a * jnp.exp(cum - lw)).astype(BF16) for a, cum, lw in zip(as_, cums, lws)]
    r2s = [stack2(r * jnp.exp(cum)).astype(BF16) for r, cum in zip(rs, cums)]
    b2s = [stack2(b * jnp.exp(-cum)).astype(BF16) for b, cum in zip(bs, cums)]
    k2s = [stack2(k * jnp.exp(-cum)).astype(BF16) for k, cum in zip(ks, cums)]
    v2s = [stack2(v) for v in vs]
    ars_in = [jnp.concatenate([a2, r2], axis=0) for a2, r2 in zip(a2s, r2s)]
    lms = [_mm_nt(ar, jnp.concatenate([b2, k2], axis=0)) for ar, b2, k2 in zip(ars_in, b2s, k2s)]
    xs = _inv_unit_lower([jnp.where(strict, lm[0:2 * c, 0:2 * c], 0.0) for lm in lms])
    e_ends = [jnp.exp(cum[c - 1:c, :] - cum) for cum in cums]
    xbs = [x.astype(BF16) for x in xs]
    qs = [_mm_tn(xb, stack2(b * e).astype(BF16)) for xb, b, e in zip(xbs, bs, e_ends)]
    wvs = [_mm(jnp.where(strict, lm[0:2 * c, 2 * c:4 * c], 0.0).astype(BF16), v2.astype(BF16))
           for lm, v2 in zip(lms, v2s)]
    ps = [_mm_tn(a2, q.astype(BF16)) for a2, q in zip(a2s, qs)]
    zs = [_mm_tn(jnp.concatenate([wv, v2], axis=0).astype(BF16),
                 jnp.concatenate([q, stack2(k * e)], axis=0).astype(BF16))
          for wv, v2, q, k, e in zip(wvs, v2s, qs, ks, e_ends)]
    sts = [s_ref[n, hp] for n, hp, _ in chains]
    sbs = [st.astype(BF16) for st in sts]
    for (n, hp, _), st, sb, cum, p, z in zip(chains, sts, sbs, cums, ps, zs):
        s_ref[n, hp] = st * jnp.exp(cum[c - 1:c, :]) + _mm(sb, p.astype(BF16)) + z
    arss = [_mm_nt(ar, sb) for ar, sb in zip(ars_in, sbs)]
    u2s = [_mm(xb, (ars[0:2 * c, :] + wv).astype(BF16)) for xb, ars, wv in zip(xbs, arss, wvs)]
    y2s = [ars[2 * c:4 * c, :]
           + _mm(jnp.concatenate([jnp.where(lower, lm[2 * c:4 * c, 0:2 * c], 0.0),
                                  jnp.where(lower, lm[2 * c:4 * c, 2 * c:4 * c], 0.0)], axis=1).astype(BF16),
                 jnp.concatenate([u2, v2], axis=0).astype(BF16))
           for ars, lm, u2, v2 in zip(arss, lms, u2s, v2s)]
    ys = [(y2[0:c, :] + y2[c:2 * c, :])[0:tb, :] for y2 in y2s]
    def head_sum(x):
        s0 = jnp.sum(jnp.where(m0, x, 0.0), axis=-1, keepdims=True)
        return jnp.where(m0, s0, jnp.sum(x, axis=-1, keepdims=True) - s0)

    mus = [head_sum(y) * (1.0 / DH) for y in ys]
    ds = [y - mu for y, mu in zip(ys, mus)]
    vars_ = [head_sum(d * d) * (1.0 / DH) for d in ds]
    bonus = [head_sum(r_ref[n, :, cs] * k_ref[n, :, cs] * rk_ref[:, cs]) for n, _, cs in chains]
    for (n, _, cs), d, var, bo in zip(chains, ds, vars_, bonus):
        yn = d * lax.rsqrt(var + RW_GN_EPS) * lnw_ref[:, cs] + lnb_ref[:, cs]
        o_ref[n, :, cs] = (yn + bo * v_ref[n, :, cs]) * g_ref[n, :, cs]
    so_ref[...] = s_ref[...]


def _rw_scan(parts, w, e, s0_blk, batch, seq):
    nb = 2
    assert batch % nb == 0
    tb = min(CHUNK, seq)
    nt = seq // tb
    npair = RW_W // LANES
    parts = [x.reshape(batch, seq, RW_W) for x in parts]
    tri = jnp.asarray(np.tril(np.ones((CHUNK, CHUNK), np.float32)), BF16)
    rowspec = pl.BlockSpec((nb, tb, RW_W), lambda bb, t: (bb, t, 0))
    stspec = pl.BlockSpec((nb, npair, LANES, LANES), lambda bb, t: (bb, 0, 0, 0))
    cst = lambda shape: pl.BlockSpec(shape, lambda bb, t: (0,) * len(shape))
    o_b, s_out = pl.pallas_call(
        functools.partial(_rw_scan_body, nb=nb, tb=tb),
        grid=(batch // nb, nt),
        in_specs=[rowspec] * 7 + [cst((1, RW_W))] * 3 + [stspec, cst(tri.shape)],
        out_specs=[rowspec, stspec],
        out_shape=[jax.ShapeDtypeStruct((batch, seq, RW_W), F32), jax.ShapeDtypeStruct(s0_blk.shape, F32)],
        scratch_shapes=[pltpu.VMEM((nb, npair, LANES, LANES), F32)],
        compiler_params=_cparams("arbitrary", "arbitrary"), name="rw_scan",
    )(*parts, w['rw_r_k'][e].reshape(1, RW_W), w['rw_ln_w'][e].reshape(1, RW_W),
      w['rw_ln_b'][e].reshape(1, RW_W), s0_blk, tri)
    return o_b.reshape(batch * seq, RW_W), s_out


def _rw_state_to_blocks(s):
    bsz = s.shape[0]
    s = s.reshape(bsz, H_A // 2, 2, DH, DH)
    z = jnp.zeros_like(s[:, :, 0])
    top = jnp.concatenate([s[:, :, 0], z], axis=-1)
    bot = jnp.concatenate([z, s[:, :, 1]], axis=-1)
    return jnp.concatenate([top, bot], axis=-2)


def _rw_blocks_to_state(sb):
    bsz = sb.shape[0]
    return jnp.stack([sb[:, :, :DH, :DH], sb[:, :, DH:, DH:]], axis=2).reshape(bsz, H_A, DH, DH)


def _gdn_body(q_ref, k_ref, v_ref, z_ref, ba_ref, qi_ref, ki_ref, vi_ref, cwq_ref, cwk_ref, cwv_ref,
              par_ref, nw_ref, s0_ref, tri_ref, o_ref, so_ref, qc_ref, kc_ref, vc_ref,
              s_ref, carry_ref, *, nb, tb):
    @pl.when(pl.program_id(1) == 0)
    def _():
        carry_ref[:, 0, 5:8, :] = qi_ref[...]
        carry_ref[:, 1, 5:8, :] = ki_ref[...]
        carry_ref[:, 2, 5:8, :] = vi_ref[...]
        s_ref[...] = s0_ref[...]

    def conv(x, w_ref, n, idx, out_ref):
        prev = carry_ref[n, idx]
        y = x * w_ref[3:4, :]
        for kk in range(1, 4):
            y = y + _shift_rows(x, kk, prev) * w_ref[3 - kk:4 - kk, :]
        tail = x[tb - 3:tb, :]
        carry_ref[n, idx, 5:8, :] = tail
        out_ref[n] = tail
        return y * _sigmoid(y)

    lane = lax.broadcasted_iota(jnp.int32, (1, LANES), 1)

    def column(x, idx):
        return jnp.sum(jnp.where(lane == idx, x, 0.0), axis=-1, keepdims=True)

    c = CHUNK
    r2i = lax.broadcasted_iota(jnp.int32, (2 * c, 2 * c), 0)
    c2i = lax.broadcasted_iota(jnp.int32, (2 * c, 2 * c), 1)
    same = (r2i // c) == (c2i // c)
    strict = same & (r2i > c2i)
    lower = same & (r2i >= c2i)
    tri = tri_ref[...]

    stacked = []
    for n in range(nb):
        q = conv(q_ref[n], cwq_ref, n, 0, qc_ref)
        k = conv(k_ref[n], cwk_ref, n, 1, kc_ref)
        v = conv(v_ref[n], cwv_ref, n, 2, vc_ref)
        ba = ba_ref[n]
        beta_all = _sigmoid(ba)
        z_in = ba + par_ref[1:2, :]
        g_all = -jnp.exp(par_ref[0:1, :]) * (jnp.maximum(z_in, 0.0) + jnp.log1p(jnp.exp(-jnp.abs(z_in))))
        for hp in range(H_C // 2):
            heads = []
            for hh in range(2):
                h = 2 * hp + hh
                sl = slice(h * DK, (h + 1) * DK)
                qh = q[:, sl]
                kh = k[:, sl]
                qn = qh * lax.rsqrt(jnp.sum(qh * qh, axis=-1, keepdims=True) + L2_EPS) * (DK ** -0.5)
                kn = kh * lax.rsqrt(jnp.sum(kh * kh, axis=-1, keepdims=True) + L2_EPS)
                beta = jnp.broadcast_to(column(beta_all, h), (tb, DK))
                g = jnp.broadcast_to(column(g_all, H_C + h), (tb, DK))
                heads.append((qn, kn, v[:, sl], beta, g))
            stacked.append([jnp.concatenate([_pad_chunk(heads[0][i], c), _pad_chunk(heads[1][i], c)], axis=0)
                            for i in range(5)])
    chains = [(n, hp) for n in range(nb) for hp in range(H_C // 2)]
    qn2s, kn2s, v2s, b2s, g2s = [[s[i] for s in stacked] for i in range(5)]
    gcs = [_sel_l(tri, g2) for g2 in g2s]
    grows = [jnp.transpose(gc) for gc in gcs]
    gammas = [jnp.where(lower, jnp.exp(jnp.where(lower, gc - grow, 0.0)), 0.0) for gc, grow in zip(gcs, grows)]
    kbs = [kn2 * b2 for kn2, b2 in zip(kn2s, b2s)]
    kkqks = [_mm_nt(jnp.concatenate([kb, qn2], axis=0).astype(BF16), kn2.astype(BF16))
             for kb, qn2, kn2 in zip(kbs, qn2s, kn2s)]
    tinvs = _inv_unit_lower([jnp.where(strict, -kkqk[0:2 * c, :] * gamma, 0.0) for kkqk, gamma in zip(kkqks, gammas)])
    egs = [jnp.exp(gc) for gc in gcs]
    uws = [_mm(tinv.astype(BF16), jnp.concatenate([kb * eg, v2 * b2], axis=1).astype(BF16))
           for tinv, kb, eg, v2, b2 in zip(tinvs, kbs, egs, v2s, b2s)]
    heads2 = [(ci, hh) for ci in range(len(chains)) for hh in range(2)]
    rows = lambda hh: slice(hh * c, (hh + 1) * c)
    glasts = [gcs[ci][hh * c + c - 1:hh * c + c, :] for ci, hh in heads2]
    pzs = [_mm_tn((kn2s[ci][rows(hh), :] * jnp.exp(gl - gcs[ci][rows(hh), :])).astype(BF16),
                  uws[ci][rows(hh), :].astype(BF16))
           for (ci, hh), gl in zip(heads2, glasts)]
    sts = [s_ref[chains[ci][0], 2 * chains[ci][1] + hh] for ci, hh in heads2]
    sbs = [st.astype(BF16) for st in sts]
    for (ci, hh), st, sb, gl, pz in zip(heads2, sts, sbs, glasts, pzs):
        s_ref[chains[ci][0], 2 * chains[ci][1] + hh] = (st * jnp.exp(gl[:, 0:1])
                                                        - _mm(pz[:, 0:DK].astype(BF16), sb) + pz[:, DK:2 * DK])
    wqs = [_mm(jnp.concatenate([uws[ci][rows(hh), 0:DK], (qn2s[ci] * egs[ci])[rows(hh), :]], axis=0).astype(BF16), sb)
           for (ci, hh), sb in zip(heads2, sbs)]
    o2s = []
    for ci in range(len(chains)):
        vnew = jnp.concatenate([uws[ci][rows(hh), DK:2 * DK] - wqs[2 * ci + hh][0:c, :] for hh in range(2)], axis=0)
        qs = jnp.concatenate([wqs[2 * ci + hh][c:2 * c, :] for hh in range(2)], axis=0)
        amat = jnp.where(lower, kkqks[ci][2 * c:4 * c, :] * gammas[ci], 0.0)
        o2s.append(qs + _mm(amat.astype(BF16), vnew.astype(BF16)))
    for ci, (n, hp) in enumerate(chains):
        for hh in range(2):
            sl = slice((2 * hp + hh) * DK, (2 * hp + hh + 1) * DK)
            oh = o2s[ci][hh * c:hh * c + tb, :]
            zz = z_ref[n, :, sl]
            on = oh * lax.rsqrt(jnp.mean(oh * oh, axis=-1, keepdims=True) + EPS) * nw_ref[...]
            o_ref[n, :, sl] = on * (zz * _sigmoid(zz))
    so_ref[...] = s_ref[...]


def _gdn(p_qkv, p_z, p_ba, conv_prev, s0, w, o_idx, batch, seq):
    nb = 2
    assert batch % nb == 0 and seq >= 3
    tb = min(CHUNK, seq)
    nt = seq // tb
    gk = H_C * DK
    cw = w['gdn_conv_w'][o_idx]
    par = jnp.concatenate([_pad_lanes(w['gdn_A_log'][o_idx], H_C), _pad_lanes(w['gdn_dt_bias'][o_idx], H_C),
                           jnp.zeros((6, LANES), F32)], axis=0)
    idx = np.arange(2 * CHUNK)
    tri = jnp.asarray((((idx[:, None] // CHUNK) == (idx[None, :] // CHUNK))
                       & (idx[:, None] >= idx[None, :])).astype(np.float32), BF16)
    p_qkv = p_qkv.reshape(batch, seq, 3 * gk)
    col = lambda part: pl.BlockSpec((nb, tb, gk), lambda b, t, part=part: (b, t, part))
    prev = lambda part: pl.BlockSpec((nb, 3, gk), lambda b, t, part=part: (b, 0, part))
    wspec = lambda part: pl.BlockSpec((4, gk), lambda b, t, part=part: (0, part))
    cst = lambda shape: pl.BlockSpec(shape, lambda b, t: (0,) * len(shape))
    stspec = pl.BlockSpec((nb, H_C, DK, DK), lambda b, t: (b, 0, 0, 0))
    outs = pl.pallas_call(
        functools.partial(_gdn_body, nb=nb, tb=tb),
        grid=(batch // nb, nt),
        in_specs=[col(0), col(1), col(2), col(0),
                  pl.BlockSpec((nb, tb, LANES), lambda b, t: (b, t, 0)),
                  prev(0), prev(1), prev(2), wspec(0), wspec(1), wspec(2),
                  cst((8, LANES)), cst((1, DK)), stspec, cst(tri.shape)],
        out_specs=[col(0), stspec, prev(0), prev(0), prev(0)],
        out_shape=[jax.ShapeDtypeStruct((batch, seq, gk), F32), jax.ShapeDtypeStruct(s0.shape, F32)]
                  + [jax.ShapeDtypeStruct((batch, 3, gk), F32)] * 3,
        scratch_shapes=[pltpu.VMEM((nb, H_C, DK, DK), F32), pltpu.VMEM((nb, 3, 8, gk), F32)],
        compiler_params=_cparams("arbitrary", "arbitrary"), name="gdn",
    )(p_qkv, p_qkv, p_qkv, p_z.reshape(batch, seq, gk), p_ba.reshape(batch, seq, LANES),
      conv_prev, conv_prev, conv_prev, cw, cw, cw, par, w['gdn_norm_w'][o_idx].reshape(1, DK), s0, tri)
    o_c, s_out, qc, kc, vc = outs
    return o_c.reshape(batch * seq, gk), jnp.concatenate([qc, kc, vc], axis=-1), s_out


def _run_trunk(x, fox_past, rw_state, rw_shift, gdn_state, gdn_conv, ffn_conv, page_table, w, wb):
    batch, seq, d = x.shape
    n = batch * seq
    depth = w['norm_mix'].shape[0]
    tm = 512 if n % 512 == 0 else n
    x = x.reshape(n, d)
    fk, fv, flf, rws, rwsh, gs, gcv, fcv = [], [], [], [], [], [], [], []
    v_first = None
    for layer in range(depth):
        if layer % 2 == 0:
            e = layer // 2
            p_fox, p_rw, p_fl = _norm_matmul(x, w['norm_mix'][layer], wb['ev_in'][e], (4 * FOX_W, 1792, LANES), tm)
            if fox_past is None:
                q_aug, k_aug, v_bf, k_out, lf = _fox_prep(p_fox, p_fl, w['fox_q_gain'][e], w['fox_k_gain'][e],
                                                          w['fox_b_f'][e], batch, seq)
                o_attn = _fox_attn(q_aug, k_aug, v_bf, batch, seq)
            else:
                o_attn, k_out, lf = _fox_decode(p_fox, p_fl, w['fox_q_gain'][e], w['fox_k_gain'][e], w['fox_b_f'][e],
                                                fox_past[0], fox_past[1], fox_past[2], page_table, e, batch, seq)
            parts, sh = _rw_prep(p_rw, rw_shift[e], w, e, v_first, batch, seq)
            if e == 0:
                v_first = parts[3]
            o_b, s_blk = _rw_scan(parts, w, e, _rw_state_to_blocks(rw_state[e]), batch, seq)
            x = _ev_out(o_attn, p_fox, o_b, wb['ev_out'][e], x, tm)
            fk.append(k_out.reshape(batch, seq, H_A, DH))
            fv.append(p_fox[:, 2 * FOX_W:3 * FOX_W].reshape(batch, seq, H_A, DH))
            flf.append(lf[:, :H_A].reshape(batch, seq, H_A))
            rws.append(_rw_blocks_to_state(s_blk))
            rwsh.append(sh)
        else:
            o_idx = layer // 2
            p_qkv, p_z, p_ba = _norm_matmul(x, w['norm_mix'][layer], wb['od_in'][o_idx],
                                            (3 * H_C * DK, H_C * DK, LANES), tm)
            o_c, cv, s_out = _gdn(p_qkv, p_z, p_ba, gdn_conv[o_idx], gdn_state[o_idx], w, o_idx, batch, seq)
            x = _od_out(o_c, wb['od_out'][o_idx], x, tm)
            gs.append(s_out)
            gcv.append(cv)
        x, buf = _ffn(x, w['norm_ffn'][layer], wb['ffn_up'][layer], w['ffn_conv_w'][layer],
                      w['ffn_conv_b'][layer], wb['ffn_down'][layer], ffn_conv[layer], batch, seq)
        fcv.append(buf)
    y = _final_norm(x, w['norm_out'], tm).reshape(batch, seq, d)
    return (y, jnp.stack(fk), jnp.stack(fv), jnp.stack(flf), jnp.stack(rws), jnp.stack(rwsh),
            jnp.stack(gs), jnp.stack(gcv), jnp.stack(fcv))


def _prep_weights(w):
    fox_cols = 4 * FOX_W + H_A
    ev = w['ev_w_in']
    pad = jnp.zeros(ev.shape[:2] + (LANES - H_A,), ev.dtype)
    ev_in = jnp.concatenate([ev[..., :4 * FOX_W], ev[..., fox_cols:], ev[..., 4 * FOX_W:fox_cols], pad], axis=-1)
    od = w['od_w_in']
    pad2 = jnp.zeros(od.shape[:2] + (LANES - 2 * H_C,), od.dtype)
    od_in = jnp.concatenate([od, pad2], axis=-1)
    return dict(ev_in=ev_in.astype(BF16), ev_out=w['ev_w_out'].astype(BF16), od_in=od_in.astype(BF16),
                od_out=w['od_w_out'].astype(BF16), ffn_up=w['ffn_w_up'].astype(BF16),
                ffn_down=w['ffn_w_down'].astype(BF16))


def kernel(x_prompt, x_sample, cache_fox_k, cache_fox_v, cache_fox_logf, state_rwkv, state_rwkv_shift, state_gdn, state_gdn_conv, state_ffn_conv, page_table, norm_mix, norm_ffn, norm_out, ev_w_in, ev_w_out, fox_b_f, fox_q_gain, fox_k_gain, rw_mu, rw_w0, rw_w2, rw_a0, rw_a2, rw_g2, rw_k_k, rw_k_a, rw_r_k, rw_ln_w, rw_ln_b, rw_v0, rw_v1, rw_v2, od_w_in, od_w_out, gdn_conv_w, gdn_A_log, gdn_dt_bias, gdn_norm_w, ffn_w_up, ffn_conv_w, ffn_conv_b, ffn_w_down):
    w = dict(norm_mix=norm_mix, norm_ffn=norm_ffn, norm_out=norm_out, ev_w_in=ev_w_in, ev_w_out=ev_w_out,
             fox_b_f=fox_b_f, fox_q_gain=fox_q_gain, fox_k_gain=fox_k_gain, rw_mu=rw_mu, rw_w0=rw_w0,
             rw_w2=rw_w2, rw_a0=rw_a0, rw_a2=rw_a2, rw_g2=rw_g2, rw_k_k=rw_k_k, rw_k_a=rw_k_a, rw_r_k=rw_r_k,
             rw_ln_w=rw_ln_w, rw_ln_b=rw_ln_b, rw_v0=rw_v0, rw_v1=rw_v1, rw_v2=rw_v2, od_w_in=od_w_in,
             od_w_out=od_w_out, gdn_conv_w=gdn_conv_w, gdn_A_log=gdn_A_log, gdn_dt_bias=gdn_dt_bias,
             gdn_norm_w=gdn_norm_w, ffn_w_up=ffn_w_up, ffn_conv_w=ffn_conv_w, ffn_conv_b=ffn_conv_b,
             ffn_w_down=ffn_w_down)
    wb = _prep_weights(w)
    bp = x_prompt.shape[0]
    n_even, n_odd, depth = ev_w_in.shape[0], od_w_in.shape[0], norm_mix.shape[0]
    rw_cols = rw_mu.shape[1]
    (y_p, fk_p, fv_p, flf_p, rw_p, rwsh_p, gdn_p, gcv_p, fcv_p) = _run_trunk(
        x_prompt, None,
        jnp.zeros((n_even, bp, H_A, DH, DH), F32), jnp.zeros((n_even, bp, rw_cols), F32),
        jnp.zeros((n_odd, bp, H_C, DK, DK), F32), jnp.zeros((n_odd, bp, 3, 3 * H_C * DK), F32),
        jnp.zeros((depth, bp, 2, ffn_w_up.shape[2]), F32), page_table, w, wb)
    (y_s, fk_s, fv_s, flf_s, rw_s, rwsh_s, gdn_s, gcv_s, fcv_s) = _run_trunk(
        x_sample, (cache_fox_k, cache_fox_v, cache_fox_logf), state_rwkv, state_rwkv_shift,
        state_gdn, state_gdn_conv, state_ffn_conv, page_table, w, wb)
    return (y_p, y_s, fk_p, fv_p, flf_p, fk_s, fv_s, flf_s, rw_p, rw_s, rwsh_p, rwsh_s,
            gdn_p, gdn_s, gcv_p, gcv_s, fcv_p, fcv_s)
```

```python
import functools
import math

import jax
import jax.numpy as jnp
import numpy as np
from jax import lax
from jax.experimental import pallas as pl
from jax.experimental.pallas import tpu as pltpu

F32 = jnp.float32
BF16 = jnp.bfloat16

EPS = 1e-6
RW_GN_EPS = 64e-5
L2_EPS = 1e-6
NEG_BIG = -1e30

H_A = 8
DH = 64
FOX_W = 512
RW_W = 512
H_C = 8
DK = 128
CHUNK = 64
LANES = 128
VMEM_LIMIT = 56 * 1024 * 1024


def _cparams(*sem):
    return pltpu.CompilerParams(dimension_semantics=sem, vmem_limit_bytes=VMEM_LIMIT)


def _const_spec(shape):
    nd = len(shape)
    return pl.BlockSpec(shape, lambda *_: (0,) * nd, pipeline_mode=pl.Buffered(1))


def _mm(a, b):
    return jnp.dot(a, b, preferred_element_type=F32)


def _mm_nt(a, b):
    return lax.dot_general(a, b, (((1,), (1,)), ((), ())), preferred_element_type=F32)


def _mm_tn(a, b):
    return lax.dot_general(a, b, (((0,), (0,)), ((), ())), preferred_element_type=F32)


def _split3(x):
    hi = x.astype(BF16)
    r = x - hi.astype(F32)
    mid = r.astype(BF16)
    lo = (r - mid.astype(F32)).astype(BF16)
    return hi, mid, lo


def _sel_l(m01, x):
    hi, mid, lo = _split3(x)
    return _mm(m01, hi) + _mm(m01, mid) + _mm(m01, lo)


def _sel_r(x, m01):
    hi, mid, lo = _split3(x)
    return _mm(hi, m01) + _mm(mid, m01) + _mm(lo, m01)


def _sel_nt(m01, x):
    hi, mid, lo = _split3(x)
    return _mm_nt(m01, hi) + _mm_nt(m01, mid) + _mm_nt(m01, lo)


def _log_sigmoid(z):
    return jnp.minimum(z, 0.0) - jnp.log1p(jnp.exp(-jnp.abs(z)))


def _sigmoid(z):
    return 1.0 / (1.0 + jnp.exp(-z))


def _inv_unit_lower(ns):
    size = ns[0].shape[0]
    r = lax.broadcasted_iota(jnp.int32, (size, size), 0)
    c = lax.broadcasted_iota(jnp.int32, (size, size), 1)
    eye = jnp.where(r == c, 1.0, 0.0)
    ps = [eye + n for n in ns]
    nks = list(ns)
    for _ in range(int(math.log2(CHUNK)) - 1):
        nks = [_mm(nk.astype(BF16), nk.astype(BF16)) for nk in nks]
        ps = [p + _mm(p.astype(BF16), nk.astype(BF16)) for p, nk in zip(ps, nks)]
    return ps


def _head_scale(x2, fn):
    lane_h = lax.broadcasted_iota(jnp.int32, (1, x2.shape[1]), 1) // DH
    out = jnp.zeros_like(x2)
    for h in range(x2.shape[1] // DH):
        s = jnp.sum(x2[:, h * DH:(h + 1) * DH], axis=-1, keepdims=True)
        out = jnp.where(lane_h == h, fn(s), out)
    return out


def _rms_heads(x, gain):
    return x * _head_scale(x * x, lambda s: lax.rsqrt(s * (1.0 / DH) + EPS)) * gain


def _shift_rows(x, k, prev):
    rolled = pltpu.roll(x, k, 0)
    sub = 8
    row = lax.broadcasted_iota(jnp.int32, (sub, x.shape[1]), 0)
    head = rolled[0:sub, :]
    nprev = prev.shape[0]
    for j in range(k):
        head = jnp.where(row == j, prev[nprev - k + j:nprev - k + j + 1, :], head)
    return head if x.shape[0] == sub else jnp.concatenate([head, rolled[sub:, :]], axis=0)


def _nm_body(x_ref, g_ref, w_ref, *o_refs, splits):
    x = x_ref[...]
    xn = (x * lax.rsqrt(jnp.mean(x * x, axis=-1, keepdims=True) + EPS) * g_ref[...]).astype(BF16)
    off = 0
    for o_ref, n in zip(o_refs, splits):
        o_ref[...] = _mm(xn, w_ref[:, off:off + n])
        off += n


def _norm_matmul(x, g, w_bf, splits, tm):
    n, d = x.shape
    ntot = w_bf.shape[1]
    assert sum(splits) == ntot and n % tm == 0
    return pl.pallas_call(
        functools.partial(_nm_body, splits=tuple(splits)),
        grid=(n // tm,),
        in_specs=[pl.BlockSpec((tm, d), lambda i: (i, 0)), _const_spec((1, d)), _const_spec((d, ntot))],
        out_specs=[pl.BlockSpec((tm, s), lambda i: (i, 0)) for s in splits],
        out_shape=[jax.ShapeDtypeStruct((n, s), F32) for s in splits],
        compiler_params=_cparams("arbitrary"),
        name="norm_matmul",
    )(x, g.reshape(1, d), w_bf)


def _final_norm_body(x_ref, g_ref, o_ref):
    x = x_ref[...]
    o_ref[...] = x * lax.rsqrt(jnp.mean(x * x, axis=-1, keepdims=True) + EPS) * g_ref[...]


def _final_norm(x, g, tm):
    n, d = x.shape
    return pl.pallas_call(
        _final_norm_body, grid=(n // tm,),
        in_specs=[pl.BlockSpec((tm, d), lambda i: (i, 0)), _const_spec((1, d))],
        out_specs=pl.BlockSpec((tm, d), lambda i: (i, 0)),
        out_shape=jax.ShapeDtypeStruct((n, d), F32),
        compiler_params=_cparams("arbitrary"), name="final_norm",
    )(x, g.reshape(1, d))


def _evout_body(oa_ref, og_ref, ob_ref, w_ref, x_ref, o_ref):
    a = (oa_ref[...] * _sigmoid(og_ref[...])).astype(BF16)
    b = ob_ref[...].astype(BF16)
    o_ref[...] = x_ref[...] + _mm(a, w_ref[0:FOX_W, :]) + _mm(b, w_ref[FOX_W:FOX_W + RW_W, :])


def _ev_out(o_attn, p_fox, o_b, w_bf, x, tm):
    n, d = x.shape
    return pl.pallas_call(
        _evout_body, grid=(n // tm,),
        in_specs=[pl.BlockSpec((tm, FOX_W), lambda i: (i, 0)),
                  pl.BlockSpec((tm, FOX_W), lambda i: (i, 3)),
                  pl.BlockSpec((tm, RW_W), lambda i: (i, 0)),
                  _const_spec((FOX_W + RW_W, d)),
                  pl.BlockSpec((tm, d), lambda i: (i, 0))],
        out_specs=pl.BlockSpec((tm, d), lambda i: (i, 0)),
        out_shape=jax.ShapeDtypeStruct((n, d), F32),
        compiler_params=_cparams("arbitrary"), name="ev_out",
    )(o_attn, p_fox, o_b, w_bf, x)


def _odout_body(oc_ref, w_ref, x_ref, o_ref):
    o_ref[...] = x_ref[...] + _mm(oc_ref[...].astype(BF16), w_ref[...])


def _od_out(o_c, w_bf, x, tm):
    n, d = x.shape
    k = o_c.shape[1]
    return pl.pallas_call(
        _odout_body, grid=(n // tm,),
        in_specs=[pl.BlockSpec((tm, k), lambda i: (i, 0)), _const_spec((k, d)),
                  pl.BlockSpec((tm, d), lambda i: (i, 0))],
        out_specs=pl.BlockSpec((tm, d), lambda i: (i, 0)),
        out_shape=jax.ShapeDtypeStruct((n, d), F32),
        compiler_params=_cparams("arbitrary"), name="od_out",
    )(o_c, w_bf, x)


def _ffn_cols(xn, wup_ref, cw_ref, cb_ref, wdn_ref, acc, prev_fn, tail_fn, f, cwb):
    for c in range(f // cwb):
        ys = []
        for half in (0, 1):
            lo = half * f + c * cwb
            h = _mm(xn, wup_ref[:, lo:lo + cwb])
            hm1, hm2 = prev_fn(h, lo)
            ys.append(hm2 * cw_ref[0:1, lo:lo + cwb] + hm1 * cw_ref[1:2, lo:lo + cwb]
                      + h * cw_ref[2:3, lo:lo + cwb] + cb_ref[:, lo:lo + cwb])
            tail_fn(h, lo)
        u, gt = ys
        act = (gt * _sigmoid(gt) * u).astype(BF16)
        acc = acc + _mm(act, wdn_ref[c * cwb:(c + 1) * cwb, :])
    return acc


def _ffn_norm(x_ref, g_ref):
    x = x_ref[...]
    return x, (x * lax.rsqrt(jnp.mean(x * x, axis=-1, keepdims=True) + EPS) * g_ref[...]).astype(BF16)


def _ffn_seq_body(x_ref, g_ref, wup_ref, cw_ref, cb_ref, wdn_ref, init_ref, o_ref, st_ref, carry_ref,
                  *, tb, f, cwb):
    @pl.when(pl.program_id(1) == 0)
    def _():
        carry_ref[0:2, :] = init_ref[0]

    x, xn = _ffn_norm(x_ref, g_ref)

    def prev_fn(h, lo):
        prev = carry_ref[0:2, lo:lo + cwb]
        return _shift_rows(h, 1, prev), _shift_rows(h, 2, prev)

    def tail_fn(h, lo):
        carry_ref[0:2, lo:lo + cwb] = h[tb - 2:tb, :]
        st_ref[0, :, lo:lo + cwb] = h[tb - 2:tb, :]

    o_ref[...] = _ffn_cols(xn, wup_ref, cw_ref, cb_ref, wdn_ref, x, prev_fn, tail_fn, f, cwb)


def _ffn_flat_body(x_ref, g_ref, wup_ref, cw_ref, cb_ref, wdn_ref, f1_ref, f2_ref, o_ref, st_ref,
                   *, rows, seq, f, cwb):
    x, xn = _ffn_norm(x_ref, g_ref)
    tmod = lax.broadcasted_iota(jnp.int32, (rows, cwb), 0) % seq

    def prev_fn(h, lo):
        hm1 = jnp.where(tmod == 0, f1_ref[:, lo:lo + cwb], pltpu.roll(h, 1, 0))
        hm2 = jnp.where(tmod < 2, f2_ref[:, lo:lo + cwb], pltpu.roll(h, 2, 0))
        return hm1, hm2

    def tail_fn(h, lo):
        st_ref[:, :, lo:lo + cwb] = h.reshape(rows // seq, seq, cwb)[:, seq - 2:seq, :]

    o_ref[...] = _ffn_cols(xn, wup_ref, cw_ref, cb_ref, wdn_ref, x, prev_fn, tail_fn, f, cwb)


def _ffn(x, g, wup_bf, conv_w, conv_b, wdn_bf, conv_prev, batch, seq):
    n, d = x.shape
    f2 = wup_bf.shape[1]
    f = f2 // 2
    cwb = f
    weights = [_const_spec((1, d)), _const_spec((d, f2)), _const_spec((3, f2)), _const_spec((1, f2)),
               _const_spec((f, d))]
    out_shape = [jax.ShapeDtypeStruct((n, d), F32), jax.ShapeDtypeStruct((batch, 2, f2), F32)]
    args = (x, g.reshape(1, d), wup_bf, conv_w, conv_b.reshape(1, f2), wdn_bf)
    if seq >= 256:
        tb = 512 if seq % 512 == 0 else 256
        nt = seq // tb
        return pl.pallas_call(
            functools.partial(_ffn_seq_body, tb=tb, f=f, cwb=cwb),
            grid=(batch, nt),
            in_specs=[pl.BlockSpec((tb, d), lambda b, t: (b * nt + t, 0))] + weights
                     + [pl.BlockSpec((1, 2, f2), lambda b, t: (b, 0, 0))],
            out_specs=[pl.BlockSpec((tb, d), lambda b, t: (b * nt + t, 0)),
                       pl.BlockSpec((1, 2, f2), lambda b, t: (b, 0, 0))],
            out_shape=out_shape,
            scratch_shapes=[pltpu.VMEM((8, f2), F32)],
            compiler_params=_cparams("arbitrary", "arbitrary"), name="ffn_seq",
        )(*args, conv_prev)
    zeros = jnp.zeros((batch, seq - 2, f2), F32)
    fill2 = jnp.concatenate([conv_prev, zeros], axis=1).reshape(n, f2)
    fill1 = jnp.concatenate([conv_prev[:, 1:2], zeros, zeros[:, :1]], axis=1).reshape(n, f2)
    return pl.pallas_call(
        functools.partial(_ffn_flat_body, rows=n, seq=seq, f=f, cwb=cwb),
        grid=(1,),
        in_specs=[pl.BlockSpec((n, d), lambda i: (0, 0))] + weights
                 + [pl.BlockSpec((n, f2), lambda i: (0, 0)), pl.BlockSpec((n, f2), lambda i: (0, 0))],
        out_specs=[pl.BlockSpec((n, d), lambda i: (0, 0)), pl.BlockSpec((batch, 2, f2), lambda i: (0, 0, 0))],
        out_shape=out_shape,
        compiler_params=_cparams("arbitrary"), name="ffn_flat",
    )(*args, fill1, fill2)


def _fox_consts():
    src = np.arange(FOX_W)
    place = np.zeros((FOX_W, H_A * LANES), np.float32)
    place[src, (src // DH) * LANES + src % DH] = 1.0
    hh = np.arange(H_A)
    eq = np.zeros((3, LANES, H_A * LANES), np.float32)
    ek = np.zeros((3, LANES, H_A * LANES), np.float32)
    cq = np.zeros((1, H_A * LANES), np.float32)
    ck = np.zeros((1, H_A * LANES), np.float32)
    for piece in range(3):
        eq[piece, hh, hh * LANES + DH + piece] = 1.0
        ek[piece, hh, hh * LANES + DH + 3 + piece] = -1.0
        cq[0, hh * LANES + DH + 3 + piece] = 1.0
        ck[0, hh * LANES + DH + piece] = 1.0
    as_bf = lambda a: jnp.asarray(a, BF16)
    return as_bf(place), as_bf(place.T), as_bf(eq), as_bf(ek), jnp.asarray(cq), jnp.asarray(ck)


def _fox_prep_body(q_ref, k_ref, v_ref, fl_ref, qg_ref, kg_ref, bf_ref, pm_ref, pt_ref, eq_ref, ek_ref, cq_ref,
                   ck_ref, vone_ref, tri_ref, bd_ref, qa_ref, ka_ref, vt_ref, ko_ref, lf_ref, carry_ref, *, tm):
    @pl.when(pl.program_id(1) == 0)
    def _():
        carry_ref[...] = jnp.zeros_like(carry_ref)

    bd = bd_ref[...]

    def rms_heads(x, gain):
        x2 = x * x
        hi = x2.astype(BF16)
        mid = (x2 - hi.astype(F32)).astype(BF16)
        ms = (_mm(hi, bd) + _mm(mid, bd)) * (1.0 / DH)
        return x * lax.rsqrt(ms + EPS) * gain

    qn = rms_heads(q_ref[...], qg_ref[...]) * (DH ** -0.5)
    kn = rms_heads(k_ref[...], kg_ref[...])
    lf = _log_sigmoid(fl_ref[...] + bf_ref[...])
    lf_ref[...] = lf
    fcum = _sel_l(tri_ref[...], lf) + carry_ref[0:1, :]
    carry_ref[0:1, :] = fcum[tm - 1:tm, :]
    fh, fm, flo = _split3(fcum)
    pm = pm_ref[...]
    qa = (_mm(qn.astype(BF16), pm) + _mm(fh, eq_ref[0]) + _mm(fm, eq_ref[1]) + _mm(flo, eq_ref[2])
          + cq_ref[...])
    ka = (_mm(kn.astype(BF16), pm) + _mm(fh, ek_ref[0]) + _mm(fm, ek_ref[1]) + _mm(flo, ek_ref[2])
          + ck_ref[...])
    qa_ref[...] = qa.astype(BF16)
    ka_ref[...] = ka.astype(BF16)
    vt = _mm_nt(pt_ref[...], v_ref[...].astype(BF16)) + vone_ref[...]
    vt_ref[0, :, 0] = vt.astype(BF16).reshape(H_A, LANES, tm)
    ko_ref[...] = kn


FOX_TILE = 512
FOX_QSUB = 4


def _fox_tile(seq):
    return min(FOX_TILE, seq // FOX_QSUB)


def _pad_lanes(v, offset=0):
    return jnp.concatenate([jnp.zeros((offset,), F32), v.astype(F32),
                            jnp.zeros((LANES - offset - v.shape[0],), F32)]).reshape(1, LANES)


def _fox_prep(p_fox, p_fl, q_gain, k_gain, b_f, batch, seq):
    n = p_fox.shape[0]
    tm = _fox_tile(seq)
    nt = seq // tm
    place, place_t, eq, ek, cq, ck = _fox_consts()
    vone = np.zeros((H_A * LANES, 1), np.float32)
    vone[np.arange(H_A) * LANES + DH, 0] = 1.0
    vone = jnp.asarray(vone)
    tri = jnp.asarray(np.tril(np.ones((tm, tm), np.float32)), BF16)
    lane = np.arange(FOX_W)
    bd = jnp.asarray((lane[:, None] // DH == lane[None, :] // DH).astype(np.float32), BF16)
    bf = _pad_lanes(b_f)
    row = lambda c: pl.BlockSpec((tm, FOX_W), lambda b, t, c=c: (b * nt + t, c))
    wide = pl.BlockSpec((tm, H_A * LANES), lambda b, t: (b * nt + t, 0))
    narrow = pl.BlockSpec((tm, LANES), lambda b, t: (b * nt + t, 0))
    return pl.pallas_call(
        functools.partial(_fox_prep_body, tm=tm),
        grid=(batch, nt),
        in_specs=[row(0), row(1), row(2), narrow,
                  _const_spec((1, FOX_W)), _const_spec((1, FOX_W)), _const_spec((1, LANES)),
                  _const_spec(place.shape), _const_spec(place_t.shape), _const_spec(eq.shape), _const_spec(ek.shape),
                  _const_spec(cq.shape), _const_spec(ck.shape), _const_spec(vone.shape), _const_spec(tri.shape),
                  _const_spec(bd.shape)],
        out_specs=[wide, wide, pl.BlockSpec((1, H_A, 1, LANES, tm), lambda b, t: (b, 0, t, 0, 0)), row(0), narrow],
        out_shape=[jax.ShapeDtypeStruct((n, H_A * LANES), BF16), jax.ShapeDtypeStruct((n, H_A * LANES), BF16),
                   jax.ShapeDtypeStruct((batch, H_A, nt, LANES, tm), BF16), jax.ShapeDtypeStruct((n, FOX_W), F32),
                   jax.ShapeDtypeStruct((n, LANES), F32)],
        scratch_shapes=[pltpu.VMEM((8, LANES), F32)],
        compiler_params=_cparams("arbitrary", "arbitrary"), name="fox_prep",
    )(p_fox, p_fox, p_fox, p_fl, jnp.tile(q_gain, H_A).reshape(1, FOX_W),
      jnp.tile(k_gain, H_A).reshape(1, FOX_W), bf, place, place_t, eq, ek, cq, ck, vone, tri, bd)


def _fox_attn_body(qa_ref, ka_ref, vt_ref, o_ref, *, tq):
    i = pl.program_id(2)
    key = lax.broadcasted_iota(jnp.int32, (tq, tq), 0)
    qry = lax.broadcasted_iota(jnp.int32, (tq, tq), 1)
    causal = key <= qry
    nsb = FOX_QSUB
    chains = [(sb, hh) for sb in range(nsb) for hh in range(2)]
    qs = [qa_ref[sb * tq:(sb + 1) * tq, hh * LANES:(hh + 1) * LANES] for sb, hh in chains]

    def block(j, carry, active):
        off = pl.multiple_of(j * tq, tq)
        ks = [ka_ref[pl.ds(off, tq), hh * LANES:(hh + 1) * LANES] for hh in range(2)]
        vts = [vt_ref[0, hh, j] for hh in range(2)]
        sts = {c: _mm_nt(ks[chains[c][1]], qs[c]) for c in active}
        sts = {c: jnp.where(causal, st, NEG_BIG) if active[c] else st for c, st in sts.items()}
        m2s = {c: jnp.maximum(carry[c][0], jnp.max(st, axis=0, keepdims=True)) for c, st in sts.items()}
        ps = {c: jnp.exp(st - m2s[c]).astype(BF16) for c, st in sts.items()}
        return tuple((m2s[c], jnp.exp(carry[c][0] - m2s[c]) * carry[c][1] + _mm(vts[chains[c][1]], ps[c]))
                     if c in active else carry[c] for c in range(len(chains)))

    init = tuple((jnp.full((1, tq), NEG_BIG, F32), jnp.zeros((LANES, tq), F32)) for _ in chains)
    carry = lax.fori_loop(0, nsb * i, lambda j, c: block(j, c, {c_: False for c_ in range(len(chains))}), init)
    for d in range(nsb):
        carry = block(nsb * i + d, carry, {c: chains[c][0] == d for c in range(len(chains)) if chains[c][0] >= d})
    for sb in range(nsb):
        halves = [acc[0:DH, :] / acc[DH:DH + 1, :] for _, acc in carry[2 * sb:2 * sb + 2]]
        o_ref[sb * tq:(sb + 1) * tq, :] = jnp.transpose(jnp.concatenate(halves, axis=0))


def _fox_attn(q_aug, k_aug, v_t, batch, seq):
    n = q_aug.shape[0]
    tq = _fox_tile(seq)
    rows = FOX_QSUB * tq
    nq = seq // rows
    return pl.pallas_call(
        functools.partial(_fox_attn_body, tq=tq),
        grid=(batch, H_A // 2, nq),
        in_specs=[pl.BlockSpec((rows, 2 * LANES), lambda b, hp, i: (b * nq + i, hp)),
                  pl.BlockSpec((seq, 2 * LANES), lambda b, hp, i: (b, hp)),
                  pl.BlockSpec((1, 2, seq // tq, LANES, tq), lambda b, hp, i: (b, hp, 0, 0, 0))],
        out_specs=pl.BlockSpec((rows, LANES), lambda b, hp, i: (b * nq + i, hp)),
        out_shape=jax.ShapeDtypeStruct((n, FOX_W), F32),
        compiler_params=_cparams("arbitrary", "arbitrary", "arbitrary"), name="fox_attn",
    )(q_aug, k_aug, v_t)


def _fox_decode_body(pt_ref, q_ref, k_ref, v_ref, fl_ref, qg_ref, kg_ref, bf_ref, ms_ref, ps_ref, pa_ref,
                     asel_ref, *rest, nb, pps, seq):
    del pt_ref
    npg = nb * pps
    kp, vp, lp = rest[:npg], rest[npg:2 * npg], rest[2 * npg:3 * npg]
    o_ref, ko_ref, lfo_ref = rest[3 * npg:3 * npg + 3]
    qh_ref, cb_ref, m_ref, l_ref, acc_ref, car_ref = rest[3 * npg + 3:]
    j = pl.program_id(1)
    nrow = H_A * seq

    @pl.when(j == 0)
    def _new_tokens():
        q_all = _rms_heads(q_ref[...], qg_ref[...]) * (DH ** -0.5)
        k_all = _rms_heads(k_ref[...], kg_ref[...])
        ko_ref[...] = k_all
        lf_all = _log_sigmoid(fl_ref[...] + bf_ref[...])
        lfo_ref[...] = lf_all
        row = lax.broadcasted_iota(jnp.int32, (seq, LANES), 0)
        zpad = jnp.zeros((LANES - seq, DH), F32)
        keyi = lax.broadcasted_iota(jnp.int32, (nrow, LANES), 1)
        ti = lax.broadcasted_iota(jnp.int32, (nrow, LANES), 0) % seq
        for n in range(nb):
            rs = slice(n * seq, (n + 1) * seq)
            q, k, v = q_all[rs, :], k_all[rs, :], v_ref[rs, :]
            c = lf_all[rs, :]
            s = 1
            while s < seq:
                c = c + jnp.where(row >= s, pltpu.roll(c, s, 0), 0.0)
                s *= 2
            srows = []
            for h in range(H_A):
                qh = q[:, h * DH:(h + 1) * DH]
                qh_ref[n, h] = qh
                cb_ref[n, h * seq:(h + 1) * seq, :] = jnp.broadcast_to(c[:, h:h + 1], (seq, LANES))
                kpad = jnp.concatenate([k[:, h * DH:(h + 1) * DH], zpad], axis=0).astype(BF16)
                srows.append(_mm_nt(qh.astype(BF16), kpad))
            cneg = jnp.concatenate([-c, jnp.zeros((LANES - seq, LANES), F32)], axis=0)
            s_new = jnp.concatenate(srows, axis=0) + cb_ref[n] + _sel_nt(asel_ref[0:nrow, :], cneg)
            s_new = jnp.where(keyi <= ti, s_new, NEG_BIG)
            m = jnp.max(s_new, axis=-1, keepdims=True)
            p = jnp.exp(s_new - m)
            m_ref[n] = m
            l_ref[n] = jnp.sum(p, axis=-1, keepdims=True)
            accs = []
            for h in range(H_A):
                vpad = jnp.concatenate([v[:, h * DH:(h + 1) * DH], zpad], axis=0).astype(BF16)
                accs.append(_mm(p[h * seq:(h + 1) * seq, :].astype(BF16), vpad))
            acc_ref[n] = jnp.concatenate(accs, axis=0)
        car_ref[...] = jnp.zeros_like(car_ref)

    seqs = range(nb)
    lfts = [jnp.concatenate([lp[n * pps + i][...] for i in range(pps)]
                            + [jnp.zeros((LANES - pps * H_A, LANES), F32)], axis=0) for n in seqs]
    tots = [jnp.broadcast_to(jnp.sum(lft, axis=1, keepdims=True), (LANES, LANES)) for lft in lfts]
    scores = [[jnp.concatenate([_mm(qh_ref[n, h].astype(BF16), kp[n * pps + i][h].astype(BF16))
                                for h in range(H_A)], axis=0) for i in range(pps)] for n in seqs]
    cars = [car_ref[n] for n in seqs]
    rfulls = [_sel_r(lft, ms_ref[...]) + _sel_l(ps_ref[...], tot) + car for lft, tot, car in zip(lfts, tots, cars)]
    for n in seqs:
        car_ref[n] = cars[n] + _sel_l(pa_ref[...], tots[n])
    biases = [_sel_l(asel_ref[...], rfull) for rfull in rfulls]
    s_alls = [jnp.concatenate([scores[n][i] + biases[n][i * nrow:(i + 1) * nrow, :] + cb_ref[n]
                               for i in range(pps)], axis=1) for n in seqs]
    m_olds = [m_ref[n] for n in seqs]
    m_news = [jnp.maximum(m_old, jnp.max(s_all, axis=-1, keepdims=True)) for m_old, s_all in zip(m_olds, s_alls)]
    probs = [jnp.exp(s_all - m_new) for s_all, m_new in zip(s_alls, m_news)]
    alphas = [jnp.exp(m_old - m_new) for m_old, m_new in zip(m_olds, m_news)]
    pvs = [jnp.concatenate(
        [_mm_nt(probs[n][h * seq:(h + 1) * seq, :].astype(BF16),
                jnp.concatenate([vp[n * pps + i][h] for i in range(pps)], axis=1).astype(BF16))
         for h in range(H_A)], axis=0) for n in seqs]
    for n in seqs:
        m_ref[n] = m_news[n]
        l_ref[n] = alphas[n] * l_ref[n] + jnp.sum(probs[n], axis=-1, keepdims=True)
        acc_ref[n] = alphas[n] * acc_ref[n] + pvs[n]

    @pl.when(j == pl.num_programs(1) - 1)
    def _():
        for n in seqs:
            o = acc_ref[n] / l_ref[n]
            for h in range(H_A):
                o_ref[n * seq:(n + 1) * seq, h * DH:(h + 1) * DH] = o[h * seq:(h + 1) * seq, :]


def _fox_decode(p_fox, p_fl, q_gain, k_gain, b_f, pool_k, pool_v, pool_lf, page_table, e, batch, seq):
    n = p_fox.shape[0]
    n_pages = page_table.shape[1]
    page = pool_lf.shape[2]
    nb = 2
    assert page == LANES and seq == 8 and batch % nb == 0
    pps = 8
    while n_pages % pps:
        pps //= 2
    groups = n_pages // pps
    pk = jnp.transpose(pool_k, (0, 1, 3, 4, 2))
    pv = jnp.transpose(pool_v, (0, 1, 3, 4, 2))
    plf = jnp.transpose(pool_lf, (0, 1, 3, 2))
    nrow = H_A * seq
    idx = np.arange(LANES)
    as_bf = lambda a: jnp.asarray(a.astype(np.float32), BF16)
    ms = as_bf(idx[:, None] > idx[None, :])
    same_h = (idx[:, None] % H_A) == (idx[None, :] % H_A)
    valid = (idx[:, None] < pps * H_A) & (idx[None, :] < pps * H_A)
    ps = as_bf(same_h & valid & (idx[None, :] // H_A < idx[:, None] // H_A))
    pa = as_bf(same_h & valid)
    r = np.arange(pps * nrow)
    asel = np.zeros((pps * nrow, LANES), np.float32)
    asel[r, (r // nrow) * H_A + (r % nrow) // seq] = 1.0
    asel = as_bf(asel)
    bf = _pad_lanes(b_f)

    def page_spec(s, i, shape):
        def index(b, j, pt):
            return (e, pt[b * nb + s, n_pages - 1 - (j * pps + i)]) + (0,) * len(shape)
        return pl.BlockSpec((None, None) + shape, index)

    pages = lambda shape: [page_spec(s, i, shape) for s in range(nb) for i in range(pps)]
    rowspec = lambda c: pl.BlockSpec((nb * seq, FOX_W), lambda b, j, pt, c=c: (b, c))
    narrow = pl.BlockSpec((nb * seq, LANES), lambda b, j, pt: (b, 0))
    cst = lambda shape: pl.BlockSpec(shape, lambda b, j, pt: (0,) * len(shape))
    in_specs = ([rowspec(0), rowspec(1), rowspec(2), narrow,
                 cst((1, FOX_W)), cst((1, FOX_W)), cst((1, LANES)), cst(ms.shape), cst(ps.shape), cst(pa.shape),
                 cst(asel.shape)]
                + pages((H_A, DH, page)) + pages((H_A, DH, page)) + pages((H_A, page)))
    grid_spec = pltpu.PrefetchScalarGridSpec(
        num_scalar_prefetch=1, grid=(batch // nb, groups), in_specs=in_specs,
        out_specs=[rowspec(0), rowspec(0), narrow],
        scratch_shapes=[pltpu.VMEM((nb, H_A, seq, DH), F32), pltpu.VMEM((nb, nrow, LANES), F32),
                        pltpu.VMEM((nb, nrow, 1), F32), pltpu.VMEM((nb, nrow, 1), F32),
                        pltpu.VMEM((nb, nrow, DH), F32), pltpu.VMEM((nb, LANES, LANES), F32)])
    return pl.pallas_call(
        functools.partial(_fox_decode_body, nb=nb, pps=pps, seq=seq),
        grid_spec=grid_spec,
        out_shape=[jax.ShapeDtypeStruct((n, FOX_W), F32), jax.ShapeDtypeStruct((n, FOX_W), F32),
                   jax.ShapeDtypeStruct((n, LANES), F32)],
        compiler_params=_cparams("arbitrary", "arbitrary"), name="fox_decode",
    )(page_table, p_fox, p_fox, p_fox, p_fl, jnp.tile(q_gain, H_A).reshape(1, FOX_W),
      jnp.tile(k_gain, H_A).reshape(1, FOX_W), bf, ms, ps, pa, asel,
      *([pk] * (nb * pps)), *([pv] * (nb * pps)), *([plf] * (nb * pps)))


def _rw_prep_body(p_ref, init_ref, mu_ref, w0_ref, w2_ref, a0_ref, a2_ref, g2_ref, kk_ref, ka_ref, *rest,
                  tb, first):
    if first:
        r_o, lw_o, k_o, v_o, a_o, b_o, g_o, sh_o, carry_ref = rest
    else:
        v0_ref, v1_ref, v2_ref, vf_ref, r_o, lw_o, k_o, v_o, a_o, b_o, g_o, sh_o, carry_ref = rest

    @pl.when(pl.program_id(1) == 0)
    def _():
        carry_ref[7:8, :] = init_ref[0]

    p = p_ref[...]
    p_prev = _shift_rows(p, 1, carry_ref[...])
    carry_ref[7:8, :] = p[tb - 1:tb, :]
    sh_o[0] = p[tb - 1:tb, :]
    ps = p + (p_prev - p) * mu_ref[...]
    r = ps[:, 0:RW_W]
    k = ps[:, RW_W:2 * RW_W]
    v = ps[:, 2 * RW_W:3 * RW_W]
    x128 = ps[:, 3 * RW_W:3 * RW_W + LANES]
    gd = ps[:, 3 * RW_W + LANES:3 * RW_W + 2 * LANES]
    lane = lax.broadcasted_iota(jnp.int32, (1, LANES), 1)
    xw = jnp.where(lane < DH, jnp.tanh(x128), 0.0).astype(BF16)
    xa = jnp.where(lane < DH, 0.0, x128).astype(BF16)
    w_log = _log_sigmoid(w0_ref[...] + _mm(xw, w2_ref[...])) - 0.5
    lw_o[...] = -jnp.exp(w_log)
    a = _sigmoid(a0_ref[...] + _mm(xa, a2_ref[...]))
    g_o[...] = _mm(_sigmoid(gd).astype(BF16), g2_ref[...])
    if not first:
        gate = _sigmoid(v0_ref[...] + _mm(_mm(v.astype(BF16), v1_ref[...]).astype(BF16), v2_ref[...]))
        v = v + (vf_ref[...] - v) * gate
    kkx = k * kk_ref[...]
    kk = kkx * _head_scale(kkx * kkx, lambda s: lax.rsqrt(s + L2_EPS))
    r_o[...] = r
    k_o[...] = k * (1.0 + (a - 1.0) * ka_ref[...])
    v_o[...] = v
    a_o[...] = -kk
    b_o[...] = kk * a


def _rw_prep(p_rw, shift_prev, w, e, v_first, batch, seq):
    n, cols = p_rw.shape
    tb = min(256, seq)
    nt = seq // tb
    first = e == 0
    pad_rows = lambda m: jnp.concatenate([m, jnp.zeros((LANES - m.shape[0], m.shape[1]), m.dtype)], axis=0)
    w2p = pad_rows(w['rw_w2'][e]).astype(BF16)
    a2p = jnp.concatenate([jnp.zeros((DH, RW_W), F32), w['rw_a2'][e]], axis=0).astype(BF16)
    vec = lambda x: x.reshape(1, -1)
    args = [p_rw, shift_prev.reshape(batch, 1, cols), vec(w['rw_mu'][e]), vec(w['rw_w0'][e]), w2p,
            vec(w['rw_a0'][e]), a2p, w['rw_g2'][e].astype(BF16), vec(w['rw_k_k'][e]), vec(w['rw_k_a'][e])]
    rowspec = pl.BlockSpec((tb, RW_W), lambda b, t: (b * nt + t, 0))
    in_specs = [pl.BlockSpec((tb, cols), lambda b, t: (b * nt + t, 0)),
                pl.BlockSpec((1, 1, cols), lambda b, t: (b, 0, 0)),
                _const_spec((1, cols)), _const_spec((1, RW_W)), _const_spec((LANES, RW_W)),
                _const_spec((1, RW_W)), _const_spec((LANES, RW_W)), _const_spec((LANES, RW_W)),
                _const_spec((1, RW_W)), _const_spec((1, RW_W))]
    if not first:
        v1p = jnp.concatenate([w['rw_v1'][e - 1], jnp.zeros((RW_W, LANES - w['rw_v1'].shape[2]), F32)], axis=1)
        args += [vec(w['rw_v0'][e - 1]), v1p.astype(BF16), pad_rows(w['rw_v2'][e - 1]).astype(BF16), v_first]
        in_specs += [_const_spec((1, RW_W)), _const_spec((RW_W, LANES)), _const_spec((LANES, RW_W)), rowspec]
    outs = pl.pallas_call(
        functools.partial(_rw_prep_body, tb=tb, first=first),
        grid=(batch, nt), in_specs=in_specs,
        out_specs=[rowspec] * 7 + [pl.BlockSpec((1, 1, cols), lambda b, t: (b, 0, 0))],
        out_shape=[jax.ShapeDtypeStruct((n, RW_W), F32)] * 7 + [jax.ShapeDtypeStruct((batch, 1, cols), F32)],
        scratch_shapes=[pltpu.VMEM((8, cols), F32)],
        compiler_params=_cparams("arbitrary", "arbitrary"), name="rw_prep",
    )(*args)
    return outs[:7], outs[7].reshape(batch, cols)


def _pad_chunk(x, rows):
    if x.shape[0] == rows:
        return x
    return jnp.concatenate([x, jnp.zeros((rows - x.shape[0], x.shape[1]), x.dtype)], axis=0)


def _rw_scan_body(r_ref, lw_ref, k_ref, v_ref, a_ref, b_ref, g_ref, rk_ref, lnw_ref, lnb_ref, s0_ref, tri_ref,
                  o_ref, so_ref, s_ref, *, nb, tb):
    @pl.when(pl.program_id(1) == 0)
    def _():
        s_ref[...] = s0_ref[...]

    c = CHUNK
    lane = lax.broadcasted_iota(jnp.int32, (1, LANES), 1)
    m0 = lane < DH
    r2i = lax.broadcasted_iota(jnp.int32, (2 * c, 2 * c), 0)
    c2i = lax.broadcasted_iota(jnp.int32, (2 * c, 2 * c), 1)
    strict = (r2i % c) > (c2i % c)
    lower = (r2i % c) >= (c2i % c)
    tri = tri_ref[...]

    def stack2(z):
        return jnp.concatenate([jnp.where(m0, z, 0.0), jnp.where(m0, 0.0, z)], axis=0)

    chains = [(n, hp, slice(hp * LANES, (hp + 1) * LANES)) for n in range(nb) for hp in range(RW_W // LANES)]
    load = lambda ref: [_pad_chunk(ref[n, :, cs], c) for n, _, cs in chains]
    lws, rs, ks, vs, as_, bs = load(lw_ref), load(r_ref), load(k_ref), load(v_ref), load(a_ref), load(b_ref)
    cums = [_sel_l(tri, lw) for lw in lws]
    a2s = [stack2(a * jnp.exp(cum - lw)).astype(BF16) for a, cum, lw in zip(as_, cums, lws)]
    r2s = [stack2(r * jnp.exp(cum)).astype(BF16) for r, cum in zip(rs, cums)]
    b2s = [stack2(b * jnp.exp(-cum)).astype(BF16) for b, cum in zip(bs, cums)]
    k2s = [stack2(k * jnp.exp(-cum)).astype(BF16) for k, cum in zip(ks, cums)]
    v2s = [stack2(v) for v in vs]
    ars_in = [jnp.concatenate([a2, r2], axis=0) for a2, r2 in zip(a2s, r2s)]
    lms = [_mm_nt(ar, jnp.concatenate([b2, k2], axis=0)) for ar, b2, k2 in zip(ars_in, b2s, k2s)]
    xs = _inv_unit_lower([jnp.where(strict, lm[0:2 * c, 0:2 * c], 0.0) for lm in lms])
    e_ends = [jnp.exp(cum[c - 1:c, :] - cum) for cum in cums]
    xbs = [x.astype(BF16) for x in xs]
    qs = [_mm_tn(xb, stack2(b * e).astype(BF16)) for xb, b, e in zip(xbs, bs, e_ends)]
    wvs = [_mm(jnp.where(strict, lm[0:2 * c, 2 * c:4 * c], 0.0).astype(BF16), v2.astype(BF16))
           for lm, v2 in zip(lms, v2s)]
    ps = [_mm_tn(a2, q.astype(BF16)) for a2, q in zip(a2s, qs)]
    zs = [_mm_tn(jnp.concatenate([wv, v2], axis=0).astype(BF16),
                 jnp.concatenate([q, stack2(k * e)], axis=0).astype(BF16))
          for wv, v2, q, k, e in zip(wvs, v2s, qs, ks, e_ends)]
    sts = [s_ref[n, hp] for n, hp, _ in chains]
    sbs = [st.astype(BF16) for st in sts]
    for (n, hp, _), st, sb, cum, p, z in zip(chains, sts, sbs, cums, ps, zs):
        s_ref[n, hp] = st * jnp.exp(cum[c - 1:c, :]) + _mm(sb, p.astype(BF16)) + z
    arss = [_mm_nt(ar, sb) for ar, sb in zip(ars_in, sbs)]
    u2s = [_mm(xb, (ars[0:2 * c, :] + wv).astype(BF16)) for xb, ars, wv in zip(xbs, arss, wvs)]
    y2s = [ars[2 * c:4 * c, :]
           + _mm(jnp.concatenate([jnp.where(lower, lm[2 * c:4 * c, 0:2 * c], 0.0),
                                  jnp.where(lower, lm[2 * c:4 * c, 2 * c:4 * c], 0.0)], axis=1).astype(BF16),
                 jnp.concatenate([u2, v2], axis=0).astype(BF16))
           for ars, lm, u2, v2 in zip(arss, lms, u2s, v2s)]
    ys = [(y2[0:c, :] + y2[c:2 * c, :])[0:tb, :] for y2 in y2s]
    def head_sum(x):
        s0 = jnp.sum(jnp.where(m0, x, 0.0), axis=-1, keepdims=True)
        return jnp.where(m0, s0, jnp.sum(x, axis=-1, keepdims=True) - s0)

    mus = [head_sum(y) * (1.0 / DH) for y in ys]
    ds = [y - mu for y, mu in zip(ys, mus)]
    vars_ = [head_sum(d * d) * (1.0 / DH) for d in ds]
    bonus = [head_sum(r_ref[n, :, cs] * k_ref[n, :, cs] * rk_ref[:, cs]) for n, _, cs in chains]
    for (n, _, cs), d, var, bo in zip(chains, ds, vars_, bonus):
        yn = d * lax.rsqrt(var + RW_GN_EPS) * lnw_ref[:, cs] + lnb_ref[:, cs]
        o_ref[n, :, cs] = (yn + bo * v_ref[n, :, cs]) * g_ref[n, :, cs]
    so_ref[...] = s_ref[...]


def _rw_scan(parts, w, e, s0_blk, batch, seq):
    nb = 2
    assert batch % nb == 0
    tb = min(CHUNK, seq)
    nt = seq // tb
    npair = RW_W // LANES
    parts = [x.reshape(batch, seq, RW_W) for x in parts]
    tri = jnp.asarray(np.tril(np.ones((CHUNK, CHUNK), np.float32)), BF16)
    rowspec = pl.BlockSpec((nb, tb, RW_W), lambda bb, t: (bb, t, 0))
    stspec = pl.BlockSpec((nb, npair, LANES, LANES), lambda bb, t: (bb, 0, 0, 0))
    cst = lambda shape: pl.BlockSpec(shape, lambda bb, t: (0,) * len(shape))
    o_b, s_out = pl.pallas_call(
        functools.partial(_rw_scan_body, nb=nb, tb=tb),
        grid=(batch // nb, nt),
        in_specs=[rowspec] * 7 + [cst((1, RW_W))] * 3 + [stspec, cst(tri.shape)],
        out_specs=[rowspec, stspec],
        out_shape=[jax.ShapeDtypeStruct((batch, seq, RW_W), F32), jax.ShapeDtypeStruct(s0_blk.shape, F32)],
        scratch_shapes=[pltpu.VMEM((nb, npair, LANES, LANES), F32)],
        compiler_params=_cparams("arbitrary", "arbitrary"), name="rw_scan",
    )(*parts, w['rw_r_k'][e].reshape(1, RW_W), w['rw_ln_w'][e].reshape(1, RW_W),
      w['rw_ln_b'][e].reshape(1, RW_W), s0_blk, tri)
    return o_b.reshape(batch * seq, RW_W), s_out


def _rw_state_to_blocks(s):
    bsz = s.shape[0]
    s = s.reshape(bsz, H_A // 2, 2, DH, DH)
    z = jnp.zeros_like(s[:, :, 0])
    top = jnp.concatenate([s[:, :, 0], z], axis=-1)
    bot = jnp.concatenate([z, s[:, :, 1]], axis=-1)
    return jnp.concatenate([top, bot], axis=-2)


def _rw_blocks_to_state(sb):
    bsz = sb.shape[0]
    return jnp.stack([sb[:, :, :DH, :DH], sb[:, :, DH:, DH:]], axis=2).reshape(bsz, H_A, DH, DH)


def _gdn_body(q_ref, k_ref, v_ref, z_ref, ba_ref, qi_ref, ki_ref, vi_ref, cwq_ref, cwk_ref, cwv_ref,
              par_ref, nw_ref, s0_ref, tri_ref, o_ref, so_ref, qc_ref, kc_ref, vc_ref,
              s_ref, carry_ref, *, nb, tb):
    @pl.when(pl.program_id(1) == 0)
    def _():
        carry_ref[:, 0, 5:8, :] = qi_ref[...]
        carry_ref[:, 1, 5:8, :] = ki_ref[...]
        carry_ref[:, 2, 5:8, :] = vi_ref[...]
        s_ref[...] = s0_ref[...]

    def conv(x, w_ref, n, idx, out_ref):
        prev = carry_ref[n, idx]
        y = x * w_ref[3:4, :]
        for kk in range(1, 4):
            y = y + _shift_rows(x, kk, prev) * w_ref[3 - kk:4 - kk, :]
        tail = x[tb - 3:tb, :]
        carry_ref[n, idx, 5:8, :] = tail
        out_ref[n] = tail
        return y * _sigmoid(y)

    lane = lax.broadcasted_iota(jnp.int32, (1, LANES), 1)

    def column(x, idx):
        return jnp.sum(jnp.where(lane == idx, x, 0.0), axis=-1, keepdims=True)

    c = CHUNK
    r2i = lax.broadcasted_iota(jnp.int32, (2 * c, 2 * c), 0)
    c2i = lax.broadcasted_iota(jnp.int32, (2 * c, 2 * c), 1)
    same = (r2i // c) == (c2i // c)
    strict = same & (r2i > c2i)
    lower = same & (r2i >= c2i)
    tri = tri_ref[...]

    stacked = []
    for n in range(nb):
        q = conv(q_ref[n], cwq_ref, n, 0, qc_ref)
        k = conv(k_ref[n], cwk_ref, n, 1, kc_ref)
        v = conv(v_ref[n], cwv_ref, n, 2, vc_ref)
        ba = ba_ref[n]
        beta_all = _sigmoid(ba)
        z_in = ba + par_ref[1:2, :]
        g_all = -jnp.exp(par_ref[0:1, :]) * (jnp.maximum(z_in, 0.0) + jnp.log1p(jnp.exp(-jnp.abs(z_in))))
        for hp in range(H_C // 2):
            heads = []
            for hh in range(2):
                h = 2 * hp + hh
                sl = slice(h * DK, (h + 1) * DK)
                qh = q[:, sl]
                kh = k[:, sl]
                qn = qh * lax.rsqrt(jnp.sum(qh * qh, axis=-1, keepdims=True) + L2_EPS) * (DK ** -0.5)
                kn = kh * lax.rsqrt(jnp.sum(kh * kh, axis=-1, keepdims=True) + L2_EPS)
                beta = jnp.broadcast_to(column(beta_all, h), (tb, DK))
                g = jnp.broadcast_to(column(g_all, H_C + h), (tb, DK))
                heads.append((qn, kn, v[:, sl], beta, g))
            stacked.append([jnp.concatenate([_pad_chunk(heads[0][i], c), _pad_chunk(heads[1][i], c)], axis=0)
                            for i in range(5)])
    chains = [(n, hp) for n in range(nb) for hp in range(H_C // 2)]
    qn2s, kn2s, v2s, b2s, g2s = [[s[i] for s in stacked] for i in range(5)]
    gcs = [_sel_l(tri, g2) for g2 in g2s]
    grows = [jnp.transpose(gc) for gc in gcs]
    gammas = [jnp.where(lower, jnp.exp(jnp.where(lower, gc - grow, 0.0)), 0.0) for gc, grow in zip(gcs, grows)]
    kbs = [kn2 * b2 for kn2, b2 in zip(kn2s, b2s)]
    kkqks = [_mm_nt(jnp.concatenate([kb, qn2], axis=0).astype(BF16), kn2.astype(BF16))
             for kb, qn2, kn2 in zip(kbs, qn2s, kn2s)]
    tinvs = _inv_unit_lower([jnp.where(strict, -kkqk[0:2 * c, :] * gamma, 0.0) for kkqk, gamma in zip(kkqks, gammas)])
    egs = [jnp.exp(gc) for gc in gcs]
    uws = [_mm(tinv.astype(BF16), jnp.concatenate([kb * eg, v2 * b2], axis=1).astype(BF16))
           for tinv, kb, eg, v2, b2 in zip(tinvs, kbs, egs, v2s, b2s)]
    heads2 = [(ci, hh) for ci in range(len(chains)) for hh in range(2)]
    rows = lambda hh: slice(hh * c, (hh + 1) * c)
    glasts = [gcs[ci][hh * c + c - 1:hh * c + c, :] for ci, hh in heads2]
    pzs = [_mm_tn((kn2s[ci][rows(hh), :] * jnp.exp(gl - gcs[ci][rows(hh), :])).astype(BF16),
                  uws[ci][rows(hh), :].astype(BF16))
           for (ci, hh), gl in zip(heads2, glasts)]
    sts = [s_ref[chains[ci][0], 2 * chains[ci][1] + hh] for ci, hh in heads2]
    sbs = [st.astype(BF16) for st in sts]
    for (ci, hh), st, sb, gl, pz in zip(heads2, sts, sbs, glasts, pzs):
        s_ref[chains[ci][0], 2 * chains[ci][1] + hh] = (st * jnp.exp(gl[:, 0:1])
                                                        - _mm(pz[:, 0:DK].astype(BF16), sb) + pz[:, DK:2 * DK])
    wqs = [_mm(jnp.concatenate([uws[ci][rows(hh), 0:DK], (qn2s[ci] * egs[ci])[rows(hh), :]], axis=0).astype(BF16), sb)
           for (ci, hh), sb in zip(heads2, sbs)]
    o2s = []
    for ci in range(len(chains)):
        vnew = jnp.concatenate([uws[ci][rows(hh), DK:2 * DK] - wqs[2 * ci + hh][0:c, :] for hh in range(2)], axis=0)
        qs = jnp.concatenate([wqs[2 * ci + hh][c:2 * c, :] for hh in range(2)], axis=0)
        amat = jnp.where(lower, kkqks[ci][2 * c:4 * c, :] * gammas[ci], 0.0)
        o2s.append(qs + _mm(amat.astype(BF16), vnew.astype(BF16)))
    for ci, (n, hp) in enumerate(chains):
        for hh in range(2):
            sl = slice((2 * hp + hh) * DK, (2 * hp + hh + 1) * DK)
            oh = o2s[ci][hh * c:hh * c + tb, :]
            zz = z_ref[n, :, sl]
            on = oh * lax.rsqrt(jnp.mean(oh * oh, axis=-1, keepdims=True) + EPS) * nw_ref[...]
            o_ref[n, :, sl] = on * (zz * _sigmoid(zz))
    so_ref[...] = s_ref[...]


def _gdn(p_qkv, p_z, p_ba, conv_prev, s0, w, o_idx, batch, seq):
    nb = 2
    assert batch % nb == 0 and seq >= 3
    tb = min(CHUNK, seq)
    nt = seq // tb
    gk = H_C * DK
    cw = w['gdn_conv_w'][o_idx]
    par = jnp.concatenate([_pad_lanes(w['gdn_A_log'][o_idx], H_C), _pad_lanes(w['gdn_dt_bias'][o_idx], H_C),
                           jnp.zeros((6, LANES), F32)], axis=0)
    idx = np.arange(2 * CHUNK)
    tri = jnp.asarray((((idx[:, None] // CHUNK) == (idx[None, :] // CHUNK))
                       & (idx[:, None] >= idx[None, :])).astype(np.float32), BF16)
    p_qkv = p_qkv.reshape(batch, seq, 3 * gk)
    col = lambda part: pl.BlockSpec((nb, tb, gk), lambda b, t, part=part: (b, t, part))
    prev = lambda part: pl.BlockSpec((nb, 3, gk), lambda b, t, part=part: (b, 0, part))
    wspec = lambda part: pl.BlockSpec((4, gk), lambda b, t, part=part: (0, part))
    cst = lambda shape: pl.BlockSpec(shape, lambda b, t: (0,) * len(shape))
    stspec = pl.BlockSpec((nb, H_C, DK, DK), lambda b, t: (b, 0, 0, 0))
    outs = pl.pallas_call(
        functools.partial(_gdn_body, nb=nb, tb=tb),
        grid=(batch // nb, nt),
        in_specs=[col(0), col(1), col(2), col(0),
                  pl.BlockSpec((nb, tb, LANES), lambda b, t: (b, t, 0)),
                  prev(0), prev(1), prev(2), wspec(0), wspec(1), wspec(2),
                  cst((8, LANES)), cst((1, DK)), stspec, cst(tri.shape)],
        out_specs=[col(0), stspec, prev(0), prev(0), prev(0)],
        out_shape=[jax.ShapeDtypeStruct((batch, seq, gk), F32), jax.ShapeDtypeStruct(s0.shape, F32)]
                  + [jax.ShapeDtypeStruct((batch, 3, gk), F32)] * 3,
        scratch_shapes=[pltpu.VMEM((nb, H_C, DK, DK), F32), pltpu.VMEM((nb, 3, 8, gk), F32)],
        compiler_params=_cparams("arbitrary", "arbitrary"), name="gdn",
    )(p_qkv, p_qkv, p_qkv, p_z.reshape(batch, seq, gk), p_ba.reshape(batch, seq, LANES),
      conv_prev, conv_prev, conv_prev, cw, cw, cw, par, w['gdn_norm_w'][o_idx].reshape(1, DK), s0, tri)
    o_c, s_out, qc, kc, vc = outs
    return o_c.reshape(batch * seq, gk), jnp.concatenate([qc, kc, vc], axis=-1), s_out


def _run_trunk(x, fox_past, rw_state, rw_shift, gdn_state, gdn_conv, ffn_conv, page_table, w, wb):
    batch, seq, d = x.shape
    n = batch * seq
    depth = w['norm_mix'].shape[0]
    tm = 512 if n % 512 == 0 else n
    x = x.reshape(n, d)
    fk, fv, flf, rws, rwsh, gs, gcv, fcv = [], [], [], [], [], [], [], []
    v_first = None
    for layer in range(depth):
        if layer % 2 == 0:
            e = layer // 2
            p_fox, p_rw, p_fl = _norm_matmul(x, w['norm_mix'][layer], wb['ev_in'][e], (4 * FOX_W, 1792, LANES), tm)
            if fox_past is None:
                q_aug, k_aug, v_bf, k_out, lf = _fox_prep(p_fox, p_fl, w['fox_q_gain'][e], w['fox_k_gain'][e],
                                                          w['fox_b_f'][e], batch, seq)
                o_attn = _fox_attn(q_aug, k_aug, v_bf, batch, seq)
            else:
                o_attn, k_out, lf = _fox_decode(p_fox, p_fl, w['fox_q_gain'][e], w['fox_k_gain'][e], w['fox_b_f'][e],
                                                fox_past[0], fox_past[1], fox_past[2], page_table, e, batch, seq)
            parts, sh = _rw_prep(p_rw, rw_shift[e], w, e, v_first, batch, seq)
            if e == 0:
                v_first = parts[3]
            o_b, s_blk = _rw_scan(parts, w, e, _rw_state_to_blocks(rw_state[e]), batch, seq)
            x = _ev_out(o_attn, p_fox, o_b, wb['ev_out'][e], x, tm)
            fk.append(k_out.reshape(batch, seq, H_A, DH))
            fv.append(p_fox[:, 2 * FOX_W:3 * FOX_W].reshape(batch, seq, H_A, DH))
            flf.append(lf[:, :H_A].reshape(batch, seq, H_A))
            rws.append(_rw_blocks_to_state(s_blk))
            rwsh.append(sh)
        else:
            o_idx = layer // 2
            p_qkv, p_z, p_ba = _norm_matmul(x, w['norm_mix'][layer], wb['od_in'][o_idx],
                                            (3 * H_C * DK, H_C * DK, LANES), tm)
            o_c, cv, s_out = _gdn(p_qkv, p_z, p_ba, gdn_conv[o_idx], gdn_state[o_idx], w, o_idx, batch, seq)
            x = _od_out(o_c, wb['od_out'][o_idx], x, tm)
            gs.append(s_out)
            gcv.append(cv)
        x, buf = _ffn(x, w['norm_ffn'][layer], wb['ffn_up'][layer], w['ffn_conv_w'][layer],
                      w['ffn_conv_b'][layer], wb['ffn_down'][layer], ffn_conv[layer], batch, seq)
        fcv.append(buf)
    y = _final_norm(x, w['norm_out'], tm).reshape(batch, seq, d)
    return (y, jnp.stack(fk), jnp.stack(fv), jnp.stack(flf), jnp.stack(rws), jnp.stack(rwsh),
            jnp.stack(gs), jnp.stack(gcv), jnp.stack(fcv))


def _prep_weights(w):
    fox_cols = 4 * FOX_W + H_A
    ev = w['ev_w_in']
    pad = jnp.zeros(ev.shape[:2] + (LANES - H_A,), ev.dtype)
    ev_in = jnp.concatenate([ev[..., :4 * FOX_W], ev[..., fox_cols:], ev[..., 4 * FOX_W:fox_cols], pad], axis=-1)
    od = w['od_w_in']
    pad2 = jnp.zeros(od.shape[:2] + (LANES - 2 * H_C,), od.dtype)
    od_in = jnp.concatenate([od, pad2], axis=-1)
    return dict(ev_in=ev_in.astype(BF16), ev_out=w['ev_w_out'].astype(BF16), od_in=od_in.astype(BF16),
                od_out=w['od_w_out'].astype(BF16), ffn_up=w['ffn_w_up'].astype(BF16),
                ffn_down=w['ffn_w_down'].astype(BF16))


def kernel(x_prompt, x_sample, cache_fox_k, cache_fox_v, cache_fox_logf, state_rwkv, state_rwkv_shift, state_gdn, state_gdn_conv, state_ffn_conv, page_table, norm_mix, norm_ffn, norm_out, ev_w_in, ev_w_out, fox_b_f, fox_q_gain, fox_k_gain, rw_mu, rw_w0, rw_w2, rw_a0, rw_a2, rw_g2, rw_k_k, rw_k_a, rw_r_k, rw_ln_w, rw_ln_b, rw_v0, rw_v1, rw_v2, od_w_in, od_w_out, gdn_conv_w, gdn_A_log, gdn_dt_bias, gdn_norm_w, ffn_w_up, ffn_conv_w, ffn_conv_b, ffn_w_down):
    w = dict(norm_mix=norm_mix, norm_ffn=norm_ffn, norm_out=norm_out, ev_w_in=ev_w_in, ev_w_out=ev_w_out,
             fox_b_f=fox_b_f, fox_q_gain=fox_q_gain, fox_k_gain=fox_k_gain, rw_mu=rw_mu, rw_w0=rw_w0,
             rw_w2=rw_w2, rw_a0=rw_a0, rw_a2=rw_a2, rw_g2=rw_g2, rw_k_k=rw_k_k, rw_k_a=rw_k_a, rw_r_k=rw_r_k,
             rw_ln_w=rw_ln_w, rw_ln_b=rw_ln_b, rw_v0=rw_v0, rw_v1=rw_v1, rw_v2=rw_v2, od_w_in=od_w_in,
             od_w_out=od_w_out, gdn_conv_w=gdn_conv_w, gdn_A_log=gdn_A_log, gdn_dt_bias=gdn_dt_bias,
             gdn_norm_w=gdn_norm_w, ffn_w_up=ffn_w_up, ffn_conv_w=ffn_conv_w, ffn_conv_b=ffn_conv_b,
             ffn_w_down=ffn_w_down)
    wb = _prep_weights(w)
    bp = x_prompt.shape[0]
    n_even, n_odd, depth = ev_w_in.shape[0], od_w_in.shape[0], norm_mix.shape[0]
    rw_cols = rw_mu.shape[1]
    (y_p, fk_p, fv_p, flf_p, rw_p, rwsh_p, gdn_p, gcv_p, fcv_p) = _run_trunk(
        x_prompt, None,
        jnp.zeros((n_even, bp, H_A, DH, DH), F32), jnp.zeros((n_even, bp, rw_cols), F32),
        jnp.zeros((n_odd, bp, H_C, DK, DK), F32), jnp.zeros((n_odd, bp, 3, 3 * H_C * DK), F32),
        jnp.zeros((depth, bp, 2, ffn_w_up.shape[2]), F32), page_table, w, wb)
    (y_s, fk_s, fv_s, flf_s, rw_s, rwsh_s, gdn_s, gcv_s, fcv_s) = _run_trunk(
        x_sample, (cache_fox_k, cache_fox_v, cache_fox_logf), state_rwkv, state_rwkv_shift,
        state_gdn, state_gdn_conv, state_ffn_conv, page_table, w, wb)
    return (y_p, y_s, fk_p, fv_p, flf_p, fk_s, fv_s, flf_s, rw_p, rw_s, rwsh_p, rwsh_s,
            gdn_p, gdn_s, gcv_p, gcv_s, fcv_p, fcv_s)
```

```python
import functools
import math

import jax
import jax.numpy as jnp
import numpy as np
from jax import lax
from jax.experimental import pallas as pl
from jax.experimental.pallas import tpu as pltpu

F32 = jnp.float32
BF16 = jnp.bfloat16

EPS = 1e-6
RW_GN_EPS = 64e-5
L2_EPS = 1e-6
NEG_BIG = -1e30

H_A = 8
DH = 64
FOX_W = 512
RW_W = 512
H_C = 8
DK = 128
CHUNK = 64
LANES = 128
VMEM_LIMIT = 56 * 1024 * 1024


def _cparams(*sem):
    return pltpu.CompilerParams(dimension_semantics=sem, vmem_limit_bytes=VMEM_LIMIT)


def _const_spec(shape):
    nd = len(shape)
    return pl.BlockSpec(shape, lambda *_: (0,) * nd, pipeline_mode=pl.Buffered(1))


def _layer_spec(stack, layer):
    return pl.BlockSpec((None,) + stack.shape[1:], lambda *_: (layer, 0, 0), pipeline_mode=pl.Buffered(1))


def _mm(a, b):
    return jnp.dot(a, b, preferred_element_type=F32)


def _mm_nt(a, b):
    return lax.dot_general(a, b, (((1,), (1,)), ((), ())), preferred_element_type=F32)


def _mm_tn(a, b):
    return lax.dot_general(a, b, (((0,), (0,)), ((), ())), preferred_element_type=F32)


def _split3(x):
    hi = x.astype(BF16)
    r = x - hi.astype(F32)
    mid = r.astype(BF16)
    lo = (r - mid.astype(F32)).astype(BF16)
    return hi, mid, lo


def _sel_l(m01, x):
    hi, mid, lo = _split3(x)
    return _mm(m01, hi) + _mm(m01, mid) + _mm(m01, lo)


def _sel_r(x, m01):
    hi, mid, lo = _split3(x)
    return _mm(hi, m01) + _mm(mid, m01) + _mm(lo, m01)


def _sel_nt(m01, x):
    hi, mid, lo = _split3(x)
    return _mm_nt(m01, hi) + _mm_nt(m01, mid) + _mm_nt(m01, lo)


def _log_sigmoid(z):
    return jnp.minimum(z, 0.0) - jnp.log1p(jnp.exp(-jnp.abs(z)))


def _sigmoid(z):
    return 1.0 / (1.0 + jnp.exp(-z))


def _inv_unit_lower(ns):
    size = ns[0].shape[0]
    r = lax.broadcasted_iota(jnp.int32, (size, size), 0)
    c = lax.broadcasted_iota(jnp.int32, (size, size), 1)
    eye = jnp.where(r == c, 1.0, 0.0)
    ps = [eye + n for n in ns]
    nks = list(ns)
    for _ in range(int(math.log2(CHUNK)) - 1):
        nks = [_mm(nk.astype(BF16), nk.astype(BF16)) for nk in nks]
        ps = [p + _mm(p.astype(BF16), nk.astype(BF16)) for p, nk in zip(ps, nks)]
    return ps


def _head_scale(x2, fn):
    lane_h = lax.broadcasted_iota(jnp.int32, (1, x2.shape[1]), 1) // DH
    out = jnp.zeros_like(x2)
    for h in range(x2.shape[1] // DH):
        s = jnp.sum(x2[:, h * DH:(h + 1) * DH], axis=-1, keepdims=True)
        out = jnp.where(lane_h == h, fn(s), out)
    return out


def _rms_heads(x, gain):
    return x * _head_scale(x * x, lambda s: lax.rsqrt(s * (1.0 / DH) + EPS)) * gain


def _shift_rows(x, k, prev):
    rolled = pltpu.roll(x, k, 0)
    sub = 8
    row = lax.broadcasted_iota(jnp.int32, (sub, x.shape[1]), 0)
    head = rolled[0:sub, :]
    nprev = prev.shape[0]
    for j in range(k):
        head = jnp.where(row == j, prev[nprev - k + j:nprev - k + j + 1, :], head)
    return head if x.shape[0] == sub else jnp.concatenate([head, rolled[sub:, :]], axis=0)


def _nm_body(x_ref, g_ref, w_ref, *o_refs, splits):
    x = x_ref[...]
    xn = (x * lax.rsqrt(jnp.mean(x * x, axis=-1, keepdims=True) + EPS) * g_ref[...]).astype(BF16)
    off = 0
    for o_ref, n in zip(o_refs, splits):
        o_ref[...] = _mm(xn, w_ref[:, off:off + n])
        off += n


def _norm_matmul(x, g, w_stack, layer, splits, tm):
    n, d = x.shape
    ntot = w_stack.shape[2]
    assert sum(splits) == ntot and n % tm == 0
    return pl.pallas_call(
        functools.partial(_nm_body, splits=tuple(splits)),
        grid=(n // tm,),
        in_specs=[pl.BlockSpec((tm, d), lambda i: (i, 0)), _const_spec((1, d)), _layer_spec(w_stack, layer)],
        out_specs=[pl.BlockSpec((tm, s), lambda i: (i, 0)) for s in splits],
        out_shape=[jax.ShapeDtypeStruct((n, s), F32) for s in splits],
        compiler_params=_cparams("arbitrary"),
        name="norm_matmul",
    )(x, g.reshape(1, d), w_stack)


def _final_norm_body(x_ref, g_ref, o_ref):
    x = x_ref[...]
    o_ref[...] = x * lax.rsqrt(jnp.mean(x * x, axis=-1, keepdims=True) + EPS) * g_ref[...]


def _final_norm(x, g, tm):
    n, d = x.shape
    return pl.pallas_call(
        _final_norm_body, grid=(n // tm,),
        in_specs=[pl.BlockSpec((tm, d), lambda i: (i, 0)), _const_spec((1, d))],
        out_specs=pl.BlockSpec((tm, d), lambda i: (i, 0)),
        out_shape=jax.ShapeDtypeStruct((n, d), F32),
        compiler_params=_cparams("arbitrary"), name="final_norm",
    )(x, g.reshape(1, d))


def _evout_body(oa_ref, og_ref, ob_ref, w_ref, x_ref, o_ref):
    a = (oa_ref[...] * _sigmoid(og_ref[...])).astype(BF16)
    b = ob_ref[...].astype(BF16)
    o_ref[...] = x_ref[...] + _mm(a, w_ref[0:FOX_W, :]) + _mm(b, w_ref[FOX_W:FOX_W + RW_W, :])


def _ev_out(o_attn, p_fox, o_b, w_stack, layer, x, tm):
    n, d = x.shape
    return pl.pallas_call(
        _evout_body, grid=(n // tm,),
        in_specs=[pl.BlockSpec((tm, FOX_W), lambda i: (i, 0)),
                  pl.BlockSpec((tm, FOX_W), lambda i: (i, 3)),
                  pl.BlockSpec((tm, RW_W), lambda i: (i, 0)),
                  _layer_spec(w_stack, layer),
                  pl.BlockSpec((tm, d), lambda i: (i, 0))],
        out_specs=pl.BlockSpec((tm, d), lambda i: (i, 0)),
        out_shape=jax.ShapeDtypeStruct((n, d), F32),
        compiler_params=_cparams("arbitrary"), name="ev_out",
    )(o_attn, p_fox, o_b, w_stack, x)


def _odout_body(oc_ref, w_ref, x_ref, o_ref):
    o_ref[...] = x_ref[...] + _mm(oc_ref[...].astype(BF16), w_ref[...])


def _od_out(o_c, w_stack, layer, x, tm):
    n, d = x.shape
    k = o_c.shape[1]
    return pl.pallas_call(
        _odout_body, grid=(n // tm,),
        in_specs=[pl.BlockSpec((tm, k), lambda i: (i, 0)), _layer_spec(w_stack, layer),
                  pl.BlockSpec((tm, d), lambda i: (i, 0))],
        out_specs=pl.BlockSpec((tm, d), lambda i: (i, 0)),
        out_shape=jax.ShapeDtypeStruct((n, d), F32),
        compiler_params=_cparams("arbitrary"), name="od_out",
    )(o_c, w_stack, x)


def _ffn_cols(xn, wup_ref, cw_ref, cb_ref, wdn_ref, acc, prev_fn, tail_fn, f, cwb):
    for c in range(f // cwb):
        ys = []
        for half in (0, 1):
            lo = half * f + c * cwb
            h = _mm(xn, wup_ref[:, lo:lo + cwb])
            hm1, hm2 = prev_fn(h, lo)
            ys.append(hm2 * cw_ref[0:1, lo:lo + cwb] + hm1 * cw_ref[1:2, lo:lo + cwb]
                      + h * cw_ref[2:3, lo:lo + cwb] + cb_ref[:, lo:lo + cwb])
            tail_fn(h, lo)
        u, gt = ys
        act = (gt * _sigmoid(gt) * u).astype(BF16)
        acc = acc + _mm(act, wdn_ref[c * cwb:(c + 1) * cwb, :])
    return acc


def _ffn_norm(x_ref, g_ref):
    x = x_ref[...]
    return x, (x * lax.rsqrt(jnp.mean(x * x, axis=-1, keepdims=True) + EPS) * g_ref[...]).astype(BF16)


def _ffn_seq_body(x_ref, g_ref, wup_ref, cw_ref, cb_ref, wdn_ref, init_ref, o_ref, st_ref, carry_ref,
                  *, tb, f, cwb):
    @pl.when(pl.program_id(1) == 0)
    def _():
        carry_ref[0:2, :] = init_ref[0]

    x, xn = _ffn_norm(x_ref, g_ref)

    def prev_fn(h, lo):
        prev = carry_ref[0:2, lo:lo + cwb]
        return _shift_rows(h, 1, prev), _shift_rows(h, 2, prev)

    def tail_fn(h, lo):
        carry_ref[0:2, lo:lo + cwb] = h[tb - 2:tb, :]
        st_ref[0, :, lo:lo + cwb] = h[tb - 2:tb, :]

    o_ref[...] = _ffn_cols(xn, wup_ref, cw_ref, cb_ref, wdn_ref, x, prev_fn, tail_fn, f, cwb)


def _ffn_flat_body(x_ref, g_ref, wup_ref, cw_ref, cb_ref, wdn_ref, f1_ref, f2_ref, o_ref, st_ref,
                   *, rows, seq, f, cwb):
    x, xn = _ffn_norm(x_ref, g_ref)
    tmod = lax.broadcasted_iota(jnp.int32, (rows, cwb), 0) % seq

    def prev_fn(h, lo):
        hm1 = jnp.where(tmod == 0, f1_ref[:, lo:lo + cwb], pltpu.roll(h, 1, 0))
        hm2 = jnp.where(tmod < 2, f2_ref[:, lo:lo + cwb], pltpu.roll(h, 2, 0))
        return hm1, hm2

    def tail_fn(h, lo):
        st_ref[:, :, lo:lo + cwb] = h.reshape(rows // seq, seq, cwb)[:, seq - 2:seq, :]

    o_ref[...] = _ffn_cols(xn, wup_ref, cw_ref, cb_ref, wdn_ref, x, prev_fn, tail_fn, f, cwb)


def _ffn(x, g, wup_stack, conv_w, conv_b, wdn_stack, layer, conv_prev, batch, seq):
    n, d = x.shape
    f2 = wup_stack.shape[2]
    f = f2 // 2
    cwb = f
    weights = [_const_spec((1, d)), _layer_spec(wup_stack, layer), _const_spec((3, f2)), _const_spec((1, f2)),
               _layer_spec(wdn_stack, layer)]
    out_shape = [jax.ShapeDtypeStruct((n, d), F32), jax.ShapeDtypeStruct((batch, 2, f2), F32)]
    args = (x, g.reshape(1, d), wup_stack, conv_w, conv_b.reshape(1, f2), wdn_stack)
    if seq >= 256:
        tb = 512 if seq % 512 == 0 else 256
        nt = seq // tb
        return pl.pallas_call(
            functools.partial(_ffn_seq_body, tb=tb, f=f, cwb=cwb),
            grid=(batch, nt),
            in_specs=[pl.BlockSpec((tb, d), lambda b, t: (b * nt + t, 0))] + weights
                     + [pl.BlockSpec((1, 2, f2), lambda b, t: (b, 0, 0))],
            out_specs=[pl.BlockSpec((tb, d), lambda b, t: (b * nt + t, 0)),
                       pl.BlockSpec((1, 2, f2), lambda b, t: (b, 0, 0))],
            out_shape=out_shape,
            scratch_shapes=[pltpu.VMEM((8, f2), F32)],
            compiler_params=_cparams("arbitrary", "arbitrary"), name="ffn_seq",
        )(*args, conv_prev)
    zeros = jnp.zeros((batch, seq - 2, f2), F32)
    fill2 = jnp.concatenate([conv_prev, zeros], axis=1).reshape(n, f2)
    fill1 = jnp.concatenate([conv_prev[:, 1:2], zeros, zeros[:, :1]], axis=1).reshape(n, f2)
    return pl.pallas_call(
        functools.partial(_ffn_flat_body, rows=n, seq=seq, f=f, cwb=cwb),
        grid=(1,),
        in_specs=[pl.BlockSpec((n, d), lambda i: (0, 0))] + weights
                 + [pl.BlockSpec((n, f2), lambda i: (0, 0)), pl.BlockSpec((n, f2), lambda i: (0, 0))],
        out_specs=[pl.BlockSpec((n, d), lambda i: (0, 0)), pl.BlockSpec((batch, 2, f2), lambda i: (0, 0, 0))],
        out_shape=out_shape,
        compiler_params=_cparams("arbitrary"), name="ffn_flat",
    )(*args, fill1, fill2)


def _fox_consts():
    src = np.arange(FOX_W)
    place = np.zeros((FOX_W, H_A * LANES), np.float32)
    place[src, (src // DH) * LANES + src % DH] = 1.0
    hh = np.arange(H_A)
    eq = np.zeros((3, LANES, H_A * LANES), np.float32)
    ek = np.zeros((3, LANES, H_A * LANES), np.float32)
    cq = np.zeros((1, H_A * LANES), np.float32)
    ck = np.zeros((1, H_A * LANES), np.float32)
    for piece in range(3):
        eq[piece, hh, hh * LANES + DH + piece] = 1.0
        ek[piece, hh, hh * LANES + DH + 3 + piece] = -1.0
        cq[0, hh * LANES + DH + 3 + piece] = 1.0
        ck[0, hh * LANES + DH + piece] = 1.0
    as_bf = lambda a: jnp.asarray(a, BF16)
    return as_bf(place), as_bf(place.T), as_bf(eq), as_bf(ek), jnp.asarray(cq), jnp.asarray(ck)


def _fox_prep_body(q_ref, k_ref, v_ref, fl_ref, qg_ref, kg_ref, bf_ref, pm_ref, pt_ref, eq_ref, ek_ref, cq_ref,
                   ck_ref, vone_ref, tri_ref, bd_ref, qa_ref, ka_ref, vt_ref, ko_ref, lf_ref, carry_ref, *, tm):
    @pl.when(pl.program_id(1) == 0)
    def _():
        carry_ref[...] = jnp.zeros_like(carry_ref)

    bd = bd_ref[...]

    def rms_heads(x, gain):
        x2 = x * x
        hi = x2.astype(BF16)
        mid = (x2 - hi.astype(F32)).astype(BF16)
        ms = (_mm(hi, bd) + _mm(mid, bd)) * (1.0 / DH)
        return x * lax.rsqrt(ms + EPS) * gain

    qn = rms_heads(q_ref[...], qg_ref[...]) * (DH ** -0.5)
    kn = rms_heads(k_ref[...], kg_ref[...])
    lf = _log_sigmoid(fl_ref[...] + bf_ref[...])
    lf_ref[...] = lf
    fcum = _sel_l(tri_ref[...], lf) + carry_ref[0:1, :]
    carry_ref[0:1, :] = fcum[tm - 1:tm, :]
    fh, fm, flo = _split3(fcum)
    pm = pm_ref[...]
    qa = (_mm(qn.astype(BF16), pm) + _mm(fh, eq_ref[0]) + _mm(fm, eq_ref[1]) + _mm(flo, eq_ref[2])
          + cq_ref[...])
    ka = (_mm(kn.astype(BF16), pm) + _mm(fh, ek_ref[0]) + _mm(fm, ek_ref[1]) + _mm(flo, ek_ref[2])
          + ck_ref[...])
    qa_ref[...] = qa.astype(BF16)
    ka_ref[...] = ka.astype(BF16)
    vt = _mm_nt(pt_ref[...], v_ref[...].astype(BF16)) + vone_ref[...]
    vt_ref[0, :, 0] = vt.astype(BF16).reshape(H_A, LANES, tm)
    ko_ref[...] = kn


FOX_TILE = 512
FOX_QSUB = 4


def _fox_tile(seq):
    return min(FOX_TILE, seq // FOX_QSUB)


def _pad_lanes(v, offset=0):
    return jnp.concatenate([jnp.zeros((offset,), F32), v.astype(F32),
                            jnp.zeros((LANES - offset - v.shape[0],), F32)]).reshape(1, LANES)


def _fox_prep(p_fox, p_fl, q_gain, k_gain, b_f, batch, seq):
    n = p_fox.shape[0]
    tm = _fox_tile(seq)
    nt = seq // tm
    place, place_t, eq, ek, cq, ck = _fox_consts()
    vone = np.zeros((H_A * LANES, 1), np.float32)
    vone[np.arange(H_A) * LANES + DH, 0] = 1.0
    vone = jnp.asarray(vone)
    tri = jnp.asarray(np.tril(np.ones((tm, tm), np.float32)), BF16)
    lane = np.arange(FOX_W)
    bd = jnp.asarray((lane[:, None] // DH == lane[None, :] // DH).astype(np.float32), BF16)
    bf = _pad_lanes(b_f)
    row = lambda c: pl.BlockSpec((tm, FOX_W), lambda b, t, c=c: (b * nt + t, c))
    wide = pl.BlockSpec((tm, H_A * LANES), lambda b, t: (b * nt + t, 0))
    narrow = pl.BlockSpec((tm, LANES), lambda b, t: (b * nt + t, 0))
    return pl.pallas_call(
        functools.partial(_fox_prep_body, tm=tm),
        grid=(batch, nt),
        in_specs=[row(0), row(1), row(2), narrow,
                  _const_spec((1, FOX_W)), _const_spec((1, FOX_W)), _const_spec((1, LANES)),
                  _const_spec(place.shape), _const_spec(place_t.shape), _const_spec(eq.shape), _const_spec(ek.shape),
                  _const_spec(cq.shape), _const_spec(ck.shape), _const_spec(vone.shape), _const_spec(tri.shape),
                  _const_spec(bd.shape)],
        out_specs=[wide, wide, pl.BlockSpec((1, H_A, 1, LANES, tm), lambda b, t: (b, 0, t, 0, 0)), row(0), narrow],
        out_shape=[jax.ShapeDtypeStruct((n, H_A * LANES), BF16), jax.ShapeDtypeStruct((n, H_A * LANES), BF16),
                   jax.ShapeDtypeStruct((batch, H_A, nt, LANES, tm), BF16), jax.ShapeDtypeStruct((n, FOX_W), F32),
                   jax.ShapeDtypeStruct((n, LANES), F32)],
        scratch_shapes=[pltpu.VMEM((8, LANES), F32)],
        compiler_params=_cparams("arbitrary", "arbitrary"), name="fox_prep",
    )(p_fox, p_fox, p_fox, p_fl, jnp.tile(q_gain, H_A).reshape(1, FOX_W),
      jnp.tile(k_gain, H_A).reshape(1, FOX_W), bf, place, place_t, eq, ek, cq, ck, vone, tri, bd)


def _fox_attn_body(qa_ref, ka_ref, vt_ref, o_ref, *, tq):
    i = pl.program_id(2)
    key = lax.broadcasted_iota(jnp.int32, (tq, tq), 0)
    qry = lax.broadcasted_iota(jnp.int32, (tq, tq), 1)
    causal = key <= qry
    nsb = FOX_QSUB
    chains = [(sb, hh) for sb in range(nsb) for hh in range(2)]
    qs = [qa_ref[sb * tq:(sb + 1) * tq, hh * LANES:(hh + 1) * LANES] for sb, hh in chains]

    def block(j, carry, active):
        off = pl.multiple_of(j * tq, tq)
        ks = [ka_ref[pl.ds(off, tq), hh * LANES:(hh + 1) * LANES] for hh in range(2)]
        vts = [vt_ref[0, hh, j] for hh in range(2)]
        sts = {c: _mm_nt(ks[chains[c][1]], qs[c]) for c in active}
        sts = {c: jnp.where(causal, st, NEG_BIG) if active[c] else st for c, st in sts.items()}
        m2s = {c: jnp.maximum(carry[c][0], jnp.max(st, axis=0, keepdims=True)) for c, st in sts.items()}
        ps = {c: jnp.exp(st - m2s[c]).astype(BF16) for c, st in sts.items()}
        return tuple((m2s[c], jnp.exp(carry[c][0] - m2s[c]) * carry[c][1] + _mm(vts[chains[c][1]], ps[c]))
                     if c in active else carry[c] for c in range(len(chains)))

    init = tuple((jnp.full((1, tq), NEG_BIG, F32), jnp.zeros((LANES, tq), F32)) for _ in chains)
    carry = lax.fori_loop(0, nsb * i, lambda j, c: block(j, c, {c_: False for c_ in range(len(chains))}), init)
    for d in range(nsb):
        carry = block(nsb * i + d, carry, {c: chains[c][0] == d for c in range(len(chains)) if chains[c][0] >= d})
    for sb in range(nsb):
        halves = [acc[0:DH, :] / acc[DH:DH + 1, :] for _, acc in carry[2 * sb:2 * sb + 2]]
        o_ref[sb * tq:(sb + 1) * tq, :] = jnp.transpose(jnp.concatenate(halves, axis=0))


def _fox_attn(q_aug, k_aug, v_t, batch, seq):
    n = q_aug.shape[0]
    tq = _fox_tile(seq)
    rows = FOX_QSUB * tq
    nq = seq // rows
    return pl.pallas_call(
        functools.partial(_fox_attn_body, tq=tq),
        grid=(batch, H_A // 2, nq),
        in_specs=[pl.BlockSpec((rows, 2 * LANES), lambda b, hp, i: (b * nq + i, hp)),
                  pl.BlockSpec((seq, 2 * LANES), lambda b, hp, i: (b, hp)),
                  pl.BlockSpec((1, 2, seq // tq, LANES, tq), lambda b, hp, i: (b, hp, 0, 0, 0))],
        out_specs=pl.BlockSpec((rows, LANES), lambda b, hp, i: (b * nq + i, hp)),
        out_shape=jax.ShapeDtypeStruct((n, FOX_W), F32),
        compiler_params=_cparams("arbitrary", "arbitrary", "arbitrary"), name="fox_attn",
    )(q_aug, k_aug, v_t)


def _fox_decode_body(pt_ref, q_ref, k_ref, v_ref, fl_ref, qg_ref, kg_ref, bf_ref, ms_ref, ps_ref, pa_ref,
                     asel_ref, *rest, nb, pps, seq):
    del pt_ref
    npg = nb * pps
    kp, vp, lp = rest[:npg], rest[npg:2 * npg], rest[2 * npg:3 * npg]
    o_ref, ko_ref, lfo_ref = rest[3 * npg:3 * npg + 3]
    qh_ref, cb_ref, m_ref, l_ref, acc_ref, car_ref = rest[3 * npg + 3:]
    j = pl.program_id(1)
    nrow = H_A * seq

    @pl.when(j == 0)
    def _new_tokens():
        q_all = _rms_heads(q_ref[...], qg_ref[...]) * (DH ** -0.5)
        k_all = _rms_heads(k_ref[...], kg_ref[...])
        ko_ref[...] = k_all
        lf_all = _log_sigmoid(fl_ref[...] + bf_ref[...])
        lfo_ref[...] = lf_all
        row = lax.broadcasted_iota(jnp.int32, (seq, LANES), 0)
        zpad = jnp.zeros((LANES - seq, DH), F32)
        keyi = lax.broadcasted_iota(jnp.int32, (nrow, LANES), 1)
        ti = lax.broadcasted_iota(jnp.int32, (nrow, LANES), 0) % seq
        for n in range(nb):
            rs = slice(n * seq, (n + 1) * seq)
            q, k, v = q_all[rs, :], k_all[rs, :], v_ref[rs, :]
            c = lf_all[rs, :]
            s = 1
            while s < seq:
                c = c + jnp.where(row >= s, pltpu.roll(c, s, 0), 0.0)
                s *= 2
            srows = []
            for h in range(H_A):
                qh = q[:, h * DH:(h + 1) * DH]
                qh_ref[n, h] = qh
                cb_ref[n, h * seq:(h + 1) * seq, :] = jnp.broadcast_to(c[:, h:h + 1], (seq, LANES))
                kpad = jnp.concatenate([k[:, h * DH:(h + 1) * DH], zpad], axis=0).astype(BF16)
                srows.append(_mm_nt(qh.astype(BF16), kpad))
            cneg = jnp.concatenate([-c, jnp.zeros((LANES - seq, LANES), F32)], axis=0)
            s_new = jnp.concatenate(srows, axis=0) + cb_ref[n] + _sel_nt(asel_ref[0:nrow, :], cneg)
            s_new = jnp.where(keyi <= ti, s_new, NEG_BIG)
            m = jnp.max(s_new, axis=-1, keepdims=True)
            p = jnp.exp(s_new - m)
            m_ref[n] = m
            l_ref[n] = jnp.sum(p, axis=-1, keepdims=True)
            accs = []
            for h in range(H_A):
                vpad = jnp.concatenate([v[:, h * DH:(h + 1) * DH], zpad], axis=0).astype(BF16)
                accs.append(_mm(p[h * seq:(h + 1) * seq, :].astype(BF16), vpad))
            acc_ref[n] = jnp.concatenate(accs, axis=0)
        car_ref[...] = jnp.zeros_like(car_ref)

    seqs = range(nb)
    lfts = [jnp.concatenate([lp[n * pps + i][...] for i in range(pps)]
                            + [jnp.zeros((LANES - pps * H_A, LANES), F32)], axis=0) for n in seqs]
    tots = [jnp.broadcast_to(jnp.sum(lft, axis=1, keepdims=True), (LANES, LANES)) for lft in lfts]
    scores = [[jnp.concatenate([_mm(qh_ref[n, h].astype(BF16), kp[n * pps + i][h].astype(BF16))
                                for h in range(H_A)], axis=0) for i in range(pps)] for n in seqs]
    cars = [car_ref[n] for n in seqs]
    rfulls = [_sel_r(lft, ms_ref[...]) + _sel_l(ps_ref[...], tot) + car for lft, tot, car in zip(lfts, tots, cars)]
    for n in seqs:
        car_ref[n] = cars[n] + _sel_l(pa_ref[...], tots[n])
    biases = [_sel_l(asel_ref[...], rfull) for rfull in rfulls]
    s_alls = [jnp.concatenate([scores[n][i] + biases[n][i * nrow:(i + 1) * nrow, :] + cb_ref[n]
                               for i in range(pps)], axis=1) for n in seqs]
    m_olds = [m_ref[n] for n in seqs]
    m_news = [jnp.maximum(m_old, jnp.max(s_all, axis=-1, keepdims=True)) for m_old, s_all in zip(m_olds, s_alls)]
    probs = [jnp.exp(s_all - m_new) for s_all, m_new in zip(s_alls, m_news)]
    alphas = [jnp.exp(m_old - m_new) for m_old, m_new in zip(m_olds, m_news)]
    pvs = [jnp.concatenate(
        [_mm_nt(probs[n][h * seq:(h + 1) * seq, :].astype(BF16),
                jnp.concatenate([vp[n * pps + i][h] for i in range(pps)], axis=1).astype(BF16))
         for h in range(H_A)], axis=0) for n in seqs]
    for n in seqs:
        m_ref[n] = m_news[n]
        l_ref[n] = alphas[n] * l_ref[n] + jnp.sum(probs[n], axis=-1, keepdims=True)
        acc_ref[n] = alphas[n] * acc_ref[n] + pvs[n]

    @pl.when(j == pl.num_programs(1) - 1)
    def _():
        for n in seqs:
            o = acc_ref[n] / l_ref[n]
            for h in range(H_A):
                o_ref[n * seq:(n + 1) * seq, h * DH:(h + 1) * DH] = o[h * seq:(h + 1) * seq, :]


def _fox_decode(p_fox, p_fl, q_gain, k_gain, b_f, pool_k, pool_v, pool_lf, page_table, e, batch, seq):
    n = p_fox.shape[0]
    n_pages = page_table.shape[1]
    page = pool_lf.shape[2]
    nb = 2
    assert page == LANES and seq == 8 and batch % nb == 0
    pps = 8
    while n_pages % pps:
        pps //= 2
    groups = n_pages // pps
    pk = jnp.transpose(pool_k, (0, 1, 3, 4, 2))
    pv = jnp.transpose(pool_v, (0, 1, 3, 4, 2))
    plf = jnp.transpose(pool_lf, (0, 1, 3, 2))
    nrow = H_A * seq
    idx = np.arange(LANES)
    as_bf = lambda a: jnp.asarray(a.astype(np.float32), BF16)
    ms = as_bf(idx[:, None] > idx[None, :])
    same_h = (idx[:, None] % H_A) == (idx[None, :] % H_A)
    valid = (idx[:, None] < pps * H_A) & (idx[None, :] < pps * H_A)
    ps = as_bf(same_h & valid & (idx[None, :] // H_A < idx[:, None] // H_A))
    pa = as_bf(same_h & valid)
    r = np.arange(pps * nrow)
    asel = np.zeros((pps * nrow, LANES), np.float32)
    asel[r, (r // nrow) * H_A + (r % nrow) // seq] = 1.0
    asel = as_bf(asel)
    bf = _pad_lanes(b_f)

    def page_spec(s, i, shape):
        def index(b, j, pt):
            return (e, pt[b * nb + s, n_pages - 1 - (j * pps + i)]) + (0,) * len(shape)
        return pl.BlockSpec((None, None) + shape, index)

    pages = lambda shape: [page_spec(s, i, shape) for s in range(nb) for i in range(pps)]
    rowspec = lambda c: pl.BlockSpec((nb * seq, FOX_W), lambda b, j, pt, c=c: (b, c))
    narrow = pl.BlockSpec((nb * seq, LANES), lambda b, j, pt: (b, 0))
    cst = lambda shape: pl.BlockSpec(shape, lambda b, j, pt: (0,) * len(shape))
    in_specs = ([rowspec(0), rowspec(1), rowspec(2), narrow,
                 cst((1, FOX_W)), cst((1, FOX_W)), cst((1, LANES)), cst(ms.shape), cst(ps.shape), cst(pa.shape),
                 cst(asel.shape)]
                + pages((H_A, DH, page)) + pages((H_A, DH, page)) + pages((H_A, page)))
    grid_spec = pltpu.PrefetchScalarGridSpec(
        num_scalar_prefetch=1, grid=(batch // nb, groups), in_specs=in_specs,
        out_specs=[rowspec(0), rowspec(0), narrow],
        scratch_shapes=[pltpu.VMEM((nb, H_A, seq, DH), F32), pltpu.VMEM((nb, nrow, LANES), F32),
                        pltpu.VMEM((nb, nrow, 1), F32), pltpu.VMEM((nb, nrow, 1), F32),
                        pltpu.VMEM((nb, nrow, DH), F32), pltpu.VMEM((nb, LANES, LANES), F32)])
    return pl.pallas_call(
        functools.partial(_fox_decode_body, nb=nb, pps=pps, seq=seq),
        grid_spec=grid_spec,
        out_shape=[jax.ShapeDtypeStruct((n, FOX_W), F32), jax.ShapeDtypeStruct((n, FOX_W), F32),
                   jax.ShapeDtypeStruct((n, LANES), F32)],
        compiler_params=_cparams("arbitrary", "arbitrary"), name="fox_decode",
    )(page_table, p_fox, p_fox, p_fox, p_fl, jnp.tile(q_gain, H_A).reshape(1, FOX_W),
      jnp.tile(k_gain, H_A).reshape(1, FOX_W), bf, ms, ps, pa, asel,
      *([pk] * (nb * pps)), *([pv] * (nb * pps)), *([plf] * (nb * pps)))


def _rw_prep_body(p_ref, init_ref, mu_ref, w0_ref, w2_ref, a0_ref, a2_ref, g2_ref, kk_ref, ka_ref, *rest,
                  tb, first):
    if first:
        r_o, lw_o, k_o, v_o, a_o, b_o, g_o, sh_o, carry_ref = rest
    else:
        v0_ref, v1_ref, v2_ref, vf_ref, r_o, lw_o, k_o, v_o, a_o, b_o, g_o, sh_o, carry_ref = rest

    @pl.when(pl.program_id(1) == 0)
    def _():
        carry_ref[7:8, :] = init_ref[0]

    p = p_ref[...]
    p_prev = _shift_rows(p, 1, carry_ref[...])
    carry_ref[7:8, :] = p[tb - 1:tb, :]
    sh_o[0] = p[tb - 1:tb, :]
    ps = p + (p_prev - p) * mu_ref[...]
    r = ps[:, 0:RW_W]
    k = ps[:, RW_W:2 * RW_W]
    v = ps[:, 2 * RW_W:3 * RW_W]
    x128 = ps[:, 3 * RW_W:3 * RW_W + LANES]
    gd = ps[:, 3 * RW_W + LANES:3 * RW_W + 2 * LANES]
    lane = lax.broadcasted_iota(jnp.int32, (1, LANES), 1)
    xw = jnp.where(lane < DH, jnp.tanh(x128), 0.0).astype(BF16)
    xa = jnp.where(lane < DH, 0.0, x128).astype(BF16)
    w_log = _log_sigmoid(w0_ref[...] + _mm(xw, w2_ref[...])) - 0.5
    lw_o[...] = -jnp.exp(w_log)
    a = _sigmoid(a0_ref[...] + _mm(xa, a2_ref[...]))
    g_o[...] = _mm(_sigmoid(gd).astype(BF16), g2_ref[...])
    if not first:
        gate = _sigmoid(v0_ref[...] + _mm(_mm(v.astype(BF16), v1_ref[...]).astype(BF16), v2_ref[...]))
        v = v + (vf_ref[...] - v) * gate
    kkx = k * kk_ref[...]
    kk = kkx * _head_scale(kkx * kkx, lambda s: lax.rsqrt(s + L2_EPS))
    r_o[...] = r
    k_o[...] = k * (1.0 + (a - 1.0) * ka_ref[...])
    v_o[...] = v
    a_o[...] = -kk
    b_o[...] = kk * a


def _rw_prep(p_rw, shift_prev, w, e, v_first, batch, seq):
    n, cols = p_rw.shape
    tb = min(256, seq)
    nt = seq // tb
    first = e == 0
    pad_rows = lambda m: jnp.concatenate([m, jnp.zeros((LANES - m.shape[0], m.shape[1]), m.dtype)], axis=0)
    w2p = pad_rows(w['rw_w2'][e]).astype(BF16)
    a2p = jnp.concatenate([jnp.zeros((DH, RW_W), F32), w['rw_a2'][e]], axis=0).astype(BF16)
    vec = lambda x: x.reshape(1, -1)
    args = [p_rw, shift_prev.reshape(batch, 1, cols), vec(w['rw_mu'][e]), vec(w['rw_w0'][e]), w2p,
            vec(w['rw_a0'][e]), a2p, w['rw_g2'][e].astype(BF16), vec(w['rw_k_k'][e]), vec(w['rw_k_a'][e])]
    rowspec = pl.BlockSpec((tb, RW_W), lambda b, t: (b * nt + t, 0))
    in_specs = [pl.BlockSpec((tb, cols), lambda b, t: (b * nt + t, 0)),
                pl.BlockSpec((1, 1, cols), lambda b, t: (b, 0, 0)),
                _const_spec((1, cols)), _const_spec((1, RW_W)), _const_spec((LANES, RW_W)),
                _const_spec((1, RW_W)), _const_spec((LANES, RW_W)), _const_spec((LANES, RW_W)),
                _const_spec((1, RW_W)), _const_spec((1, RW_W))]
    if not first:
        v1p = jnp.concatenate([w['rw_v1'][e - 1], jnp.zeros((RW_W, LANES - w['rw_v1'].shape[2]), F32)], axis=1)
        args += [vec(w['rw_v0'][e - 1]), v1p.astype(BF16), pad_rows(w['rw_v2'][e - 1]).astype(BF16), v_first]
        in_specs += [_const_spec((1, RW_W)), _const_spec((RW_W, LANES)), _const_spec((LANES, RW_W)), rowspec]
    outs = pl.pallas_call(
        functools.partial(_rw_prep_body, tb=tb, first=first),
        grid=(batch, nt), in_specs=in_specs,
        out_specs=[rowspec] * 7 + [pl.BlockSpec((1, 1, cols), lambda b, t: (b, 0, 0))],
        out_shape=[jax.ShapeDtypeStruct((n, RW_W), F32)] * 7 + [jax.ShapeDtypeStruct((batch, 1, cols), F32)],
        scratch_shapes=[pltpu.VMEM((8, cols), F32)],
        compiler_params=_cparams("arbitrary", "arbitrary"), name="rw_prep",
    )(*args)
    return outs[:7], outs[7].reshape(batch, cols)


def _pad_chunk(x, rows):
    if x.shape[0] == rows:
        return x
    return jnp.concatenate([x, jnp.zeros((rows - x.shape[0], x.shape[1]), x.dtype)], axis=0)


def _rw_scan_body(r_ref, lw_ref, k_ref, v_ref, a_ref, b_ref, g_ref, rk_ref, lnw_ref, lnb_ref, s0_ref, tri_ref,
                  o_ref, so_ref, s_ref, *, nb, tb):
    @pl.when(pl.program_id(1) == 0)
    def _():
        s_ref[...] = s0_ref[...]

    c = CHUNK
    lane = lax.broadcasted_iota(jnp.int32, (1, LANES), 1)
    m0 = lane < DH
    r2i = lax.broadcasted_iota(jnp.int32, (2 * c, 2 * c), 0)
    c2i = lax.broadcasted_iota(jnp.int32, (2 * c, 2 * c), 1)
    strict = (r2i % c) > (c2i % c)
    lower = (r2i % c) >= (c2i % c)
    tri = tri_ref[...]

    def stack2(z):
        return jnp.concatenate([jnp.where(m0, z, 0.0), jnp.where(m0, 0.0, z)], axis=0)

    chains = [(n, hp, slice(hp * LANES, (hp + 1) * LANES)) for n in range(nb) for hp in range(RW_W // LANES)]
    load = lambda ref: [_pad_chunk(ref[n, :, cs], c) for n, _, cs in chains]
    lws, rs, ks, vs, as_, bs = load(lw_ref), load(r_ref), load(k_ref), load(v_ref), load(a_ref), load(b_ref)
    cums = [_sel_l(tri, lw) for lw in lws]
    a2s = [stack2(a * jnp.exp(cum - lw)).astype(BF16) for a, cum, lw in zip(as_, cums, lws)]
    r2s = [stack2(r * jnp.exp(cum)).astype(BF16) for r, cum in zip(rs, cums)]
    b2s = [stack2(b * jnp.exp(-cum)).astype(BF16) for b, cum in zip(bs, cums)]
    k2s = [stack2(k * jnp.exp(-cum)).astype(BF16) for k, cum in zip(ks, cums)]
    v2s = [stack2(v) for v in vs]
    ars_in = [jnp.concatenate([a2, r2], axis=0) for a2, r2 in zip(a2s, r2s)]
    lms = [_mm_nt(ar, jnp.concatenate([b2, k2], axis=0)) for ar, b2, k2 in zip(ars_in, b2s, k2s)]
    xs = _inv_unit_lower([jnp.where(strict, lm[0:2 * c, 0:2 * c], 0.0) for lm in lms])
    e_ends = [jnp.exp(cum[c - 1:c, :] - cum) for cum in cums]
    xbs = [x.astype(BF16) for x in xs]
    qs = [_mm_tn(xb, stack2(b * e).astype(BF16)) for xb, b, e in zip(xbs, bs, e_ends)]
    wvs = [_mm(jnp.where(strict, lm[0:2 * c, 2 * c:4 * c], 0.0).astype(BF16), v2.astype(BF16))
           for lm, v2 in zip(lms, v2s)]
    ps = [_mm_tn(a2, q.astype(BF16)) for a2, q in zip(a2s, qs)]
    zs = [_mm_tn(jnp.concatenate([wv, v2], axis=0).astype(BF16),
                 jnp.concatenate([q, stack2(k * e)], axis=0).astype(BF16))
          for wv, v2, q, k, e in zip(wvs, v2s, qs, ks, e_ends)]
    sts = [s_ref[n, hp] for n, hp, _ in chains]
    sbs = [st.astype(BF16) for st in sts]
    for (n, hp, _), st, sb, cum, p, z in zip(chains, sts, sbs, cums, ps, zs):
        s_ref[n, hp] = st * jnp.exp(cum[c - 1:c, :]) + _mm(sb, p.astype(BF16)) + z
    arss = [_mm_nt(ar, sb) for ar, sb in zip(ars_in, sbs)]
    u2s = [_mm(xb, (ars[0:2 * c, :] + wv).astype(BF16)) for xb, ars, wv in zip(xbs, arss, wvs)]
    y2s = [ars[2 * c:4 * c, :]
           + _mm(jnp.concatenate([jnp.where(lower, lm[2 * c:4 * c, 0:2 * c], 0.0),
                                  jnp.where(lower, lm[2 * c:4 * c, 2 * c:4 * c], 0.0)], axis=1).astype(BF16),
                 jnp.concatenate([u2, v2], axis=0).astype(BF16))
           for ars, lm, u2, v2 in zip(arss, lms, u2s, v2s)]
    ys = [(y2[0:c, :] + y2[c:2 * c, :])[0:tb, :] for y2 in y2s]
    def head_sum(x):
        s0 = jnp.sum(jnp.where(m0, x, 0.0), axis=-1, keepdims=True)
        return jnp.where(m0, s0, jnp.sum(x, axis=-1, keepdims=True) - s0)

    mus = [head_sum(y) * (1.0 / DH) for y in ys]
    ds = [y - mu for y, mu in zip(ys, mus)]
    vars_ = [head_sum(d * d) * (1.0 / DH) for d in ds]
    bonus = [head_sum(r_ref[n, :, cs] * k_ref[n, :, cs] * rk_ref[:, cs]) for n, _, cs in chains]
    for (n, _, cs), d, var, bo in zip(chains, ds, vars_, bonus):
        yn = d * lax.rsqrt(var + RW_GN_EPS) * lnw_ref[:, cs] + lnb_ref[:, cs]
        o_ref[n, :, cs] = (yn + bo * v_ref[n, :, cs]) * g_ref[n, :, cs]
    so_ref[...] = s_ref[...]


def _rw_scan(parts, w, e, s0_blk, batch, seq):
    nb = 2
    assert batch % nb == 0
    tb = min(CHUNK, seq)
    nt = seq // tb
    npair = RW_W // LANES
    parts = [x.reshape(batch, seq, RW_W) for x in parts]
    tri = jnp.asarray(np.tril(np.ones((CHUNK, CHUNK), np.float32)), BF16)
    rowspec = pl.BlockSpec((nb, tb, RW_W), lambda bb, t: (bb, t, 0))
    stspec = pl.BlockSpec((nb, npair, LANES, LANES), lambda bb, t: (bb, 0, 0, 0))
    cst = lambda shape: pl.BlockSpec(shape, lambda bb, t: (0,) * len(shape))
    o_b, s_out = pl.pallas_call(
        functools.partial(_rw_scan_body, nb=nb, tb=tb),
        grid=(batch // nb, nt),
        in_specs=[rowspec] * 7 + [cst((1, RW_W))] * 3 + [stspec, cst(tri.shape)],
        out_specs=[rowspec, stspec],
        out_shape=[jax.ShapeDtypeStruct((batch, seq, RW_W), F32), jax.ShapeDtypeStruct(s0_blk.shape, F32)],
        scratch_shapes=[pltpu.VMEM((nb, npair, LANES, LANES), F32)],
        compiler_params=_cparams("arbitrary", "arbitrary"), name="rw_scan",
    )(*parts, w['rw_r_k'][e].reshape(1, RW_W), w['rw_ln_w'][e].reshape(1, RW_W),
      w['rw_ln_b'][e].reshape(1, RW_W), s0_blk, tri)
    return o_b.reshape(batch * seq, RW_W), s_out


def _rw_state_to_blocks(s):
    bsz = s.shape[0]
    s = s.reshape(bsz, H_A // 2, 2, DH, DH)
    z = jnp.zeros_like(s[:, :, 0])
    top = jnp.concatenate([s[:, :, 0], z], axis=-1)
    bot = jnp.concatenate([z, s[:, :, 1]], axis=-1)
    return jnp.concatenate([top, bot], axis=-2)


def _rw_blocks_to_state(sb):
    bsz = sb.shape[0]
    return jnp.stack([sb[:, :, :DH, :DH], sb[:, :, DH:, DH:]], axis=2).reshape(bsz, H_A, DH, DH)


def _gdn_body(q_ref, k_ref, v_ref, z_ref, ba_ref, qi_ref, ki_ref, vi_ref, cwq_ref, cwk_ref, cwv_ref,
              par_ref, nw_ref, s0_ref, tri_ref, o_ref, so_ref, qc_ref, kc_ref, vc_ref,
              s_ref, carry_ref, *, nb, tb):
    @pl.when(pl.program_id(1) == 0)
    def _():
        carry_ref[:, 0, 5:8, :] = qi_ref[...]
        carry_ref[:, 1, 5:8, :] = ki_ref[...]
        carry_ref[:, 2, 5:8, :] = vi_ref[...]
        s_ref[...] = s0_ref[...]

    def conv(x, w_ref, n, idx, out_ref):
        prev = carry_ref[n, idx]
        y = x * w_ref[3:4, :]
        for kk in range(1, 4):
            y = y + _shift_rows(x, kk, prev) * w_ref[3 - kk:4 - kk, :]
        tail = x[tb - 3:tb, :]
        carry_ref[n, idx, 5:8, :] = tail
        out_ref[n] = tail
        return y * _sigmoid(y)

    lane = lax.broadcasted_iota(jnp.int32, (1, LANES), 1)

    def column(x, idx):
        return jnp.sum(jnp.where(lane == idx, x, 0.0), axis=-1, keepdims=True)

    c = CHUNK
    r2i = lax.broadcasted_iota(jnp.int32, (2 * c, 2 * c), 0)
    c2i = lax.broadcasted_iota(jnp.int32, (2 * c, 2 * c), 1)
    same = (r2i // c) == (c2i // c)
    strict = same & (r2i > c2i)
    lower = same & (r2i >= c2i)
    tri = tri_ref[...]

    stacked = []
    for n in range(nb):
        q = conv(q_ref[n], cwq_ref, n, 0, qc_ref)
        k = conv(k_ref[n], cwk_ref, n, 1, kc_ref)
        v = conv(v_ref[n], cwv_ref, n, 2, vc_ref)
        ba = ba_ref[n]
        beta_all = _sigmoid(ba)
        z_in = ba + par_ref[1:2, :]
        g_all = -jnp.exp(par_ref[0:1, :]) * (jnp.maximum(z_in, 0.0) + jnp.log1p(jnp.exp(-jnp.abs(z_in))))
        for hp in range(H_C // 2):
            heads = []
            for hh in range(2):
                h = 2 * hp + hh
                sl = slice(h * DK, (h + 1) * DK)
                qh = q[:, sl]
                kh = k[:, sl]
                qn = qh * lax.rsqrt(jnp.sum(qh * qh, axis=-1, keepdims=True) + L2_EPS) * (DK ** -0.5)
                kn = kh * lax.rsqrt(jnp.sum(kh * kh, axis=-1, keepdims=True) + L2_EPS)
                beta = jnp.broadcast_to(column(beta_all, h), (tb, DK))
                g = jnp.broadcast_to(column(g_all, H_C + h), (tb, DK))
                heads.append((qn, kn, v[:, sl], beta, g))
            stacked.append([jnp.concatenate([_pad_chunk(heads[0][i], c), _pad_chunk(heads[1][i], c)], axis=0)
                            for i in range(5)])
    chains = [(n, hp) for n in range(nb) for hp in range(H_C // 2)]
    qn2s, kn2s, v2s, b2s, g2s = [[s[i] for s in stacked] for i in range(5)]
    gcs = [_sel_l(tri, g2) for g2 in g2s]
    grows = [jnp.transpose(gc) for gc in gcs]
    gammas = [jnp.where(lower, jnp.exp(jnp.where(lower, gc - grow, 0.0)), 0.0) for gc, grow in zip(gcs, grows)]
    kbs = [kn2 * b2 for kn2, b2 in zip(kn2s, b2s)]
    kkqks = [_mm_nt(jnp.concatenate([kb, qn2], axis=0).astype(BF16), kn2.astype(BF16))
             for kb, qn2, kn2 in zip(kbs, qn2s, kn2s)]
    tinvs = _inv_unit_lower([jnp.where(strict, -kkqk[0:2 * c, :] * gamma, 0.0) for kkqk, gamma in zip(kkqks, gammas)])
    egs = [jnp.exp(gc) for gc in gcs]
    uws = [_mm(tinv.astype(BF16), jnp.concatenate([kb * eg, v2 * b2], axis=1).astype(BF16))
           for tinv, kb, eg, v2, b2 in zip(tinvs, kbs, egs, v2s, b2s)]
    heads2 = [(ci, hh) for ci in range(len(chains)) for hh in range(2)]
    rows = lambda hh: slice(hh * c, (hh + 1) * c)
    glasts = [gcs[ci][hh * c + c - 1:hh * c + c, :] for ci, hh in heads2]
    pzs = [_mm_tn((kn2s[ci][rows(hh), :] * jnp.exp(gl - gcs[ci][rows(hh), :])).astype(BF16),
                  uws[ci][rows(hh), :].astype(BF16))
           for (ci, hh), gl in zip(heads2, glasts)]
    sts = [s_ref[chains[ci][0], 2 * chains[ci][1] + hh] for ci, hh in heads2]
    sbs = [st.astype(BF16) for st in sts]
    for (ci, hh), st, sb, gl, pz in zip(heads2, sts, sbs, glasts, pzs):
        s_ref[chains[ci][0], 2 * chains[ci][1] + hh] = (st * jnp.exp(gl[:, 0:1])
                                                        - _mm(pz[:, 0:DK].astype(BF16), sb) + pz[:, DK:2 * DK])
    wqs = [_mm(jnp.concatenate([uws[ci][rows(hh), 0:DK], (qn2s[ci] * egs[ci])[rows(hh), :]], axis=0).astype(BF16), sb)
           for (ci, hh), sb in zip(heads2, sbs)]
    o2s = []
    for ci in range(len(chains)):
        vnew = jnp.concatenate([uws[ci][rows(hh), DK:2 * DK] - wqs[2 * ci + hh][0:c, :] for hh in range(2)], axis=0)
        qs = jnp.concatenate([wqs[2 * ci + hh][c:2 * c, :] for hh in range(2)], axis=0)
        amat = jnp.where(lower, kkqks[ci][2 * c:4 * c, :] * gammas[ci], 0.0)
        o2s.append(qs + _mm(amat.astype(BF16), vnew.astype(BF16)))
    for ci, (n, hp) in enumerate(chains):
        for hh in range(2):
            sl = slice((2 * hp + hh) * DK, (2 * hp + hh + 1) * DK)
            oh = o2s[ci][hh * c:hh * c + tb, :]
            zz = z_ref[n, :, sl]
            on = oh * lax.rsqrt(jnp.mean(oh * oh, axis=-1, keepdims=True) + EPS) * nw_ref[...]
            o_ref[n, :, sl] = on * (zz * _sigmoid(zz))
    so_ref[...] = s_ref[...]


def _gdn(p_qkv, p_z, p_ba, conv_prev, s0, w, o_idx, batch, seq):
    nb = 2
    assert batch % nb == 0 and seq >= 3
    tb = min(CHUNK, seq)
    nt = seq // tb
    gk = H_C * DK
    cw = w['gdn_conv_w'][o_idx]
    par = jnp.concatenate([_pad_lanes(w['gdn_A_log'][o_idx], H_C), _pad_lanes(w['gdn_dt_bias'][o_idx], H_C),
                           jnp.zeros((6, LANES), F32)], axis=0)
    idx = np.arange(2 * CHUNK)
    tri = jnp.asarray((((idx[:, None] // CHUNK) == (idx[None, :] // CHUNK))
                       & (idx[:, None] >= idx[None, :])).astype(np.float32), BF16)
    p_qkv = p_qkv.reshape(batch, seq, 3 * gk)
    col = lambda part: pl.BlockSpec((nb, tb, gk), lambda b, t, part=part: (b, t, part))
    prev = lambda part: pl.BlockSpec((nb, 3, gk), lambda b, t, part=part: (b, 0, part))
    wspec = lambda part: pl.BlockSpec((4, gk), lambda b, t, part=part: (0, part))
    cst = lambda shape: pl.BlockSpec(shape, lambda b, t: (0,) * len(shape))
    stspec = pl.BlockSpec((nb, H_C, DK, DK), lambda b, t: (b, 0, 0, 0))
    outs = pl.pallas_call(
        functools.partial(_gdn_body, nb=nb, tb=tb),
        grid=(batch // nb, nt),
        in_specs=[col(0), col(1), col(2), col(0),
                  pl.BlockSpec((nb, tb, LANES), lambda b, t: (b, t, 0)),
                  prev(0), prev(1), prev(2), wspec(0), wspec(1), wspec(2),
                  cst((8, LANES)), cst((1, DK)), stspec, cst(tri.shape)],
        out_specs=[col(0), stspec, prev(0), prev(0), prev(0)],
        out_shape=[jax.ShapeDtypeStruct((batch, seq, gk), F32), jax.ShapeDtypeStruct(s0.shape, F32)]
                  + [jax.ShapeDtypeStruct((batch, 3, gk), F32)] * 3,
        scratch_shapes=[pltpu.VMEM((nb, H_C, DK, DK), F32), pltpu.VMEM((nb, 3, 8, gk), F32)],
        compiler_params=_cparams("arbitrary", "arbitrary"), name="gdn",
    )(p_qkv, p_qkv, p_qkv, p_z.reshape(batch, seq, gk), p_ba.reshape(batch, seq, LANES),
      conv_prev, conv_prev, conv_prev, cw, cw, cw, par, w['gdn_norm_w'][o_idx].reshape(1, DK), s0, tri)
    o_c, s_out, qc, kc, vc = outs
    return o_c.reshape(batch * seq, gk), jnp.concatenate([qc, kc, vc], axis=-1), s_out


def _run_trunk(x, fox_past, rw_state, rw_shift, gdn_state, gdn_conv, ffn_conv, page_table, w, wb):
    batch, seq, d = x.shape
    n = batch * seq
    depth = w['norm_mix'].shape[0]
    tm = 512 if n % 512 == 0 else n
    x = x.reshape(n, d)
    fk, fv, flf, rws, rwsh, gs, gcv, fcv = [], [], [], [], [], [], [], []
    v_first = None
    for layer in range(depth):
        if layer % 2 == 0:
            e = layer // 2
            p_fox, p_rw, p_fl = _norm_matmul(x, w['norm_mix'][layer], wb['ev_in'], e, (4 * FOX_W, 1792, LANES), tm)
            if fox_past is None:
                q_aug, k_aug, v_bf, k_out, lf = _fox_prep(p_fox, p_fl, w['fox_q_gain'][e], w['fox_k_gain'][e],
                                                          w['fox_b_f'][e], batch, seq)
                o_attn = _fox_attn(q_aug, k_aug, v_bf, batch, seq)
            else:
                o_attn, k_out, lf = _fox_decode(p_fox, p_fl, w['fox_q_gain'][e], w['fox_k_gain'][e], w['fox_b_f'][e],
                                                fox_past[0], fox_past[1], fox_past[2], page_table, e, batch, seq)
            parts, sh = _rw_prep(p_rw, rw_shift[e], w, e, v_first, batch, seq)
            if e == 0:
                v_first = parts[3]
            o_b, s_blk = _rw_scan(parts, w, e, _rw_state_to_blocks(rw_state[e]), batch, seq)
            x = _ev_out(o_attn, p_fox, o_b, wb['ev_out'], e, x, tm)
            fk.append(k_out.reshape(batch, seq, H_A, DH))
            fv.append(p_fox[:, 2 * FOX_W:3 * FOX_W].reshape(batch, seq, H_A, DH))
            flf.append(lf[:, :H_A].reshape(batch, seq, H_A))
            rws.append(_rw_blocks_to_state(s_blk))
            rwsh.append(sh)
        else:
            o_idx = layer // 2
            p_qkv, p_z, p_ba = _norm_matmul(x, w['norm_mix'][layer], wb['od_in'], o_idx,
                                            (3 * H_C * DK, H_C * DK, LANES), tm)
            o_c, cv, s_out = _gdn(p_qkv, p_z, p_ba, gdn_conv[o_idx], gdn_state[o_idx], w, o_idx, batch, seq)
            x = _od_out(o_c, wb['od_out'], o_idx, x, tm)
            gs.append(s_out)
            gcv.append(cv)
        x, buf = _ffn(x, w['norm_ffn'][layer], wb['ffn_up'], w['ffn_conv_w'][layer],
                      w['ffn_conv_b'][layer], wb['ffn_down'], layer, ffn_conv[layer], batch, seq)
        fcv.append(buf)
    y = _final_norm(x, w['norm_out'], tm).reshape(batch, seq, d)
    return (y, jnp.stack(fk), jnp.stack(fv), jnp.stack(flf), jnp.stack(rws), jnp.stack(rwsh),
            jnp.stack(gs), jnp.stack(gcv), jnp.stack(fcv))


def _prep_weights(w):
    fox_cols = 4 * FOX_W + H_A
    ev = w['ev_w_in']
    pad = jnp.zeros(ev.shape[:2] + (LANES - H_A,), ev.dtype)
    ev_in = jnp.concatenate([ev[..., :4 * FOX_W], ev[..., fox_cols:], ev[..., 4 * FOX_W:fox_cols], pad], axis=-1)
    od = w['od_w_in']
    pad2 = jnp.zeros(od.shape[:2] + (LANES - 2 * H_C,), od.dtype)
    od_in = jnp.concatenate([od, pad2], axis=-1)
    return dict(ev_in=ev_in.astype(BF16), ev_out=w['ev_w_out'].astype(BF16), od_in=od_in.astype(BF16),
                od_out=w['od_w_out'].astype(BF16), ffn_up=w['ffn_w_up'].astype(BF16),
                ffn_down=w['ffn_w_down'].astype(BF16))


def kernel(x_prompt, x_sample, cache_fox_k, cache_fox_v, cache_fox_logf, state_rwkv, state_rwkv_shift, state_gdn, state_gdn_conv, state_ffn_conv, page_table, norm_mix, norm_ffn, norm_out, ev_w_in, ev_w_out, fox_b_f, fox_q_gain, fox_k_gain, rw_mu, rw_w0, rw_w2, rw_a0, rw_a2, rw_g2, rw_k_k, rw_k_a, rw_r_k, rw_ln_w, rw_ln_b, rw_v0, rw_v1, rw_v2, od_w_in, od_w_out, gdn_conv_w, gdn_A_log, gdn_dt_bias, gdn_norm_w, ffn_w_up, ffn_conv_w, ffn_conv_b, ffn_w_down):
    w = dict(norm_mix=norm_mix, norm_ffn=norm_ffn, norm_out=norm_out, ev_w_in=ev_w_in, ev_w_out=ev_w_out,
             fox_b_f=fox_b_f, fox_q_gain=fox_q_gain, fox_k_gain=fox_k_gain, rw_mu=rw_mu, rw_w0=rw_w0,
             rw_w2=rw_w2, rw_a0=rw_a0, rw_a2=rw_a2, rw_g2=rw_g2, rw_k_k=rw_k_k, rw_k_a=rw_k_a, rw_r_k=rw_r_k,
             rw_ln_w=rw_ln_w, rw_ln_b=rw_ln_b, rw_v0=rw_v0, rw_v1=rw_v1, rw_v2=rw_v2, od_w_in=od_w_in,
             od_w_out=od_w_out, gdn_conv_w=gdn_conv_w, gdn_A_log=gdn_A_log, gdn_dt_bias=gdn_dt_bias,
             gdn_norm_w=gdn_norm_w, ffn_w_up=ffn_w_up, ffn_conv_w=ffn_conv_w, ffn_conv_b=ffn_conv_b,
             ffn_w_down=ffn_w_down)
    wb = _prep_weights(w)
    bp = x_prompt.shape[0]
    n_even, n_odd, depth = ev_w_in.shape[0], od_w_in.shape[0], norm_mix.shape[0]
    rw_cols = rw_mu.shape[1]
    (y_p, fk_p, fv_p, flf_p, rw_p, rwsh_p, gdn_p, gcv_p, fcv_p) = _run_trunk(
        x_prompt, None,
        jnp.zeros((n_even, bp, H_A, DH, DH), F32), jnp.zeros((n_even, bp, rw_cols), F32),
        jnp.zeros((n_odd, bp, H_C, DK, DK), F32), jnp.zeros((n_odd, bp, 3, 3 * H_C * DK), F32),
        jnp.zeros((depth, bp, 2, ffn_w_up.shape[2]), F32), page_table, w, wb)
    (y_s, fk_s, fv_s, flf_s, rw_s, rwsh_s, gdn_s, gcv_s, fcv_s) = _run_trunk(
        x_sample, (cache_fox_k, cache_fox_v, cache_fox_logf), state_rwkv, state_rwkv_shift,
        state_gdn, state_gdn_conv, state_ffn_conv, page_table, w, wb)
    return (y_p, y_s, fk_p, fv_p, flf_p, fk_s, fv_s, flf_s, rw_p, rw_s, rwsh_p, rwsh_s,
            gdn_p, gdn_s, gcv_p, gcv_s, fcv_p, fcv_s)
```

```python
import functools
import math

import jax
import jax.numpy as jnp
import numpy as np
from jax import lax
from jax.experimental import pallas as pl
from jax.experimental.pallas import tpu as pltpu

F32 = jnp.float32
BF16 = jnp.bfloat16

EPS = 1e-6
RW_GN_EPS = 64e-5
L2_EPS = 1e-6
NEG_BIG = -1e30

H_A = 8
DH = 64
FOX_W = 512
RW_W = 512
H_C = 8
DK = 128
CHUNK = 64
LANES = 128
VMEM_LIMIT = 56 * 1024 * 1024


def _cparams(*sem):
    return pltpu.CompilerParams(dimension_semantics=sem, vmem_limit_bytes=VMEM_LIMIT)


def _const_spec(shape):
    nd = len(shape)
    return pl.BlockSpec(shape, lambda *_: (0,) * nd, pipeline_mode=pl.Buffered(1))


def _layer_spec(stack, layer):
    return pl.BlockSpec((None,) + stack.shape[1:], lambda *_: (layer, 0, 0), pipeline_mode=pl.Buffered(1))


def _mm(a, b):
    return jnp.dot(a, b, preferred_element_type=F32)


def _mm_nt(a, b):
    return lax.dot_general(a, b, (((1,), (1,)), ((), ())), preferred_element_type=F32)


def _mm_tn(a, b):
    return lax.dot_general(a, b, (((0,), (0,)), ((), ())), preferred_element_type=F32)


def _split3(x):
    hi = x.astype(BF16)
    r = x - hi.astype(F32)
    mid = r.astype(BF16)
    lo = (r - mid.astype(F32)).astype(BF16)
    return hi, mid, lo


def _sel_l(m01, x):
    hi, mid, lo = _split3(x)
    return _mm(m01, hi) + _mm(m01, mid) + _mm(m01, lo)


def _sel_r(x, m01):
    hi, mid, lo = _split3(x)
    return _mm(hi, m01) + _mm(mid, m01) + _mm(lo, m01)


def _sel_nt(m01, x):
    hi, mid, lo = _split3(x)
    return _mm_nt(m01, hi) + _mm_nt(m01, mid) + _mm_nt(m01, lo)


def _log_sigmoid(z):
    return jnp.minimum(z, 0.0) - jnp.log1p(jnp.exp(-jnp.abs(z)))


def _sigmoid(z):
    return 1.0 / (1.0 + jnp.exp(-z))


def _inv_unit_lower(ns):
    size = ns[0].shape[0]
    r = lax.broadcasted_iota(jnp.int32, (size, size), 0)
    c = lax.broadcasted_iota(jnp.int32, (size, size), 1)
    eye = jnp.where(r == c, 1.0, 0.0)
    ps = [eye + n for n in ns]
    nks = list(ns)
    for _ in range(int(math.log2(CHUNK)) - 1):
        nks = [_mm(nk.astype(BF16), nk.astype(BF16)) for nk in nks]
        ps = [p + _mm(p.astype(BF16), nk.astype(BF16)) for p, nk in zip(ps, nks)]
    return ps


def _head_scale(x2, fn):
    lane_h = lax.broadcasted_iota(jnp.int32, (1, x2.shape[1]), 1) // DH
    out = jnp.zeros_like(x2)
    for h in range(x2.shape[1] // DH):
        s = jnp.sum(x2[:, h * DH:(h + 1) * DH], axis=-1, keepdims=True)
        out = jnp.where(lane_h == h, fn(s), out)
    return out


def _rms_heads(x, gain):
    return x * _head_scale(x * x, lambda s: lax.rsqrt(s * (1.0 / DH) + EPS)) * gain


def _shift_rows(x, k, prev):
    rolled = pltpu.roll(x, k, 0)
    sub = 8
    row = lax.broadcasted_iota(jnp.int32, (sub, x.shape[1]), 0)
    head = rolled[0:sub, :]
    nprev = prev.shape[0]
    for j in range(k):
        head = jnp.where(row == j, prev[nprev - k + j:nprev - k + j + 1, :], head)
    return head if x.shape[0] == sub else jnp.concatenate([head, rolled[sub:, :]], axis=0)


def _nm_body(x_ref, g_ref, w_ref, *o_refs, splits):
    x = x_ref[...]
    xn = (x * lax.rsqrt(jnp.mean(x * x, axis=-1, keepdims=True) + EPS) * g_ref[...]).astype(BF16)
    off = 0
    for o_ref, n in zip(o_refs, splits):
        o_ref[...] = _mm(xn, w_ref[:, off:off + n])
        off += n


def _norm_matmul(x, g, w_stack, layer, splits, tm):
    n, d = x.shape
    ntot = w_stack.shape[2]
    assert sum(splits) == ntot and n % tm == 0
    return pl.pallas_call(
        functools.partial(_nm_body, splits=tuple(splits)),
        grid=(n // tm,),
        in_specs=[pl.BlockSpec((tm, d), lambda i: (i, 0)), _const_spec((1, d)), _layer_spec(w_stack, layer)],
        out_specs=[pl.BlockSpec((tm, s), lambda i: (i, 0)) for s in splits],
        out_shape=[jax.ShapeDtypeStruct((n, s), F32) for s in splits],
        compiler_params=_cparams("arbitrary"),
        name="norm_matmul",
    )(x, g.reshape(1, d), w_stack)


def _final_norm_body(x_ref, g_ref, o_ref):
    x = x_ref[...]
    o_ref[...] = x * lax.rsqrt(jnp.mean(x * x, axis=-1, keepdims=True) + EPS) * g_ref[...]


def _final_norm(x, g, tm):
    n, d = x.shape
    return pl.pallas_call(
        _final_norm_body, grid=(n // tm,),
        in_specs=[pl.BlockSpec((tm, d), lambda i: (i, 0)), _const_spec((1, d))],
        out_specs=pl.BlockSpec((tm, d), lambda i: (i, 0)),
        out_shape=jax.ShapeDtypeStruct((n, d), F32),
        compiler_params=_cparams("arbitrary"), name="final_norm",
    )(x, g.reshape(1, d))


def _evout_body(oa_ref, og_ref, ob_ref, w_ref, x_ref, o_ref):
    a = (oa_ref[...] * _sigmoid(og_ref[...])).astype(BF16)
    b = ob_ref[...].astype(BF16)
    o_ref[...] = x_ref[...] + _mm(a, w_ref[0:FOX_W, :]) + _mm(b, w_ref[FOX_W:FOX_W + RW_W, :])


def _ev_out(o_attn, p_fox, o_b, w_stack, layer, x, tm):
    n, d = x.shape
    return pl.pallas_call(
        _evout_body, grid=(n // tm,),
        in_specs=[pl.BlockSpec((tm, FOX_W), lambda i: (i, 0)),
                  pl.BlockSpec((tm, FOX_W), lambda i: (i, 3)),
                  pl.BlockSpec((tm, RW_W), lambda i: (i, 0)),
                  _layer_spec(w_stack, layer),
                  pl.BlockSpec((tm, d), lambda i: (i, 0))],
        out_specs=pl.BlockSpec((tm, d), lambda i: (i, 0)),
        out_shape=jax.ShapeDtypeStruct((n, d), F32),
        compiler_params=_cparams("arbitrary"), name="ev_out",
    )(o_attn, p_fox, o_b, w_stack, x)


def _odout_body(oc_ref, w_ref, x_ref, o_ref):
    o_ref[...] = x_ref[...] + _mm(oc_ref[...].astype(BF16), w_ref[...])


def _od_out(o_c, w_stack, layer, x, tm):
    n, d = x.shape
    k = o_c.shape[1]
    return pl.pallas_call(
        _odout_body, grid=(n // tm,),
        in_specs=[pl.BlockSpec((tm, k), lambda i: (i, 0)), _layer_spec(w_stack, layer),
                  pl.BlockSpec((tm, d), lambda i: (i, 0))],
        out_specs=pl.BlockSpec((tm, d), lambda i: (i, 0)),
        out_shape=jax.ShapeDtypeStruct((n, d), F32),
        compiler_params=_cparams("arbitrary"), name="od_out",
    )(o_c, w_stack, x)


def _ffn_cols(xn, wup_ref, cw_ref, cb_ref, wdn_ref, acc, prev_fn, tail_fn, f, cwb):
    for c in range(f // cwb):
        ys = []
        for half in (0, 1):
            lo = half * f + c * cwb
            h = _mm(xn, wup_ref[:, lo:lo + cwb])
            hm1, hm2 = prev_fn(h, lo)
            ys.append(hm2 * cw_ref[0:1, lo:lo + cwb] + hm1 * cw_ref[1:2, lo:lo + cwb]
                      + h * cw_ref[2:3, lo:lo + cwb] + cb_ref[:, lo:lo + cwb])
            tail_fn(h, lo)
        u, gt = ys
        act = (gt * _sigmoid(gt) * u).astype(BF16)
        acc = acc + _mm(act, wdn_ref[c * cwb:(c + 1) * cwb, :])
    return acc


def _ffn_norm(x_ref, g_ref):
    x = x_ref[...]
    return x, (x * lax.rsqrt(jnp.mean(x * x, axis=-1, keepdims=True) + EPS) * g_ref[...]).astype(BF16)


def _ffn_seq_body(x_ref, g_ref, wup_ref, cw_ref, cb_ref, wdn_ref, init_ref, o_ref, st_ref, carry_ref,
                  *, tb, f, cwb):
    @pl.when(pl.program_id(1) == 0)
    def _():
        carry_ref[0:2, :] = init_ref[0]

    x, xn = _ffn_norm(x_ref, g_ref)

    def prev_fn(h, lo):
        prev = carry_ref[0:2, lo:lo + cwb]
        return _shift_rows(h, 1, prev), _shift_rows(h, 2, prev)

    def tail_fn(h, lo):
        carry_ref[0:2, lo:lo + cwb] = h[tb - 2:tb, :]
        st_ref[0, :, lo:lo + cwb] = h[tb - 2:tb, :]

    o_ref[...] = _ffn_cols(xn, wup_ref, cw_ref, cb_ref, wdn_ref, x, prev_fn, tail_fn, f, cwb)


def _ffn_flat_body(x_ref, g_ref, wup_ref, cw_ref, cb_ref, wdn_ref, f1_ref, f2_ref, o_ref, st_ref,
                   *, rows, seq, f, cwb):
    x, xn = _ffn_norm(x_ref, g_ref)
    tmod = lax.broadcasted_iota(jnp.int32, (rows, cwb), 0) % seq

    def prev_fn(h, lo):
        hm1 = jnp.where(tmod == 0, f1_ref[:, lo:lo + cwb], pltpu.roll(h, 1, 0))
        hm2 = jnp.where(tmod < 2, f2_ref[:, lo:lo + cwb], pltpu.roll(h, 2, 0))
        return hm1, hm2

    def tail_fn(h, lo):
        st_ref[:, :, lo:lo + cwb] = h.reshape(rows // seq, seq, cwb)[:, seq - 2:seq, :]

    o_ref[...] = _ffn_cols(xn, wup_ref, cw_ref, cb_ref, wdn_ref, x, prev_fn, tail_fn, f, cwb)


def _ffn(x, g, wup_stack, conv_w, conv_b, wdn_stack, layer, conv_prev, batch, seq):
    n, d = x.shape
    f2 = wup_stack.shape[2]
    f = f2 // 2
    cwb = f
    weights = [_const_spec((1, d)), _layer_spec(wup_stack, layer), _const_spec((3, f2)), _const_spec((1, f2)),
               _layer_spec(wdn_stack, layer)]
    out_shape = [jax.ShapeDtypeStruct((n, d), F32), jax.ShapeDtypeStruct((batch, 2, f2), F32)]
    args = (x, g.reshape(1, d), wup_stack, conv_w, conv_b.reshape(1, f2), wdn_stack)
    if seq >= 256:
        tb = 512 if seq % 512 == 0 else 256
        nt = seq // tb
        return pl.pallas_call(
            functools.partial(_ffn_seq_body, tb=tb, f=f, cwb=cwb),
            grid=(batch, nt),
            in_specs=[pl.BlockSpec((tb, d), lambda b, t: (b * nt + t, 0))] + weights
                     + [pl.BlockSpec((1, 2, f2), lambda b, t: (b, 0, 0))],
            out_specs=[pl.BlockSpec((tb, d), lambda b, t: (b * nt + t, 0)),
                       pl.BlockSpec((1, 2, f2), lambda b, t: (b, 0, 0))],
            out_shape=out_shape,
            scratch_shapes=[pltpu.VMEM((8, f2), F32)],
            compiler_params=_cparams("arbitrary", "arbitrary"), name="ffn_seq",
        )(*args, conv_prev)
    zeros = jnp.zeros((batch, seq - 2, f2), F32)
    fill2 = jnp.concatenate([conv_prev, zeros], axis=1).reshape(n, f2)
    fill1 = jnp.concatenate([conv_prev[:, 1:2], zeros, zeros[:, :1]], axis=1).reshape(n, f2)
    return pl.pallas_call(
        functools.partial(_ffn_flat_body, rows=n, seq=seq, f=f, cwb=cwb),
        grid=(1,),
        in_specs=[pl.BlockSpec((n, d), lambda i: (0, 0))] + weights
                 + [pl.BlockSpec((n, f2), lambda i: (0, 0)), pl.BlockSpec((n, f2), lambda i: (0, 0))],
        out_specs=[pl.BlockSpec((n, d), lambda i: (0, 0)), pl.BlockSpec((batch, 2, f2), lambda i: (0, 0, 0))],
        out_shape=out_shape,
        compiler_params=_cparams("arbitrary"), name="ffn_flat",
    )(*args, fill1, fill2)


def _fox_consts():
    src = np.arange(FOX_W)
    place = np.zeros((FOX_W, H_A * LANES), np.float32)
    place[src, (src // DH) * LANES + src % DH] = 1.0
    hh = np.arange(H_A)
    eq = np.zeros((3, LANES, H_A * LANES), np.float32)
    ek = np.zeros((3, LANES, H_A * LANES), np.float32)
    cq = np.zeros((1, H_A * LANES), np.float32)
    ck = np.zeros((1, H_A * LANES), np.float32)
    for piece in range(3):
        eq[piece, hh, hh * LANES + DH + piece] = 1.0
        ek[piece, hh, hh * LANES + DH + 3 + piece] = -1.0
        cq[0, hh * LANES + DH + 3 + piece] = 1.0
        ck[0, hh * LANES + DH + piece] = 1.0
    as_bf = lambda a: jnp.asarray(a, BF16)
    return as_bf(place), as_bf(place.T), as_bf(eq), as_bf(ek), jnp.asarray(cq), jnp.asarray(ck)


def _fox_prep_body(q_ref, k_ref, v_ref, fl_ref, qg_ref, kg_ref, bf_ref, pm_ref, pt_ref, eq_ref, ek_ref, cq_ref,
                   ck_ref, vone_ref, tri_ref, bd_ref, qa_ref, ka_ref, vt_ref, ko_ref, lf_ref, carry_ref, *, tm):
    @pl.when(pl.program_id(1) == 0)
    def _():
        carry_ref[...] = jnp.zeros_like(carry_ref)

    bd = bd_ref[...]

    def rms_heads(x, gain):
        x2 = x * x
        hi = x2.astype(BF16)
        mid = (x2 - hi.astype(F32)).astype(BF16)
        ms = (_mm(hi, bd) + _mm(mid, bd)) * (1.0 / DH)
        return x * lax.rsqrt(ms + EPS) * gain

    qn = rms_heads(q_ref[...], qg_ref[...]) * (DH ** -0.5)
    kn = rms_heads(k_ref[...], kg_ref[...])
    lf = _log_sigmoid(fl_ref[...] + bf_ref[...])
    lf_ref[...] = lf
    fcum = _sel_l(tri_ref[...], lf) + carry_ref[0:1, :]
    carry_ref[0:1, :] = fcum[tm - 1:tm, :]
    fh, fm, flo = _split3(fcum)
    pm = pm_ref[...]
    qa = (_mm(qn.astype(BF16), pm) + _mm(fh, eq_ref[0]) + _mm(fm, eq_ref[1]) + _mm(flo, eq_ref[2])
          + cq_ref[...])
    ka = (_mm(kn.astype(BF16), pm) + _mm(fh, ek_ref[0]) + _mm(fm, ek_ref[1]) + _mm(flo, ek_ref[2])
          + ck_ref[...])
    qa_ref[...] = qa.astype(BF16)
    ka_ref[...] = ka.astype(BF16)
    vt = _mm_nt(pt_ref[...], v_ref[...].astype(BF16)) + vone_ref[...]
    vt_ref[0, :, 0] = vt.astype(BF16).reshape(H_A, LANES, tm)
    ko_ref[...] = kn


FOX_TILE = 512
FOX_QSUB = 4


def _fox_tile(seq):
    return min(FOX_TILE, seq // FOX_QSUB)


def _pad_lanes(v, offset=0):
    return jnp.concatenate([jnp.zeros((offset,), F32), v.astype(F32),
                            jnp.zeros((LANES - offset - v.shape[0],), F32)]).reshape(1, LANES)


def _fox_prep(p_fox, p_fl, q_gain, k_gain, b_f, batch, seq):
    n = p_fox.shape[0]
    tm = _fox_tile(seq)
    nt = seq // tm
    place, place_t, eq, ek, cq, ck = _fox_consts()
    vone = np.zeros((H_A * LANES, 1), np.float32)
    vone[np.arange(H_A) * LANES + DH, 0] = 1.0
    vone = jnp.asarray(vone)
    tri = jnp.asarray(np.tril(np.ones((tm, tm), np.float32)), BF16)
    lane = np.arange(FOX_W)
    bd = jnp.asarray((lane[:, None] // DH == lane[None, :] // DH).astype(np.float32), BF16)
    bf = _pad_lanes(b_f)
    row = lambda c: pl.BlockSpec((tm, FOX_W), lambda b, t, c=c: (b * nt + t, c))
    wide = pl.BlockSpec((tm, H_A * LANES), lambda b, t: (b * nt + t, 0))
    narrow = pl.BlockSpec((tm, LANES), lambda b, t: (b * nt + t, 0))
    return pl.pallas_call(
        functools.partial(_fox_prep_body, tm=tm),
        grid=(batch, nt),
        in_specs=[row(0), row(1), row(2), narrow,
                  _const_spec((1, FOX_W)), _const_spec((1, FOX_W)), _const_spec((1, LANES)),
                  _const_spec(place.shape), _const_spec(place_t.shape), _const_spec(eq.shape), _const_spec(ek.shape),
                  _const_spec(cq.shape), _const_spec(ck.shape), _const_spec(vone.shape), _const_spec(tri.shape),
                  _const_spec(bd.shape)],
        out_specs=[wide, wide, pl.BlockSpec((1, H_A, 1, LANES, tm), lambda b, t: (b, 0, t, 0, 0)), row(0), narrow],
        out_shape=[jax.ShapeDtypeStruct((n, H_A * LANES), BF16), jax.ShapeDtypeStruct((n, H_A * LANES), BF16),
                   jax.ShapeDtypeStruct((batch, H_A, nt, LANES, tm), BF16), jax.ShapeDtypeStruct((n, FOX_W), F32),
                   jax.ShapeDtypeStruct((n, LANES), F32)],
        scratch_shapes=[pltpu.VMEM((8, LANES), F32)],
        compiler_params=_cparams("arbitrary", "arbitrary"), name="fox_prep",
    )(p_fox, p_fox, p_fox, p_fl, jnp.tile(q_gain, H_A).reshape(1, FOX_W),
      jnp.tile(k_gain, H_A).reshape(1, FOX_W), bf, place, place_t, eq, ek, cq, ck, vone, tri, bd)


def _fox_attn_body(qa_ref, ka_ref, vt_ref, o_ref, *, tq):
    i = pl.program_id(2)
    key = lax.broadcasted_iota(jnp.int32, (tq, tq), 0)
    qry = lax.broadcasted_iota(jnp.int32, (tq, tq), 1)
    causal = key <= qry
    nsb = FOX_QSUB
    chains = [(sb, hh) for sb in range(nsb) for hh in range(2)]
    qs = [qa_ref[sb * tq:(sb + 1) * tq, hh * LANES:(hh + 1) * LANES] for sb, hh in chains]

    def block(j, carry, active):
        off = pl.multiple_of(j * tq, tq)
        ks = [ka_ref[pl.ds(off, tq), hh * LANES:(hh + 1) * LANES] for hh in range(2)]
        vts = [vt_ref[0, hh, j] for hh in range(2)]
        sts = {c: _mm_nt(ks[chains[c][1]], qs[c]) for c in active}
        sts = {c: jnp.where(causal, st, NEG_BIG) if active[c] else st for c, st in sts.items()}
        m2s = {c: jnp.maximum(carry[c][0], jnp.max(st, axis=0, keepdims=True)) for c, st in sts.items()}
        ps = {c: jnp.exp(st - m2s[c]).astype(BF16) for c, st in sts.items()}
        return tuple((m2s[c], jnp.exp(carry[c][0] - m2s[c]) * carry[c][1] + _mm(vts[chains[c][1]], ps[c]))
                     if c in active else carry[c] for c in range(len(chains)))

    init = tuple((jnp.full((1, tq), NEG_BIG, F32), jnp.zeros((LANES, tq), F32)) for _ in chains)
    carry = lax.fori_loop(0, nsb * i, lambda j, c: block(j, c, {c_: False for c_ in range(len(chains))}), init)
    for d in range(nsb):
        carry = block(nsb * i + d, carry, {c: chains[c][0] == d for c in range(len(chains)) if chains[c][0] >= d})
    for sb in range(nsb):
        halves = [acc[0:DH, :] / acc[DH:DH + 1, :] for _, acc in carry[2 * sb:2 * sb + 2]]
        o_ref[sb * tq:(sb + 1) * tq, :] = jnp.transpose(jnp.concatenate(halves, axis=0))


def _fox_attn(q_aug, k_aug, v_t, batch, seq):
    n = q_aug.shape[0]
    tq = _fox_tile(seq)
    rows = FOX_QSUB * tq
    nq = seq // rows
    return pl.pallas_call(
        functools.partial(_fox_attn_body, tq=tq),
        grid=(batch, H_A // 2, nq),
        in_specs=[pl.BlockSpec((rows, 2 * LANES), lambda b, hp, i: (b * nq + i, hp)),
                  pl.BlockSpec((seq, 2 * LANES), lambda b, hp, i: (b, hp)),
                  pl.BlockSpec((1, 2, seq // tq, LANES, tq), lambda b, hp, i: (b, hp, 0, 0, 0))],
        out_specs=pl.BlockSpec((rows, LANES), lambda b, hp, i: (b * nq + i, hp)),
        out_shape=jax.ShapeDtypeStruct((n, FOX_W), F32),
        compiler_params=_cparams("arbitrary", "arbitrary", "arbitrary"), name="fox_attn",
    )(q_aug, k_aug, v_t)


def _fox_decode_body(pt_ref, q_ref, k_ref, v_ref, fl_ref, qg_ref, kg_ref, bf_ref, ms_ref, ps_ref, pa_ref,
                     asel_ref, *rest, nb, pps, seq):
    del pt_ref
    npg = nb * pps
    kp, vp, lp = rest[:npg], rest[npg:2 * npg], rest[2 * npg:3 * npg]
    o_ref, ko_ref, lfo_ref = rest[3 * npg:3 * npg + 3]
    qh_ref, cb_ref, m_ref, l_ref, acc_ref, car_ref = rest[3 * npg + 3:]
    j = pl.program_id(1)
    nrow = H_A * seq

    @pl.when(j == 0)
    def _new_tokens():
        q_all = _rms_heads(q_ref[...], qg_ref[...]) * (DH ** -0.5)
        k_all = _rms_heads(k_ref[...], kg_ref[...])
        ko_ref[...] = k_all
        lf_all = _log_sigmoid(fl_ref[...] + bf_ref[...])
        lfo_ref[...] = lf_all
        row = lax.broadcasted_iota(jnp.int32, (seq, LANES), 0)
        zpad = jnp.zeros((LANES - seq, DH), F32)
        keyi = lax.broadcasted_iota(jnp.int32, (nrow, LANES), 1)
        ti = lax.broadcasted_iota(jnp.int32, (nrow, LANES), 0) % seq
        for n in range(nb):
            rs = slice(n * seq, (n + 1) * seq)
            q, k, v = q_all[rs, :], k_all[rs, :], v_ref[rs, :]
            c = lf_all[rs, :]
            s = 1
            while s < seq:
                c = c + jnp.where(row >= s, pltpu.roll(c, s, 0), 0.0)
                s *= 2
            srows = []
            for h in range(H_A):
                qh = q[:, h * DH:(h + 1) * DH]
                qh_ref[n, h] = qh
                cb_ref[n, h * seq:(h + 1) * seq, :] = jnp.broadcast_to(c[:, h:h + 1], (seq, LANES))
                kpad = jnp.concatenate([k[:, h * DH:(h + 1) * DH], zpad], axis=0).astype(BF16)
                srows.append(_mm_nt(qh.astype(BF16), kpad))
            cneg = jnp.concatenate([-c, jnp.zeros((LANES - seq, LANES), F32)], axis=0)
            s_new = jnp.concatenate(srows, axis=0) + cb_ref[n] + _sel_nt(asel_ref[0:nrow, :], cneg)
            s_new = jnp.where(keyi <= ti, s_new, NEG_BIG)
            m = jnp.max(s_new, axis=-1, keepdims=True)
            p = jnp.exp(s_new - m)
            m_ref[n] = m
            l_ref[n] = jnp.sum(p, axis=-1, keepdims=True)
            accs = []
            for h in range(H_A):
                vpad = jnp.concatenate([v[:, h * DH:(h + 1) * DH], zpad], axis=0).astype(BF16)
                accs.append(_mm(p[h * seq:(h + 1) * seq, :].astype(BF16), vpad))
            acc_ref[n] = jnp.concatenate(accs, axis=0)
        car_ref[...] = jnp.zeros_like(car_ref)

    seqs = range(nb)
    lfts = [jnp.concatenate([lp[n * pps + i][...] for i in range(pps)]
                            + [jnp.zeros((LANES - pps * H_A, LANES), F32)], axis=0) for n in seqs]
    tots = [jnp.broadcast_to(jnp.sum(lft, axis=1, keepdims=True), (LANES, LANES)) for lft in lfts]
    scores = [[jnp.concatenate([_mm(qh_ref[n, h].astype(BF16), kp[n * pps + i][h].astype(BF16))
                                for h in range(H_A)], axis=0) for i in range(pps)] for n in seqs]
    cars = [car_ref[n] for n in seqs]
    rfulls = [_sel_r(lft, ms_ref[...]) + _sel_l(ps_ref[...], tot) + car for lft, tot, car in zip(lfts, tots, cars)]
    for n in seqs:
        car_ref[n] = cars[n] + _sel_l(pa_ref[...], tots[n])
    biases = [_sel_l(asel_ref[...], rfull) for rfull in rfulls]
    s_alls = [jnp.concatenate([scores[n][i] + biases[n][i * nrow:(i + 1) * nrow, :] + cb_ref[n]
                               for i in range(pps)], axis=1) for n in seqs]
    m_olds = [m_ref[n] for n in seqs]
    m_news = [jnp.maximum(m_old, jnp.max(s_all, axis=-1, keepdims=True)) for m_old, s_all in zip(m_olds, s_alls)]
    probs = [jnp.exp(s_all - m_new) for s_all, m_new in zip(s_alls, m_news)]
    alphas = [jnp.exp(m_old - m_new) for m_old, m_new in zip(m_olds, m_news)]
    pvs = [jnp.concatenate(
        [_mm_nt(probs[n][h * seq:(h + 1) * seq, :].astype(BF16),
                jnp.concatenate([vp[n * pps + i][h] for i in range(pps)], axis=1).astype(BF16))
         for h in range(H_A)], axis=0) for n in seqs]
    for n in seqs:
        m_ref[n] = m_news[n]
        l_ref[n] = alphas[n] * l_ref[n] + jnp.sum(probs[n], axis=-1, keepdims=True)
        acc_ref[n] = alphas[n] * acc_ref[n] + pvs[n]

    @pl.when(j == pl.num_programs(1) - 1)
    def _():
        for n in seqs:
            o = acc_ref[n] / l_ref[n]
            for h in range(H_A):
                o_ref[n * seq:(n + 1) * seq, h * DH:(h + 1) * DH] = o[h * seq:(h + 1) * seq, :]


def _fox_decode(p_fox, p_fl, q_gain, k_gain, b_f, pool_k, pool_v, pool_lf, page_table, e, batch, seq):
    n = p_fox.shape[0]
    n_pages = page_table.shape[1]
    page = pool_lf.shape[2]
    nb = 4
    assert page == LANES and seq == 8 and batch % nb == 0
    pps = 8
    while n_pages % pps:
        pps //= 2
    groups = n_pages // pps
    pk = jnp.transpose(pool_k, (0, 1, 3, 4, 2))
    pv = jnp.transpose(pool_v, (0, 1, 3, 4, 2))
    plf = jnp.transpose(pool_lf, (0, 1, 3, 2))
    nrow = H_A * seq
    idx = np.arange(LANES)
    as_bf = lambda a: jnp.asarray(a.astype(np.float32), BF16)
    ms = as_bf(idx[:, None] > idx[None, :])
    same_h = (idx[:, None] % H_A) == (idx[None, :] % H_A)
    valid = (idx[:, None] < pps * H_A) & (idx[None, :] < pps * H_A)
    ps = as_bf(same_h & valid & (idx[None, :] // H_A < idx[:, None] // H_A))
    pa = as_bf(same_h & valid)
    r = np.arange(pps * nrow)
    asel = np.zeros((pps * nrow, LANES), np.float32)
    asel[r, (r // nrow) * H_A + (r % nrow) // seq] = 1.0
    asel = as_bf(asel)
    bf = _pad_lanes(b_f)

    def page_spec(s, i, shape):
        def index(b, j, pt):
            return (e, pt[b * nb + s, n_pages - 1 - (j * pps + i)]) + (0,) * len(shape)
        return pl.BlockSpec((None, None) + shape, index)

    pages = lambda shape: [page_spec(s, i, shape) for s in range(nb) for i in range(pps)]
    rowspec = lambda c: pl.BlockSpec((nb * seq, FOX_W), lambda b, j, pt, c=c: (b, c))
    narrow = pl.BlockSpec((nb * seq, LANES), lambda b, j, pt: (b, 0))
    cst = lambda shape: pl.BlockSpec(shape, lambda b, j, pt: (0,) * len(shape))
    in_specs = ([rowspec(0), rowspec(1), rowspec(2), narrow,
                 cst((1, FOX_W)), cst((1, FOX_W)), cst((1, LANES)), cst(ms.shape), cst(ps.shape), cst(pa.shape),
                 cst(asel.shape)]
                + pages((H_A, DH, page)) + pages((H_A, DH, page)) + pages((H_A, page)))
    grid_spec = pltpu.PrefetchScalarGridSpec(
        num_scalar_prefetch=1, grid=(batch // nb, groups), in_specs=in_specs,
        out_specs=[rowspec(0), rowspec(0), narrow],
        scratch_shapes=[pltpu.VMEM((nb, H_A, seq, DH), F32), pltpu.VMEM((nb, nrow, LANES), F32),
                        pltpu.VMEM((nb, nrow, 1), F32), pltpu.VMEM((nb, nrow, 1), F32),
                        pltpu.VMEM((nb, nrow, DH), F32), pltpu.VMEM((nb, LANES, LANES), F32)])
    return pl.pallas_call(
        functools.partial(_fox_decode_body, nb=nb, pps=pps, seq=seq),
        grid_spec=grid_spec,
        out_shape=[jax.ShapeDtypeStruct((n, FOX_W), F32), jax.ShapeDtypeStruct((n, FOX_W), F32),
                   jax.ShapeDtypeStruct((n, LANES), F32)],
        compiler_params=_cparams("arbitrary", "arbitrary"), name="fox_decode",
    )(page_table, p_fox, p_fox, p_fox, p_fl, jnp.tile(q_gain, H_A).reshape(1, FOX_W),
      jnp.tile(k_gain, H_A).reshape(1, FOX_W), bf, ms, ps, pa, asel,
      *([pk] * (nb * pps)), *([pv] * (nb * pps)), *([plf] * (nb * pps)))


def _rw_prep_body(p_ref, init_ref, mu_ref, w0_ref, w2_ref, a0_ref, a2_ref, g2_ref, kk_ref, ka_ref, *rest,
                  tb, first):
    if first:
        r_o, lw_o, k_o, v_o, a_o, b_o, g_o, sh_o, carry_ref = rest
    else:
        v0_ref, v1_ref, v2_ref, vf_ref, r_o, lw_o, k_o, v_o, a_o, b_o, g_o, sh_o, carry_ref = rest

    @pl.when(pl.program_id(1) == 0)
    def _():
        carry_ref[7:8, :] = init_ref[0]

    p = p_ref[...]
    p_prev = _shift_rows(p, 1, carry_ref[...])
    carry_ref[7:8, :] = p[tb - 1:tb, :]
    sh_o[0] = p[tb - 1:tb, :]
    ps = p + (p_prev - p) * mu_ref[...]
    r = ps[:, 0:RW_W]
    k = ps[:, RW_W:2 * RW_W]
    v = ps[:, 2 * RW_W:3 * RW_W]
    x128 = ps[:, 3 * RW_W:3 * RW_W + LANES]
    gd = ps[:, 3 * RW_W + LANES:3 * RW_W + 2 * LANES]
    lane = lax.broadcasted_iota(jnp.int32, (1, LANES), 1)
    xw = jnp.where(lane < DH, jnp.tanh(x128), 0.0).astype(BF16)
    xa = jnp.where(lane < DH, 0.0, x128).astype(BF16)
    w_log = _log_sigmoid(w0_ref[...] + _mm(xw, w2_ref[...])) - 0.5
    lw_o[...] = -jnp.exp(w_log)
    a = _sigmoid(a0_ref[...] + _mm(xa, a2_ref[...]))
    g_o[...] = _mm(_sigmoid(gd).astype(BF16), g2_ref[...])
    if not first:
        gate = _sigmoid(v0_ref[...] + _mm(_mm(v.astype(BF16), v1_ref[...]).astype(BF16), v2_ref[...]))
        v = v + (vf_ref[...] - v) * gate
    kkx = k * kk_ref[...]
    kk = kkx * _head_scale(kkx * kkx, lambda s: lax.rsqrt(s + L2_EPS))
    r_o[...] = r
    k_o[...] = k * (1.0 + (a - 1.0) * ka_ref[...])
    v_o[...] = v
    a_o[...] = -kk
    b_o[...] = kk * a


def _rw_prep(p_rw, shift_prev, w, e, v_first, batch, seq):
    n, cols = p_rw.shape
    tb = min(256, seq)
    nt = seq // tb
    first = e == 0
    pad_rows = lambda m: jnp.concatenate([m, jnp.zeros((LANES - m.shape[0], m.shape[1]), m.dtype)], axis=0)
    w2p = pad_rows(w['rw_w2'][e]).astype(BF16)
    a2p = jnp.concatenate([jnp.zeros((DH, RW_W), F32), w['rw_a2'][e]], axis=0).astype(BF16)
    vec = lambda x: x.reshape(1, -1)
    args = [p_rw, shift_prev.reshape(batch, 1, cols), vec(w['rw_mu'][e]), vec(w['rw_w0'][e]), w2p,
            vec(w['rw_a0'][e]), a2p, w['rw_g2'][e].astype(BF16), vec(w['rw_k_k'][e]), vec(w['rw_k_a'][e])]
    rowspec = pl.BlockSpec((tb, RW_W), lambda b, t: (b * nt + t, 0))
    in_specs = [pl.BlockSpec((tb, cols), lambda b, t: (b * nt + t, 0)),
                pl.BlockSpec((1, 1, cols), lambda b, t: (b, 0, 0)),
                _const_spec((1, cols)), _const_spec((1, RW_W)), _const_spec((LANES, RW_W)),
                _const_spec((1, RW_W)), _const_spec((LANES, RW_W)), _const_spec((LANES, RW_W)),
                _const_spec((1, RW_W)), _const_spec((1, RW_W))]
    if not first:
        v1p = jnp.concatenate([w['rw_v1'][e - 1], jnp.zeros((RW_W, LANES - w['rw_v1'].shape[2]), F32)], axis=1)
        args += [vec(w['rw_v0'][e - 1]), v1p.astype(BF16), pad_rows(w['rw_v2'][e - 1]).astype(BF16), v_first]
        in_specs += [_const_spec((1, RW_W)), _const_spec((RW_W, LANES)), _const_spec((LANES, RW_W)), rowspec]
    outs = pl.pallas_call(
        functools.partial(_rw_prep_body, tb=tb, first=first),
        grid=(batch, nt), in_specs=in_specs,
        out_specs=[rowspec] * 7 + [pl.BlockSpec((1, 1, cols), lambda b, t: (b, 0, 0))],
        out_shape=[jax.ShapeDtypeStruct((n, RW_W), F32)] * 7 + [jax.ShapeDtypeStruct((batch, 1, cols), F32)],
        scratch_shapes=[pltpu.VMEM((8, cols), F32)],
        compiler_params=_cparams("arbitrary", "arbitrary"), name="rw_prep",
    )(*args)
    return outs[:7], outs[7].reshape(batch, cols)


def _pad_chunk(x, rows):
    if x.shape[0] == rows:
        return x
    return jnp.concatenate([x, jnp.zeros((rows - x.shape[0], x.shape[1]), x.dtype)], axis=0)


def _rw_scan_body(r_ref, lw_ref, k_ref, v_ref, a_ref, b_ref, g_ref, rk_ref, lnw_ref, lnb_ref, s0_ref, tri_ref,
                  o_ref, so_ref, s_ref, *, nb, tb):
    @pl.when(pl.program_id(1) == 0)
    def _():
        s_ref[...] = s0_ref[...]

    c = CHUNK
    lane = lax.broadcasted_iota(jnp.int32, (1, LANES), 1)
    m0 = lane < DH
    r2i = lax.broadcasted_iota(jnp.int32, (2 * c, 2 * c), 0)
    c2i = lax.broadcasted_iota(jnp.int32, (2 * c, 2 * c), 1)
    strict = (r2i % c) > (c2i % c)
    lower = (r2i % c) >= (c2i % c)
    tri = tri_ref[...]

    def stack2(z):
        return jnp.concatenate([jnp.where(m0, z, 0.0), jnp.where(m0, 0.0, z)], axis=0)

    chains = [(n, hp, slice(hp * LANES, (hp + 1) * LANES)) for n in range(nb) for hp in range(RW_W // LANES)]
    load = lambda ref: [_pad_chunk(ref[n, :, cs], c) for n, _, cs in chains]
    lws, rs, ks, vs, as_, bs = load(lw_ref), load(r_ref), load(k_ref), load(v_ref), load(a_ref), load(b_ref)
    cums = [_sel_l(tri, lw) for lw in lws]
    a2s = [stack2(a * jnp.exp(cum - lw)).astype(BF16) for a, cum, lw in zip(as_, cums, lws)]
    r2s = [stack2(r * jnp.exp(cum)).astype(BF16) for r, cum in zip(rs, cums)]
    b2s = [stack2(b * jnp.exp(-cum)).astype(BF16) for b, cum in zip(bs, cums)]
    k2s = [stack2(k * jnp.exp(-cum)).astype(BF16) for k, cum in zip(ks, cums)]
    v2s = [stack2(v) for v in vs]
    ars_in = [jnp.concatenate([a2, r2], axis=0) for a2, r2 in zip(a2s, r2s)]
    lms = [_mm_nt(ar, jnp.concatenate([b2, k2], axis=0)) for ar, b2, k2 in zip(ars_in, b2s, k2s)]
    xs = _inv_unit_lower([jnp.where(strict, lm[0:2 * c, 0:2 * c], 0.0) for lm in lms])
    e_ends = [jnp.exp(cum[c - 1:c, :] - cum) for cum in cums]
    xbs = [x.astype(BF16) for x in xs]
    qs = [_mm_tn(xb, stack2(b * e).astype(BF16)) for xb, b, e in zip(xbs, bs, e_ends)]
    wvs = [_mm(jnp.where(strict, lm[0:2 * c, 2 * c:4 * c], 0.0).astype(BF16), v2.astype(BF16))
           for lm, v2 in zip(lms, v2s)]
    ps = [_mm_tn(a2, q.astype(BF16)) for a2, q in zip(a2s, qs)]
    zs = [_mm_tn(jnp.concatenate([wv, v2], axis=0).astype(BF16),
                 jnp.concatenate([q, stack2(k * e)], axis=0).astype(BF16))
          for wv, v2, q, k, e in zip(wvs, v2s, qs, ks, e_ends)]
    sts = [s_ref[n, hp] for n, hp, _ in chains]
    sbs = [st.astype(BF16) for st in sts]
    for (n, hp, _), st, sb, cum, p, z in zip(chains, sts, sbs, cums, ps, zs):
        s_ref[n, hp] = st * jnp.exp(cum[c - 1:c, :]) + _mm(sb, p.astype(BF16)) + z
    arss = [_mm_nt(ar, sb) for ar, sb in zip(ars_in, sbs)]
    u2s = [_mm(xb, (ars[0:2 * c, :] + wv).astype(BF16)) for xb, ars, wv in zip(xbs, arss, wvs)]
    y2s = [ars[2 * c:4 * c, :]
           + _mm(jnp.concatenate([jnp.where(lower, lm[2 * c:4 * c, 0:2 * c], 0.0),
                                  jnp.where(lower, lm[2 * c:4 * c, 2 * c:4 * c], 0.0)], axis=1).astype(BF16),
                 jnp.concatenate([u2, v2], axis=0).astype(BF16))
           for ars, lm, u2, v2 in zip(arss, lms, u2s, v2s)]
    ys = [(y2[0:c, :] + y2[c:2 * c, :])[0:tb, :] for y2 in y2s]
    def head_sum(x):
        s0 = jnp.sum(jnp.where(m0, x, 0.0), axis=-1, keepdims=True)
        return jnp.where(m0, s0, jnp.sum(x, axis=-1, keepdims=True) - s0)

    mus = [head_sum(y) * (1.0 / DH) for y in ys]
    ds = [y - mu for y, mu in zip(ys, mus)]
    vars_ = [head_sum(d * d) * (1.0 / DH) for d in ds]
    bonus = [head_sum(r_ref[n, :, cs] * k_ref[n, :, cs] * rk_ref[:, cs]) for n, _, cs in chains]
    for (n, _, cs), d, var, bo in zip(chains, ds, vars_, bonus):
        yn = d * lax.rsqrt(var + RW_GN_EPS) * lnw_ref[:, cs] + lnb_ref[:, cs]
        o_ref[n, :, cs] = (yn + bo * v_ref[n, :, cs]) * g_ref[n, :, cs]
    @pl.when(pl.program_id(1) == pl.num_programs(1) - 1)
    def _():
        so_ref[...] = s_ref[...]


def _rw_scan(parts, w, e, s0_blk, batch, seq):
    nb = 2
    assert batch % nb == 0
    tb = min(CHUNK, seq)
    nt = seq // tb
    npair = RW_W // LANES
    parts = [x.reshape(batch, seq, RW_W) for x in parts]
    tri = jnp.asarray(np.tril(np.ones((CHUNK, CHUNK), np.float32)), BF16)
    rowspec = pl.BlockSpec((nb, tb, RW_W), lambda bb, t: (bb, t, 0))
    stspec = pl.BlockSpec((nb, npair, LANES, LANES), lambda bb, t: (bb, 0, 0, 0))
    cst = lambda shape: pl.BlockSpec(shape, lambda bb, t: (0,) * len(shape))
    o_b, s_out = pl.pallas_call(
        functools.partial(_rw_scan_body, nb=nb, tb=tb),
        grid=(batch // nb, nt),
        in_specs=[rowspec] * 7 + [cst((1, RW_W))] * 3 + [stspec, cst(tri.shape)],
        out_specs=[rowspec, stspec],
        out_shape=[jax.ShapeDtypeStruct((batch, seq, RW_W), F32), jax.ShapeDtypeStruct(s0_blk.shape, F32)],
        scratch_shapes=[pltpu.VMEM((nb, npair, LANES, LANES), F32)],
        compiler_params=_cparams("arbitrary", "arbitrary"), name="rw_scan",
    )(*parts, w['rw_r_k'][e].reshape(1, RW_W), w['rw_ln_w'][e].reshape(1, RW_W),
      w['rw_ln_b'][e].reshape(1, RW_W), s0_blk, tri)
    return o_b.reshape(batch * seq, RW_W), s_out


def _rw_state_to_blocks(s):
    bsz = s.shape[0]
    s = s.reshape(bsz, H_A // 2, 2, DH, DH)
    z = jnp.zeros_like(s[:, :, 0])
    top = jnp.concatenate([s[:, :, 0], z], axis=-1)
    bot = jnp.concatenate([z, s[:, :, 1]], axis=-1)
    return jnp.concatenate([top, bot], axis=-2)


def _rw_blocks_to_state(sb):
    bsz = sb.shape[0]
    return jnp.stack([sb[:, :, :DH, :DH], sb[:, :, DH:, DH:]], axis=2).reshape(bsz, H_A, DH, DH)


def _gdn_body(q_ref, k_ref, v_ref, z_ref, ba_ref, qi_ref, ki_ref, vi_ref, cwq_ref, cwk_ref, cwv_ref,
              par_ref, nw_ref, s0_ref, tri_ref, o_ref, so_ref, qc_ref, kc_ref, vc_ref,
              s_ref, carry_ref, *, nb, tb):
    @pl.when(pl.program_id(1) == 0)
    def _():
        carry_ref[:, 0, 5:8, :] = qi_ref[...]
        carry_ref[:, 1, 5:8, :] = ki_ref[...]
        carry_ref[:, 2, 5:8, :] = vi_ref[...]
        s_ref[...] = s0_ref[...]

    def conv(x, w_ref, n, idx, out_ref):
        prev = carry_ref[n, idx]
        y = x * w_ref[3:4, :]
        for kk in range(1, 4):
            y = y + _shift_rows(x, kk, prev) * w_ref[3 - kk:4 - kk, :]
        tail = x[tb - 3:tb, :]
        carry_ref[n, idx, 5:8, :] = tail
        out_ref[n] = tail
        return y * _sigmoid(y)

    lane = lax.broadcasted_iota(jnp.int32, (1, LANES), 1)

    def column(x, idx):
        return jnp.sum(jnp.where(lane == idx, x, 0.0), axis=-1, keepdims=True)

    c = CHUNK
    r2i = lax.broadcasted_iota(jnp.int32, (2 * c, 2 * c), 0)
    c2i = lax.broadcasted_iota(jnp.int32, (2 * c, 2 * c), 1)
    same = (r2i // c) == (c2i // c)
    strict = same & (r2i > c2i)
    lower = same & (r2i >= c2i)
    tri = tri_ref[...]

    stacked = []
    for n in range(nb):
        q = conv(q_ref[n], cwq_ref, n, 0, qc_ref)
        k = conv(k_ref[n], cwk_ref, n, 1, kc_ref)
        v = conv(v_ref[n], cwv_ref, n, 2, vc_ref)
        ba = ba_ref[n]
        beta_all = _sigmoid(ba)
        z_in = ba + par_ref[1:2, :]
        g_all = -jnp.exp(par_ref[0:1, :]) * (jnp.maximum(z_in, 0.0) + jnp.log1p(jnp.exp(-jnp.abs(z_in))))
        for hp in range(H_C // 2):
            heads = []
            for hh in range(2):
                h = 2 * hp + hh
                sl = slice(h * DK, (h + 1) * DK)
                qh = q[:, sl]
                kh = k[:, sl]
                qn = qh * lax.rsqrt(jnp.sum(qh * qh, axis=-1, keepdims=True) + L2_EPS) * (DK ** -0.5)
                kn = kh * lax.rsqrt(jnp.sum(kh * kh, axis=-1, keepdims=True) + L2_EPS)
                beta = jnp.broadcast_to(column(beta_all, h), (tb, DK))
                g = jnp.broadcast_to(column(g_all, H_C + h), (tb, DK))
                heads.append((qn, kn, v[:, sl], beta, g))
            stacked.append([jnp.concatenate([_pad_chunk(heads[0][i], c), _pad_chunk(heads[1][i], c)], axis=0)
                            for i in range(5)])
    chains = [(n, hp) for n in range(nb) for hp in range(H_C // 2)]
    qn2s, kn2s, v2s, b2s, g2s = [[s[i] for s in stacked] for i in range(5)]
    gcs = [_sel_l(tri, g2) for g2 in g2s]
    grows = [jnp.transpose(gc) for gc in gcs]
    gammas = [jnp.where(lower, jnp.exp(jnp.where(lower, gc - grow, 0.0)), 0.0) for gc, grow in zip(gcs, grows)]
    kbs = [kn2 * b2 for kn2, b2 in zip(kn2s, b2s)]
    kkqks = [_mm_nt(jnp.concatenate([kb, qn2], axis=0).astype(BF16), kn2.astype(BF16))
             for kb, qn2, kn2 in zip(kbs, qn2s, kn2s)]
    tinvs = _inv_unit_lower([jnp.where(strict, -kkqk[0:2 * c, :] * gamma, 0.0) for kkqk, gamma in zip(kkqks, gammas)])
    egs = [jnp.exp(gc) for gc in gcs]
    uws = [_mm(tinv.astype(BF16), jnp.concatenate([kb * eg, v2 * b2], axis=1).astype(BF16))
           for tinv, kb, eg, v2, b2 in zip(tinvs, kbs, egs, v2s, b2s)]
    heads2 = [(ci, hh) for ci in range(len(chains)) for hh in range(2)]
    rows = lambda hh: slice(hh * c, (hh + 1) * c)
    glasts = [gcs[ci][hh * c + c - 1:hh * c + c, :] for ci, hh in heads2]
    pzs = [_mm_tn((kn2s[ci][rows(hh), :] * jnp.exp(gl - gcs[ci][rows(hh), :])).astype(BF16),
                  uws[ci][rows(hh), :].astype(BF16))
           for (ci, hh), gl in zip(heads2, glasts)]
    sts = [s_ref[chains[ci][0], 2 * chains[ci][1] + hh] for ci, hh in heads2]
    sbs = [st.astype(BF16) for st in sts]
    for (ci, hh), st, sb, gl, pz in zip(heads2, sts, sbs, glasts, pzs):
        s_ref[chains[ci][0], 2 * chains[ci][1] + hh] = (st * jnp.exp(gl[:, 0:1])
                                                        - _mm(pz[:, 0:DK].astype(BF16), sb) + pz[:, DK:2 * DK])
    wqs = [_mm(jnp.concatenate([uws[ci][rows(hh), 0:DK], (qn2s[ci] * egs[ci])[rows(hh), :]], axis=0).astype(BF16), sb)
           for (ci, hh), sb in zip(heads2, sbs)]
    o2s = []
    for ci in range(len(chains)):
        vnew = jnp.concatenate([uws[ci][rows(hh), DK:2 * DK] - wqs[2 * ci + hh][0:c, :] for hh in range(2)], axis=0)
        qs = jnp.concatenate([wqs[2 * ci + hh][c:2 * c, :] for hh in range(2)], axis=0)
        amat = jnp.where(lower, kkqks[ci][2 * c:4 * c, :] * gammas[ci], 0.0)
        o2s.append(qs + _mm(amat.astype(BF16), vnew.astype(BF16)))
    for ci, (n, hp) in enumerate(chains):
        for hh in range(2):
            sl = slice((2 * hp + hh) * DK, (2 * hp + hh + 1) * DK)
            oh = o2s[ci][hh * c:hh * c + tb, :]
            zz = z_ref[n, :, sl]
            on = oh * lax.rsqrt(jnp.mean(oh * oh, axis=-1, keepdims=True) + EPS) * nw_ref[...]
            o_ref[n, :, sl] = on * (zz * _sigmoid(zz))
    @pl.when(pl.program_id(1) == pl.num_programs(1) - 1)
    def _():
        so_ref[...] = s_ref[...]


def _gdn(p_qkv, p_z, p_ba, conv_prev, s0, w, o_idx, batch, seq):
    nb = 2
    assert batch % nb == 0 and seq >= 3
    tb = min(CHUNK, seq)
    nt = seq // tb
    gk = H_C * DK
    cw = w['gdn_conv_w'][o_idx]
    par = jnp.concatenate([_pad_lanes(w['gdn_A_log'][o_idx], H_C), _pad_lanes(w['gdn_dt_bias'][o_idx], H_C),
                           jnp.zeros((6, LANES), F32)], axis=0)
    idx = np.arange(2 * CHUNK)
    tri = jnp.asarray((((idx[:, None] // CHUNK) == (idx[None, :] // CHUNK))
                       & (idx[:, None] >= idx[None, :])).astype(np.float32), BF16)
    p_qkv = p_qkv.reshape(batch, seq, 3 * gk)
    col = lambda part: pl.BlockSpec((nb, tb, gk), lambda b, t, part=part: (b, t, part))
    prev = lambda part: pl.BlockSpec((nb, 3, gk), lambda b, t, part=part: (b, 0, part))
    wspec = lambda part: pl.BlockSpec((4, gk), lambda b, t, part=part: (0, part))
    cst = lambda shape: pl.BlockSpec(shape, lambda b, t: (0,) * len(shape))
    stspec = pl.BlockSpec((nb, H_C, DK, DK), lambda b, t: (b, 0, 0, 0))
    outs = pl.pallas_call(
        functools.partial(_gdn_body, nb=nb, tb=tb),
        grid=(batch // nb, nt),
        in_specs=[col(0), col(1), col(2), col(0),
                  pl.BlockSpec((nb, tb, LANES), lambda b, t: (b, t, 0)),
                  prev(0), prev(1), prev(2), wspec(0), wspec(1), wspec(2),
                  cst((8, LANES)), cst((1, DK)), stspec, cst(tri.shape)],
        out_specs=[col(0), stspec, prev(0), prev(0), prev(0)],
        out_shape=[jax.ShapeDtypeStruct((batch, seq, gk), F32), jax.ShapeDtypeStruct(s0.shape, F32)]
                  + [jax.ShapeDtypeStruct((batch, 3, gk), F32)] * 3,
        scratch_shapes=[pltpu.VMEM((nb, H_C, DK, DK), F32), pltpu.VMEM((nb, 3, 8, gk), F32)],
        compiler_params=_cparams("arbitrary", "arbitrary"), name="gdn",
    )(p_qkv, p_qkv, p_qkv, p_z.reshape(batch, seq, gk), p_ba.reshape(batch, seq, LANES),
      conv_prev, conv_prev, conv_prev, cw, cw, cw, par, w['gdn_norm_w'][o_idx].reshape(1, DK), s0, tri)
    o_c, s_out, qc, kc, vc = outs
    return o_c.reshape(batch * seq, gk), jnp.concatenate([qc, kc, vc], axis=-1), s_out


def _run_trunk(x, fox_past, rw_state, rw_shift, gdn_state, gdn_conv, ffn_conv, page_table, w, wb):
    batch, seq, d = x.shape
    n = batch * seq
    depth = w['norm_mix'].shape[0]
    tm = 512 if n % 512 == 0 else n
    x = x.reshape(n, d)
    fk, fv, flf, rws, rwsh, gs, gcv, fcv = [], [], [], [], [], [], [], []
    v_first = None
    for layer in range(depth):
        if layer % 2 == 0:
            e = layer // 2
            p_fox, p_rw, p_fl = _norm_matmul(x, w['norm_mix'][layer], wb['ev_in'], e, (4 * FOX_W, 1792, LANES), tm)
            if fox_past is None:
                q_aug, k_aug, v_bf, k_out, lf = _fox_prep(p_fox, p_fl, w['fox_q_gain'][e], w['fox_k_gain'][e],
                                                          w['fox_b_f'][e], batch, seq)
                o_attn = _fox_attn(q_aug, k_aug, v_bf, batch, seq)
            else:
                o_attn, k_out, lf = _fox_decode(p_fox, p_fl, w['fox_q_gain'][e], w['fox_k_gain'][e], w['fox_b_f'][e],
                                                fox_past[0], fox_past[1], fox_past[2], page_table, e, batch, seq)
            parts, sh = _rw_prep(p_rw, rw_shift[e], w, e, v_first, batch, seq)
            if e == 0:
                v_first = parts[3]
            o_b, s_blk = _rw_scan(parts, w, e, _rw_state_to_blocks(rw_state[e]), batch, seq)
            x = _ev_out(o_attn, p_fox, o_b, wb['ev_out'], e, x, tm)
            fk.append(k_out.reshape(batch, seq, H_A, DH))
            fv.append(p_fox[:, 2 * FOX_W:3 * FOX_W].reshape(batch, seq, H_A, DH))
            flf.append(lf[:, :H_A].reshape(batch, seq, H_A))
            rws.append(_rw_blocks_to_state(s_blk))
            rwsh.append(sh)
        else:
            o_idx = layer // 2
            p_qkv, p_z, p_ba = _norm_matmul(x, w['norm_mix'][layer], wb['od_in'], o_idx,
                                            (3 * H_C * DK, H_C * DK, LANES), tm)
            o_c, cv, s_out = _gdn(p_qkv, p_z, p_ba, gdn_conv[o_idx], gdn_state[o_idx], w, o_idx, batch, seq)
            x = _od_out(o_c, wb['od_out'], o_idx, x, tm)
            gs.append(s_out)
            gcv.append(cv)
        x, buf = _ffn(x, w['norm_ffn'][layer], wb['ffn_up'], w['ffn_conv_w'][layer],
                      w['ffn_conv_b'][layer], wb['ffn_down'], layer, ffn_conv[layer], batch, seq)
        fcv.append(buf)
    y = _final_norm(x, w['norm_out'], tm).reshape(batch, seq, d)
    return (y, jnp.stack(fk), jnp.stack(fv), jnp.stack(flf), jnp.stack(rws), jnp.stack(rwsh),
            jnp.stack(gs), jnp.stack(gcv), jnp.stack(fcv))


def _prep_weights(w):
    fox_cols = 4 * FOX_W + H_A
    ev = w['ev_w_in']
    pad = jnp.zeros(ev.shape[:2] + (LANES - H_A,), ev.dtype)
    ev_in = jnp.concatenate([ev[..., :4 * FOX_W], ev[..., fox_cols:], ev[..., 4 * FOX_W:fox_cols], pad], axis=-1)
    od = w['od_w_in']
    pad2 = jnp.zeros(od.shape[:2] + (LANES - 2 * H_C,), od.dtype)
    od_in = jnp.concatenate([od, pad2], axis=-1)
    return dict(ev_in=ev_in.astype(BF16), ev_out=w['ev_w_out'].astype(BF16), od_in=od_in.astype(BF16),
                od_out=w['od_w_out'].astype(BF16), ffn_up=w['ffn_w_up'].astype(BF16),
                ffn_down=w['ffn_w_down'].astype(BF16))


def kernel(x_prompt, x_sample, cache_fox_k, cache_fox_v, cache_fox_logf, state_rwkv, state_rwkv_shift, state_gdn, state_gdn_conv, state_ffn_conv, page_table, norm_mix, norm_ffn, norm_out, ev_w_in, ev_w_out, fox_b_f, fox_q_gain, fox_k_gain, rw_mu, rw_w0, rw_w2, rw_a0, rw_a2, rw_g2, rw_k_k, rw_k_a, rw_r_k, rw_ln_w, rw_ln_b, rw_v0, rw_v1, rw_v2, od_w_in, od_w_out, gdn_conv_w, gdn_A_log, gdn_dt_bias, gdn_norm_w, ffn_w_up, ffn_conv_w, ffn_conv_b, ffn_w_down):
    w = dict(norm_mix=norm_mix, norm_ffn=norm_ffn, norm_out=norm_out, ev_w_in=ev_w_in, ev_w_out=ev_w_out,
             fox_b_f=fox_b_f, fox_q_gain=fox_q_gain, fox_k_gain=fox_k_gain, rw_mu=rw_mu, rw_w0=rw_w0,
             rw_w2=rw_w2, rw_a0=rw_a0, rw_a2=rw_a2, rw_g2=rw_g2, rw_k_k=rw_k_k, rw_k_a=rw_k_a, rw_r_k=rw_r_k,
             rw_ln_w=rw_ln_w, rw_ln_b=rw_ln_b, rw_v0=rw_v0, rw_v1=rw_v1, rw_v2=rw_v2, od_w_in=od_w_in,
             od_w_out=od_w_out, gdn_conv_w=gdn_conv_w, gdn_A_log=gdn_A_log, gdn_dt_bias=gdn_dt_bias,
             gdn_norm_w=gdn_norm_w, ffn_w_up=ffn_w_up, ffn_conv_w=ffn_conv_w, ffn_conv_b=ffn_conv_b,
             ffn_w_down=ffn_w_down)
    wb = _prep_weights(w)
    bp = x_prompt.shape[0]
    n_even, n_odd, depth = ev_w_in.shape[0], od_w_in.shape[0], norm_mix.shape[0]
    rw_cols = rw_mu.shape[1]
    (y_p, fk_p, fv_p, flf_p, rw_p, rwsh_p, gdn_p, gcv_p, fcv_p) = _run_trunk(
        x_prompt, None,
        jnp.zeros((n_even, bp, H_A, DH, DH), F32), jnp.zeros((n_even, bp, rw_cols), F32),
        jnp.zeros((n_odd, bp, H_C, DK, DK), F32), jnp.zeros((n_odd, bp, 3, 3 * H_C * DK), F32),
        jnp.zeros((depth, bp, 2, ffn_w_up.shape[2]), F32), page_table, w, wb)
    (y_s, fk_s, fv_s, flf_s, rw_s, rwsh_s, gdn_s, gcv_s, fcv_s) = _run_trunk(
        x_sample, (cache_fox_k, cache_fox_v, cache_fox_logf), state_rwkv, state_rwkv_shift,
        state_gdn, state_gdn_conv, state_ffn_conv, page_table, w, wb)
    return (y_p, y_s, fk_p, fv_p, flf_p, fk_s, fv_s, flf_s, rw_p, rw_s, rwsh_p, rwsh_s,
            gdn_p, gdn_s, gcv_p, gcv_s, fcv_p, fcv_s)
```

```python
import functools
import math

import jax
import jax.numpy as jnp
import numpy as np
from jax import lax
from jax.experimental import pallas as pl
from jax.experimental.pallas import tpu as pltpu

F32 = jnp.float32
BF16 = jnp.bfloat16

EPS = 1e-6
RW_GN_EPS = 64e-5
L2_EPS = 1e-6
NEG_BIG = -1e30

H_A = 8
DH = 64
FOX_W = 512
RW_W = 512
H_C = 8
DK = 128
CHUNK = 64
LANES = 128
VMEM_LIMIT = 56 * 1024 * 1024


def _cparams(*sem):
    return pltpu.CompilerParams(dimension_semantics=sem, vmem_limit_bytes=VMEM_LIMIT)


def _const_spec(shape):
    nd = len(shape)
    return pl.BlockSpec(shape, lambda *_: (0,) * nd, pipeline_mode=pl.Buffered(1))


def _mix_dtype(rows):
    return BF16 if rows % 16 == 0 else F32


def _layer_spec(stack, layer):
    return pl.BlockSpec((None,) + stack.shape[1:], lambda *_: (layer, 0, 0), pipeline_mode=pl.Buffered(1))


def _mm(a, b):
    return jnp.dot(a, b, preferred_element_type=F32)


def _mm_nt(a, b):
    return lax.dot_general(a, b, (((1,), (1,)), ((), ())), preferred_element_type=F32)


def _mm_tn(a, b):
    return lax.dot_general(a, b, (((0,), (0,)), ((), ())), preferred_element_type=F32)


def _split3(x):
    hi = x.astype(BF16)
    r = x - hi.astype(F32)
    mid = r.astype(BF16)
    lo = (r - mid.astype(F32)).astype(BF16)
    return hi, mid, lo


def _sel_l(m01, x):
    hi, mid, lo = _split3(x)
    return _mm(m01, hi) + _mm(m01, mid) + _mm(m01, lo)


def _sel_r(x, m01):
    hi, mid, lo = _split3(x)
    return _mm(hi, m01) + _mm(mid, m01) + _mm(lo, m01)


def _sel_nt(m01, x):
    hi, mid, lo = _split3(x)
    return _mm_nt(m01, hi) + _mm_nt(m01, mid) + _mm_nt(m01, lo)


def _log_sigmoid(z):
    return jnp.minimum(z, 0.0) - jnp.log1p(jnp.exp(-jnp.abs(z)))


def _sigmoid(z):
    return 1.0 / (1.0 + jnp.exp(-z))


def _inv_unit_lower(ns):
    size = ns[0].shape[0]
    r = lax.broadcasted_iota(jnp.int32, (size, size), 0)
    c = lax.broadcasted_iota(jnp.int32, (size, size), 1)
    eye = jnp.where(r == c, 1.0, 0.0)
    ps = [eye + n for n in ns]
    nks = list(ns)
    for _ in range(int(math.log2(CHUNK)) - 1):
        nks = [_mm(nk.astype(BF16), nk.astype(BF16)) for nk in nks]
        ps = [p + _mm(p.astype(BF16), nk.astype(BF16)) for p, nk in zip(ps, nks)]
    return ps


def _head_scale(x2, fn):
    lane_h = lax.broadcasted_iota(jnp.int32, (1, x2.shape[1]), 1) // DH
    out = jnp.zeros_like(x2)
    for h in range(x2.shape[1] // DH):
        s = jnp.sum(x2[:, h * DH:(h + 1) * DH], axis=-1, keepdims=True)
        out = jnp.where(lane_h == h, fn(s), out)
    return out


def _rms_heads(x, gain):
    return x * _head_scale(x * x, lambda s: lax.rsqrt(s * (1.0 / DH) + EPS)) * gain


def _shift_rows(x, k, prev):
    rolled = pltpu.roll(x, k, 0)
    sub = 8
    row = lax.broadcasted_iota(jnp.int32, (sub, x.shape[1]), 0)
    head = rolled[0:sub, :]
    nprev = prev.shape[0]
    for j in range(k):
        head = jnp.where(row == j, prev[nprev - k + j:nprev - k + j + 1, :], head)
    return head if x.shape[0] == sub else jnp.concatenate([head, rolled[sub:, :]], axis=0)


def _nm_body(x_ref, g_ref, w_ref, *o_refs, splits):
    x = x_ref[...]
    xn = (x * lax.rsqrt(jnp.mean(x * x, axis=-1, keepdims=True) + EPS) * g_ref[...]).astype(BF16)
    off = 0
    for o_ref, n in zip(o_refs, splits):
        o_ref[...] = _mm(xn, w_ref[:, off:off + n])
        off += n


def _norm_matmul(x, g, w_stack, layer, splits, tm):
    n, d = x.shape
    ntot = w_stack.shape[2]
    assert sum(splits) == ntot and n % tm == 0
    return pl.pallas_call(
        functools.partial(_nm_body, splits=tuple(splits)),
        grid=(n // tm,),
        in_specs=[pl.BlockSpec((tm, d), lambda i: (i, 0)), _const_spec((1, d)), _layer_spec(w_stack, layer)],
        out_specs=[pl.BlockSpec((tm, s), lambda i: (i, 0)) for s in splits],
        out_shape=[jax.ShapeDtypeStruct((n, s), F32) for s in splits],
        compiler_params=_cparams("arbitrary"),
        name="norm_matmul",
    )(x, g.reshape(1, d), w_stack)


def _final_norm_body(x_ref, g_ref, o_ref):
    x = x_ref[...]
    o_ref[...] = x * lax.rsqrt(jnp.mean(x * x, axis=-1, keepdims=True) + EPS) * g_ref[...]


def _final_norm(x, g, tm):
    n, d = x.shape
    return pl.pallas_call(
        _final_norm_body, grid=(n // tm,),
        in_specs=[pl.BlockSpec((tm, d), lambda i: (i, 0)), _const_spec((1, d))],
        out_specs=pl.BlockSpec((tm, d), lambda i: (i, 0)),
        out_shape=jax.ShapeDtypeStruct((n, d), F32),
        compiler_params=_cparams("arbitrary"), name="final_norm",
    )(x, g.reshape(1, d))


def _evout_body(oa_ref, og_ref, ob_ref, w_ref, x_ref, o_ref):
    a = (oa_ref[...] * _sigmoid(og_ref[...])).astype(BF16)
    b = ob_ref[...].astype(BF16)
    o_ref[...] = x_ref[...] + _mm(a, w_ref[0:FOX_W, :]) + _mm(b, w_ref[FOX_W:FOX_W + RW_W, :])


def _ev_out(o_attn, p_fox, o_b, w_stack, layer, x, tm):
    n, d = x.shape
    return pl.pallas_call(
        _evout_body, grid=(n // tm,),
        in_specs=[pl.BlockSpec((tm, FOX_W), lambda i: (i, 0)),
                  pl.BlockSpec((tm, FOX_W), lambda i: (i, 3)),
                  pl.BlockSpec((tm, RW_W), lambda i: (i, 0)),
                  _layer_spec(w_stack, layer),
                  pl.BlockSpec((tm, d), lambda i: (i, 0))],
        out_specs=pl.BlockSpec((tm, d), lambda i: (i, 0)),
        out_shape=jax.ShapeDtypeStruct((n, d), F32),
        compiler_params=_cparams("arbitrary"), name="ev_out",
    )(o_attn, p_fox, o_b, w_stack, x)


def _odout_body(oc_ref, w_ref, x_ref, o_ref):
    o_ref[...] = x_ref[...] + _mm(oc_ref[...].astype(BF16), w_ref[...])


def _od_out(o_c, w_stack, layer, x, tm):
    n, d = x.shape
    k = o_c.shape[1]
    return pl.pallas_call(
        _odout_body, grid=(n // tm,),
        in_specs=[pl.BlockSpec((tm, k), lambda i: (i, 0)), _layer_spec(w_stack, layer),
                  pl.BlockSpec((tm, d), lambda i: (i, 0))],
        out_specs=pl.BlockSpec((tm, d), lambda i: (i, 0)),
        out_shape=jax.ShapeDtypeStruct((n, d), F32),
        compiler_params=_cparams("arbitrary"), name="od_out",
    )(o_c, w_stack, x)


def _ffn_cols(xn, wup_ref, cw_ref, cb_ref, wdn_ref, acc, prev_fn, tail_fn, f, cwb):
    for c in range(f // cwb):
        ys = []
        for half in (0, 1):
            lo = half * f + c * cwb
            h = _mm(xn, wup_ref[:, lo:lo + cwb])
            hm1, hm2 = prev_fn(h, lo)
            ys.append(hm2 * cw_ref[0:1, lo:lo + cwb] + hm1 * cw_ref[1:2, lo:lo + cwb]
                      + h * cw_ref[2:3, lo:lo + cwb] + cb_ref[:, lo:lo + cwb])
            tail_fn(h, lo)
        u, gt = ys
        act = (gt * _sigmoid(gt) * u).astype(BF16)
        acc = acc + _mm(act, wdn_ref[c * cwb:(c + 1) * cwb, :])
    return acc


def _ffn_norm(x_ref, g_ref):
    x = x_ref[...]
    return x, (x * lax.rsqrt(jnp.mean(x * x, axis=-1, keepdims=True) + EPS) * g_ref[...]).astype(BF16)


def _ffn_seq_body(x_ref, g_ref, wup_ref, cw_ref, cb_ref, wdn_ref, init_ref, o_ref, st_ref, carry_ref,
                  *, tb, f, cwb):
    @pl.when(pl.program_id(1) == 0)
    def _():
        carry_ref[0:2, :] = init_ref[0]

    x, xn = _ffn_norm(x_ref, g_ref)

    def prev_fn(h, lo):
        prev = carry_ref[0:2, lo:lo + cwb]
        return _shift_rows(h, 1, prev), _shift_rows(h, 2, prev)

    def tail_fn(h, lo):
        carry_ref[0:2, lo:lo + cwb] = h[tb - 2:tb, :]
        st_ref[0, :, lo:lo + cwb] = h[tb - 2:tb, :]

    o_ref[...] = _ffn_cols(xn, wup_ref, cw_ref, cb_ref, wdn_ref, x, prev_fn, tail_fn, f, cwb)


def _ffn_flat_body(x_ref, g_ref, wup_ref, cw_ref, cb_ref, wdn_ref, f1_ref, f2_ref, o_ref, st_ref,
                   *, rows, seq, f, cwb):
    x, xn = _ffn_norm(x_ref, g_ref)
    tmod = lax.broadcasted_iota(jnp.int32, (rows, cwb), 0) % seq

    def prev_fn(h, lo):
        hm1 = jnp.where(tmod == 0, f1_ref[:, lo:lo + cwb], pltpu.roll(h, 1, 0))
        hm2 = jnp.where(tmod < 2, f2_ref[:, lo:lo + cwb], pltpu.roll(h, 2, 0))
        return hm1, hm2

    def tail_fn(h, lo):
        st_ref[:, :, lo:lo + cwb] = h.reshape(rows // seq, seq, cwb)[:, seq - 2:seq, :]

    o_ref[...] = _ffn_cols(xn, wup_ref, cw_ref, cb_ref, wdn_ref, x, prev_fn, tail_fn, f, cwb)


def _ffn(x, g, wup_stack, conv_w, conv_b, wdn_stack, layer, conv_prev, batch, seq):
    n, d = x.shape
    f2 = wup_stack.shape[2]
    f = f2 // 2
    cwb = f
    weights = [_const_spec((1, d)), _layer_spec(wup_stack, layer), _const_spec((3, f2)), _const_spec((1, f2)),
               _layer_spec(wdn_stack, layer)]
    out_shape = [jax.ShapeDtypeStruct((n, d), F32), jax.ShapeDtypeStruct((batch, 2, f2), F32)]
    args = (x, g.reshape(1, d), wup_stack, conv_w, conv_b.reshape(1, f2), wdn_stack)
    if seq >= 256:
        tb = 512 if seq % 512 == 0 else 256
        nt = seq // tb
        return pl.pallas_call(
            functools.partial(_ffn_seq_body, tb=tb, f=f, cwb=cwb),
            grid=(batch, nt),
            in_specs=[pl.BlockSpec((tb, d), lambda b, t: (b * nt + t, 0))] + weights
                     + [pl.BlockSpec((1, 2, f2), lambda b, t: (b, 0, 0))],
            out_specs=[pl.BlockSpec((tb, d), lambda b, t: (b * nt + t, 0)),
                       pl.BlockSpec((1, 2, f2), lambda b, t: (b, 0, 0))],
            out_shape=out_shape,
            scratch_shapes=[pltpu.VMEM((8, f2), F32)],
            compiler_params=_cparams("arbitrary", "arbitrary"), name="ffn_seq",
        )(*args, conv_prev)
    zeros = jnp.zeros((batch, seq - 2, f2), F32)
    fill2 = jnp.concatenate([conv_prev, zeros], axis=1).reshape(n, f2)
    fill1 = jnp.concatenate([conv_prev[:, 1:2], zeros, zeros[:, :1]], axis=1).reshape(n, f2)
    return pl.pallas_call(
        functools.partial(_ffn_flat_body, rows=n, seq=seq, f=f, cwb=cwb),
        grid=(1,),
        in_specs=[pl.BlockSpec((n, d), lambda i: (0, 0))] + weights
                 + [pl.BlockSpec((n, f2), lambda i: (0, 0)), pl.BlockSpec((n, f2), lambda i: (0, 0))],
        out_specs=[pl.BlockSpec((n, d), lambda i: (0, 0)), pl.BlockSpec((batch, 2, f2), lambda i: (0, 0, 0))],
        out_shape=out_shape,
        compiler_params=_cparams("arbitrary"), name="ffn_flat",
    )(*args, fill1, fill2)


def _fox_consts():
    src = np.arange(FOX_W)
    place = np.zeros((FOX_W, H_A * LANES), np.float32)
    place[src, (src // DH) * LANES + src % DH] = 1.0
    hh = np.arange(H_A)
    eq = np.zeros((3, LANES, H_A * LANES), np.float32)
    ek = np.zeros((3, LANES, H_A * LANES), np.float32)
    cq = np.zeros((1, H_A * LANES), np.float32)
    ck = np.zeros((1, H_A * LANES), np.float32)
    for piece in range(3):
        eq[piece, hh, hh * LANES + DH + piece] = 1.0
        ek[piece, hh, hh * LANES + DH + 3 + piece] = -1.0
        cq[0, hh * LANES + DH + 3 + piece] = 1.0
        ck[0, hh * LANES + DH + piece] = 1.0
    as_bf = lambda a: jnp.asarray(a, BF16)
    return as_bf(place), as_bf(place.T), as_bf(eq), as_bf(ek), jnp.asarray(cq), jnp.asarray(ck)


def _fox_prep_body(q_ref, k_ref, v_ref, fl_ref, qg_ref, kg_ref, bf_ref, pm_ref, pt_ref, eq_ref, ek_ref, cq_ref,
                   ck_ref, vone_ref, tri_ref, bd_ref, qa_ref, ka_ref, vt_ref, ko_ref, lf_ref, carry_ref, *, tm):
    @pl.when(pl.program_id(1) == 0)
    def _():
        carry_ref[...] = jnp.zeros_like(carry_ref)

    bd = bd_ref[...]

    def rms_heads(x, gain):
        x2 = x * x
        hi = x2.astype(BF16)
        mid = (x2 - hi.astype(F32)).astype(BF16)
        ms = (_mm(hi, bd) + _mm(mid, bd)) * (1.0 / DH)
        return x * lax.rsqrt(ms + EPS) * gain

    qn = rms_heads(q_ref[...], qg_ref[...]) * (DH ** -0.5)
    kn = rms_heads(k_ref[...], kg_ref[...])
    lf = _log_sigmoid(fl_ref[...] + bf_ref[...])
    lf_ref[...] = lf
    fcum = _sel_l(tri_ref[...], lf) + carry_ref[0:1, :]
    carry_ref[0:1, :] = fcum[tm - 1:tm, :]
    fh, fm, flo = _split3(fcum)
    pm = pm_ref[...]
    qa = (_mm(qn.astype(BF16), pm) + _mm(fh, eq_ref[0]) + _mm(fm, eq_ref[1]) + _mm(flo, eq_ref[2])
          + cq_ref[...])
    ka = (_mm(kn.astype(BF16), pm) + _mm(fh, ek_ref[0]) + _mm(fm, ek_ref[1]) + _mm(flo, ek_ref[2])
          + ck_ref[...])
    qa_ref[...] = qa.astype(BF16)
    ka_ref[...] = ka.astype(BF16)
    vt = _mm_nt(pt_ref[...], v_ref[...].astype(BF16)) + vone_ref[...]
    vt_ref[0, :, 0] = vt.astype(BF16).reshape(H_A, LANES, tm)
    ko_ref[...] = kn


FOX_TILE = 512
FOX_QSUB = 4


def _fox_tile(seq):
    return min(FOX_TILE, seq // FOX_QSUB)


def _pad_lanes(v, offset=0):
    return jnp.concatenate([jnp.zeros((offset,), F32), v.astype(F32),
                            jnp.zeros((LANES - offset - v.shape[0],), F32)]).reshape(1, LANES)


def _fox_prep(p_fox, p_fl, q_gain, k_gain, b_f, batch, seq):
    n = p_fox.shape[0]
    tm = _fox_tile(seq)
    nt = seq // tm
    place, place_t, eq, ek, cq, ck = _fox_consts()
    vone = np.zeros((H_A * LANES, 1), np.float32)
    vone[np.arange(H_A) * LANES + DH, 0] = 1.0
    vone = jnp.asarray(vone)
    tri = jnp.asarray(np.tril(np.ones((tm, tm), np.float32)), BF16)
    lane = np.arange(FOX_W)
    bd = jnp.asarray((lane[:, None] // DH == lane[None, :] // DH).astype(np.float32), BF16)
    bf = _pad_lanes(b_f)
    row = lambda c: pl.BlockSpec((tm, FOX_W), lambda b, t, c=c: (b * nt + t, c))
    wide = pl.BlockSpec((tm, H_A * LANES), lambda b, t: (b * nt + t, 0))
    narrow = pl.BlockSpec((tm, LANES), lambda b, t: (b * nt + t, 0))
    return pl.pallas_call(
        functools.partial(_fox_prep_body, tm=tm),
        grid=(batch, nt),
        in_specs=[row(0), row(1), row(2), narrow,
                  _const_spec((1, FOX_W)), _const_spec((1, FOX_W)), _const_spec((1, LANES)),
                  _const_spec(place.shape), _const_spec(place_t.shape), _const_spec(eq.shape), _const_spec(ek.shape),
                  _const_spec(cq.shape), _const_spec(ck.shape), _const_spec(vone.shape), _const_spec(tri.shape),
                  _const_spec(bd.shape)],
        out_specs=[wide, wide, pl.BlockSpec((1, H_A, 1, LANES, tm), lambda b, t: (b, 0, t, 0, 0)), row(0), narrow],
        out_shape=[jax.ShapeDtypeStruct((n, H_A * LANES), BF16), jax.ShapeDtypeStruct((n, H_A * LANES), BF16),
                   jax.ShapeDtypeStruct((batch, H_A, nt, LANES, tm), BF16), jax.ShapeDtypeStruct((n, FOX_W), F32),
                   jax.ShapeDtypeStruct((n, LANES), F32)],
        scratch_shapes=[pltpu.VMEM((8, LANES), F32)],
        compiler_params=_cparams("arbitrary", "arbitrary"), name="fox_prep",
    )(p_fox, p_fox, p_fox, p_fl, jnp.tile(q_gain, H_A).reshape(1, FOX_W),
      jnp.tile(k_gain, H_A).reshape(1, FOX_W), bf, place, place_t, eq, ek, cq, ck, vone, tri, bd)


def _fox_attn_body(qa_ref, ka_ref, vt_ref, o_ref, *, tq):
    i = pl.program_id(2)
    key = lax.broadcasted_iota(jnp.int32, (tq, tq), 0)
    qry = lax.broadcasted_iota(jnp.int32, (tq, tq), 1)
    causal = key <= qry
    nsb = FOX_QSUB
    chains = [(sb, hh) for sb in range(nsb) for hh in range(2)]
    qs = [qa_ref[sb * tq:(sb + 1) * tq, hh * LANES:(hh + 1) * LANES] for sb, hh in chains]

    def block(j, carry, active):
        off = pl.multiple_of(j * tq, tq)
        ks = [ka_ref[pl.ds(off, tq), hh * LANES:(hh + 1) * LANES] for hh in range(2)]
        vts = [vt_ref[0, hh, j] for hh in range(2)]
        sts = {c: _mm_nt(ks[chains[c][1]], qs[c]) for c in active}
        sts = {c: jnp.where(causal, st, NEG_BIG) if active[c] else st for c, st in sts.items()}
        m2s = {c: jnp.maximum(carry[c][0], jnp.max(st, axis=0, keepdims=True)) for c, st in sts.items()}
        ps = {c: jnp.exp(st - m2s[c]).astype(BF16) for c, st in sts.items()}
        return tuple((m2s[c], jnp.exp(carry[c][0] - m2s[c]) * carry[c][1] + _mm(vts[chains[c][1]], ps[c]))
                     if c in active else carry[c] for c in range(len(chains)))

    init = tuple((jnp.full((1, tq), NEG_BIG, F32), jnp.zeros((LANES, tq), F32)) for _ in chains)
    carry = lax.fori_loop(0, nsb * i, lambda j, c: block(j, c, {c_: False for c_ in range(len(chains))}), init)
    for d in range(nsb):
        carry = block(nsb * i + d, carry, {c: chains[c][0] == d for c in range(len(chains)) if chains[c][0] >= d})
    for sb in range(nsb):
        halves = [acc[0:DH, :] / acc[DH:DH + 1, :] for _, acc in carry[2 * sb:2 * sb + 2]]
        o_ref[sb * tq:(sb + 1) * tq, :] = jnp.transpose(jnp.concatenate(halves, axis=0))


def _fox_attn(q_aug, k_aug, v_t, batch, seq):
    n = q_aug.shape[0]
    tq = _fox_tile(seq)
    rows = FOX_QSUB * tq
    nq = seq // rows
    return pl.pallas_call(
        functools.partial(_fox_attn_body, tq=tq),
        grid=(batch, H_A // 2, nq),
        in_specs=[pl.BlockSpec((rows, 2 * LANES), lambda b, hp, i: (b * nq + i, hp)),
                  pl.BlockSpec((seq, 2 * LANES), lambda b, hp, i: (b, hp)),
                  pl.BlockSpec((1, 2, seq // tq, LANES, tq), lambda b, hp, i: (b, hp, 0, 0, 0))],
        out_specs=pl.BlockSpec((rows, LANES), lambda b, hp, i: (b * nq + i, hp)),
        out_shape=jax.ShapeDtypeStruct((n, FOX_W), F32),
        compiler_params=_cparams("arbitrary", "arbitrary", "arbitrary"), name="fox_attn",
    )(q_aug, k_aug, v_t)


def _fox_decode_body(pt_ref, q_ref, k_ref, v_ref, fl_ref, qg_ref, kg_ref, bf_ref, ms_ref, ps_ref, pa_ref,
                     asel_ref, *rest, nb, pps, seq):
    del pt_ref
    npg = nb * pps
    kp, vp, lp = rest[:npg], rest[npg:2 * npg], rest[2 * npg:3 * npg]
    o_ref, ko_ref, lfo_ref = rest[3 * npg:3 * npg + 3]
    qh_ref, cb_ref, m_ref, l_ref, acc_ref, car_ref = rest[3 * npg + 3:]
    j = pl.program_id(1)
    nrow = H_A * seq

    @pl.when(j == 0)
    def _new_tokens():
        q_all = _rms_heads(q_ref[...], qg_ref[...]) * (DH ** -0.5)
        k_all = _rms_heads(k_ref[...], kg_ref[...])
        ko_ref[...] = k_all
        lf_all = _log_sigmoid(fl_ref[...] + bf_ref[...])
        lfo_ref[...] = lf_all
        row = lax.broadcasted_iota(jnp.int32, (seq, LANES), 0)
        zpad = jnp.zeros((LANES - seq, DH), F32)
        keyi = lax.broadcasted_iota(jnp.int32, (nrow, LANES), 1)
        ti = lax.broadcasted_iota(jnp.int32, (nrow, LANES), 0) % seq
        for n in range(nb):
            rs = slice(n * seq, (n + 1) * seq)
            q, k, v = q_all[rs, :], k_all[rs, :], v_ref[rs, :]
            c = lf_all[rs, :]
            s = 1
            while s < seq:
                c = c + jnp.where(row >= s, pltpu.roll(c, s, 0), 0.0)
                s *= 2
            srows = []
            for h in range(H_A):
                qh = q[:, h * DH:(h + 1) * DH]
                qh_ref[n, h] = qh
                cb_ref[n, h * seq:(h + 1) * seq, :] = jnp.broadcast_to(c[:, h:h + 1], (seq, LANES))
                kpad = jnp.concatenate([k[:, h * DH:(h + 1) * DH], zpad], axis=0).astype(BF16)
                srows.append(_mm_nt(qh.astype(BF16), kpad))
            cneg = jnp.concatenate([-c, jnp.zeros((LANES - seq, LANES), F32)], axis=0)
            s_new = jnp.concatenate(srows, axis=0) + cb_ref[n] + _sel_nt(asel_ref[0:nrow, :], cneg)
            s_new = jnp.where(keyi <= ti, s_new, NEG_BIG)
            m = jnp.max(s_new, axis=-1, keepdims=True)
            p = jnp.exp(s_new - m)
            m_ref[n] = m
            l_ref[n] = jnp.sum(p, axis=-1, keepdims=True)
            accs = []
            for h in range(H_A):
                vpad = jnp.concatenate([v[:, h * DH:(h + 1) * DH], zpad], axis=0).astype(BF16)
                accs.append(_mm(p[h * seq:(h + 1) * seq, :].astype(BF16), vpad))
            acc_ref[n] = jnp.concatenate(accs, axis=0)
        car_ref[...] = jnp.zeros_like(car_ref)

    seqs = range(nb)
    lfts = [jnp.concatenate([lp[n * pps + i][...] for i in range(pps)]
                            + [jnp.zeros((LANES - pps * H_A, LANES), F32)], axis=0) for n in seqs]
    tots = [jnp.broadcast_to(jnp.sum(lft, axis=1, keepdims=True), (LANES, LANES)) for lft in lfts]
    scores = [[jnp.concatenate([_mm(qh_ref[n, h].astype(BF16), kp[n * pps + i][h].astype(BF16))
                                for h in range(H_A)], axis=0) for i in range(pps)] for n in seqs]
    cars = [car_ref[n] for n in seqs]
    rfulls = [_sel_r(lft, ms_ref[...]) + _sel_l(ps_ref[...], tot) + car for lft, tot, car in zip(lfts, tots, cars)]
    for n in seqs:
        car_ref[n] = cars[n] + _sel_l(pa_ref[...], tots[n])
    biases = [_sel_l(asel_ref[...], rfull) for rfull in rfulls]
    s_alls = [jnp.concatenate([scores[n][i] + biases[n][i * nrow:(i + 1) * nrow, :] + cb_ref[n]
                               for i in range(pps)], axis=1) for n in seqs]
    m_olds = [m_ref[n] for n in seqs]
    m_news = [jnp.maximum(m_old, jnp.max(s_all, axis=-1, keepdims=True)) for m_old, s_all in zip(m_olds, s_alls)]
    probs = [jnp.exp(s_all - m_new) for s_all, m_new in zip(s_alls, m_news)]
    alphas = [jnp.exp(m_old - m_new) for m_old, m_new in zip(m_olds, m_news)]
    pvs = [jnp.concatenate(
        [_mm_nt(probs[n][h * seq:(h + 1) * seq, :].astype(BF16),
                jnp.concatenate([vp[n * pps + i][h] for i in range(pps)], axis=1).astype(BF16))
         for h in range(H_A)], axis=0) for n in seqs]
    for n in seqs:
        m_ref[n] = m_news[n]
        l_ref[n] = alphas[n] * l_ref[n] + jnp.sum(probs[n], axis=-1, keepdims=True)
        acc_ref[n] = alphas[n] * acc_ref[n] + pvs[n]

    @pl.when(j == pl.num_programs(1) - 1)
    def _():
        for n in seqs:
            o = acc_ref[n] / l_ref[n]
            for h in range(H_A):
                o_ref[n * seq:(n + 1) * seq, h * DH:(h + 1) * DH] = o[h * seq:(h + 1) * seq, :]


def _fox_decode(p_fox, p_fl, q_gain, k_gain, b_f, pool_k, pool_v, pool_lf, page_table, e, batch, seq):
    n = p_fox.shape[0]
    n_pages = page_table.shape[1]
    page = pool_lf.shape[2]
    nb = 4
    assert page == LANES and seq == 8 and batch % nb == 0
    pps = 8
    while n_pages % pps:
        pps //= 2
    groups = n_pages // pps
    pk = jnp.transpose(pool_k, (0, 1, 3, 4, 2))
    pv = jnp.transpose(pool_v, (0, 1, 3, 4, 2))
    plf = jnp.transpose(pool_lf, (0, 1, 3, 2))
    nrow = H_A * seq
    idx = np.arange(LANES)
    as_bf = lambda a: jnp.asarray(a.astype(np.float32), BF16)
    ms = as_bf(idx[:, None] > idx[None, :])
    same_h = (idx[:, None] % H_A) == (idx[None, :] % H_A)
    valid = (idx[:, None] < pps * H_A) & (idx[None, :] < pps * H_A)
    ps = as_bf(same_h & valid & (idx[None, :] // H_A < idx[:, None] // H_A))
    pa = as_bf(same_h & valid)
    r = np.arange(pps * nrow)
    asel = np.zeros((pps * nrow, LANES), np.float32)
    asel[r, (r // nrow) * H_A + (r % nrow) // seq] = 1.0
    asel = as_bf(asel)
    bf = _pad_lanes(b_f)

    def page_spec(s, i, shape):
        def index(b, j, pt):
            return (e, pt[b * nb + s, n_pages - 1 - (j * pps + i)]) + (0,) * len(shape)
        return pl.BlockSpec((None, None) + shape, index)

    pages = lambda shape: [page_spec(s, i, shape) for s in range(nb) for i in range(pps)]
    rowspec = lambda c: pl.BlockSpec((nb * seq, FOX_W), lambda b, j, pt, c=c: (b, c))
    narrow = pl.BlockSpec((nb * seq, LANES), lambda b, j, pt: (b, 0))
    cst = lambda shape: pl.BlockSpec(shape, lambda b, j, pt: (0,) * len(shape))
    in_specs = ([rowspec(0), rowspec(1), rowspec(2), narrow,
                 cst((1, FOX_W)), cst((1, FOX_W)), cst((1, LANES)), cst(ms.shape), cst(ps.shape), cst(pa.shape),
                 cst(asel.shape)]
                + pages((H_A, DH, page)) + pages((H_A, DH, page)) + pages((H_A, page)))
    grid_spec = pltpu.PrefetchScalarGridSpec(
        num_scalar_prefetch=1, grid=(batch // nb, groups), in_specs=in_specs,
        out_specs=[rowspec(0), rowspec(0), narrow],
        scratch_shapes=[pltpu.VMEM((nb, H_A, seq, DH), F32), pltpu.VMEM((nb, nrow, LANES), F32),
                        pltpu.VMEM((nb, nrow, 1), F32), pltpu.VMEM((nb, nrow, 1), F32),
                        pltpu.VMEM((nb, nrow, DH), F32), pltpu.VMEM((nb, LANES, LANES), F32)])
    return pl.pallas_call(
        functools.partial(_fox_decode_body, nb=nb, pps=pps, seq=seq),
        grid_spec=grid_spec,
        out_shape=[jax.ShapeDtypeStruct((n, FOX_W), F32), jax.ShapeDtypeStruct((n, FOX_W), F32),
                   jax.ShapeDtypeStruct((n, LANES), F32)],
        compiler_params=_cparams("arbitrary", "arbitrary"), name="fox_decode",
    )(page_table, p_fox, p_fox, p_fox, p_fl, jnp.tile(q_gain, H_A).reshape(1, FOX_W),
      jnp.tile(k_gain, H_A).reshape(1, FOX_W), bf, ms, ps, pa, asel,
      *([pk] * (nb * pps)), *([pv] * (nb * pps)), *([plf] * (nb * pps)))


def _rw_prep_body(p_ref, init_ref, mu_ref, w0_ref, w2_ref, a0_ref, a2_ref, g2_ref, kk_ref, ka_ref, *rest,
                  tb, first):
    if first:
        r_o, lw_o, k_o, v_o, a_o, b_o, g_o, sh_o, carry_ref = rest
    else:
        v0_ref, v1_ref, v2_ref, vf_ref, r_o, lw_o, k_o, v_o, a_o, b_o, g_o, sh_o, carry_ref = rest

    @pl.when(pl.program_id(1) == 0)
    def _():
        carry_ref[7:8, :] = init_ref[0]

    p = p_ref[...]
    p_prev = _shift_rows(p, 1, carry_ref[...])
    carry_ref[7:8, :] = p[tb - 1:tb, :]
    sh_o[0] = p[tb - 1:tb, :]
    ps = p + (p_prev - p) * mu_ref[...]
    r = ps[:, 0:RW_W]
    k = ps[:, RW_W:2 * RW_W]
    v = ps[:, 2 * RW_W:3 * RW_W]
    x128 = ps[:, 3 * RW_W:3 * RW_W + LANES]
    gd = ps[:, 3 * RW_W + LANES:3 * RW_W + 2 * LANES]
    lane = lax.broadcasted_iota(jnp.int32, (1, LANES), 1)
    xw = jnp.where(lane < DH, jnp.tanh(x128), 0.0).astype(BF16)
    xa = jnp.where(lane < DH, 0.0, x128).astype(BF16)
    w_log = _log_sigmoid(w0_ref[...] + _mm(xw, w2_ref[...])) - 0.5
    lw_o[...] = -jnp.exp(w_log)
    a = _sigmoid(a0_ref[...] + _mm(xa, a2_ref[...]))
    g_o[...] = _mm(_sigmoid(gd).astype(BF16), g2_ref[...])
    if not first:
        gate = _sigmoid(v0_ref[...] + _mm(_mm(v.astype(BF16), v1_ref[...]).astype(BF16), v2_ref[...]))
        v = v + (vf_ref[...] - v) * gate
    kkx = k * kk_ref[...]
    kk = kkx * _head_scale(kkx * kkx, lambda s: lax.rsqrt(s + L2_EPS))
    r_o[...] = r
    k_o[...] = k * (1.0 + (a - 1.0) * ka_ref[...])
    v_o[...] = v
    a_o[...] = -kk
    b_o[...] = kk * a


def _rw_prep(p_rw, shift_prev, w, e, v_first, batch, seq):
    n, cols = p_rw.shape
    tb = min(256, seq)
    nt = seq // tb
    first = e == 0
    pad_rows = lambda m: jnp.concatenate([m, jnp.zeros((LANES - m.shape[0], m.shape[1]), m.dtype)], axis=0)
    w2p = pad_rows(w['rw_w2'][e]).astype(BF16)
    a2p = jnp.concatenate([jnp.zeros((DH, RW_W), F32), w['rw_a2'][e]], axis=0).astype(BF16)
    vec = lambda x: x.reshape(1, -1)
    args = [p_rw, shift_prev.reshape(batch, 1, cols), vec(w['rw_mu'][e]), vec(w['rw_w0'][e]), w2p,
            vec(w['rw_a0'][e]), a2p, w['rw_g2'][e].astype(BF16), vec(w['rw_k_k'][e]), vec(w['rw_k_a'][e])]
    rowspec = pl.BlockSpec((tb, RW_W), lambda b, t: (b * nt + t, 0))
    in_specs = [pl.BlockSpec((tb, cols), lambda b, t: (b * nt + t, 0)),
                pl.BlockSpec((1, 1, cols), lambda b, t: (b, 0, 0)),
                _const_spec((1, cols)), _const_spec((1, RW_W)), _const_spec((LANES, RW_W)),
                _const_spec((1, RW_W)), _const_spec((LANES, RW_W)), _const_spec((LANES, RW_W)),
                _const_spec((1, RW_W)), _const_spec((1, RW_W))]
    if not first:
        v1p = jnp.concatenate([w['rw_v1'][e - 1], jnp.zeros((RW_W, LANES - w['rw_v1'].shape[2]), F32)], axis=1)
        args += [vec(w['rw_v0'][e - 1]), v1p.astype(BF16), pad_rows(w['rw_v2'][e - 1]).astype(BF16), v_first]
        in_specs += [_const_spec((1, RW_W)), _const_spec((RW_W, LANES)), _const_spec((LANES, RW_W)), rowspec]
    outs = pl.pallas_call(
        functools.partial(_rw_prep_body, tb=tb, first=first),
        grid=(batch, nt), in_specs=in_specs,
        out_specs=[rowspec] * 7 + [pl.BlockSpec((1, 1, cols), lambda b, t: (b, 0, 0))],
        out_shape=[jax.ShapeDtypeStruct((n, RW_W), F32)] * 7 + [jax.ShapeDtypeStruct((batch, 1, cols), F32)],
        scratch_shapes=[pltpu.VMEM((8, cols), F32)],
        compiler_params=_cparams("arbitrary", "arbitrary"), name="rw_prep",
    )(*args)
    return outs[:7], outs[7].reshape(batch, cols)


def _pad_chunk(x, rows):
    if x.shape[0] == rows:
        return x
    return jnp.concatenate([x, jnp.zeros((rows - x.shape[0], x.shape[1]), x.dtype)], axis=0)


def _rw_scan_body(r_ref, lw_ref, k_ref, v_ref, a_ref, b_ref, g_ref, rk_ref, lnw_ref, lnb_ref, s0_ref, tri_ref,
                  o_ref, so_ref, s_ref, *, nb, tb):
    @pl.when(pl.program_id(1) == 0)
    def _():
        s_ref[...] = s0_ref[...]

    c = CHUNK
    lane = lax.broadcasted_iota(jnp.int32, (1, LANES), 1)
    m0 = lane < DH
    r2i = lax.broadcasted_iota(jnp.int32, (2 * c, 2 * c), 0)
    c2i = lax.broadcasted_iota(jnp.int32, (2 * c, 2 * c), 1)
    strict = (r2i % c) > (c2i % c)
    lower = (r2i % c) >= (c2i % c)
    tri = tri_ref[...]

    def stack2(z):
        return jnp.concatenate([jnp.where(m0, z, 0.0), jnp.where(m0, 0.0, z)], axis=0)

    chains = [(n, hp, slice(hp * LANES, (hp + 1) * LANES)) for n in range(nb) for hp in range(RW_W // LANES)]
    load = lambda ref: [_pad_chunk(ref[n, :, cs], c) for n, _, cs in chains]
    lws, rs, ks, vs, as_, bs = load(lw_ref), load(r_ref), load(k_ref), load(v_ref), load(a_ref), load(b_ref)
    cums = [_sel_l(tri, lw) for lw in lws]
    a2s = [stack2(a * jnp.exp(cum - lw)).astype(BF16) for a, cum, lw in zip(as_, cums, lws)]
    r2s = [stack2(r * jnp.exp(cum)).astype(BF16) for r, cum in zip(rs, cums)]
    b2s = [stack2(b * jnp.exp(-cum)).astype(BF16) for b, cum in zip(bs, cums)]
    k2s = [stack2(k * jnp.exp(-cum)).astype(BF16) for k, cum in zip(ks, cums)]
    v2s = [stack2(v) for v in vs]
    ars_in = [jnp.concatenate([a2, r2], axis=0) for a2, r2 in zip(a2s, r2s)]
    lms = [_mm_nt(ar, jnp.concatenate([b2, k2], axis=0)) for ar, b2, k2 in zip(ars_in, b2s, k2s)]
    xs = _inv_unit_lower([jnp.where(strict, lm[0:2 * c, 0:2 * c], 0.0) for lm in lms])
    e_ends = [jnp.exp(cum[c - 1:c, :] - cum) for cum in cums]
    xbs = [x.astype(BF16) for x in xs]
    qs = [_mm_tn(xb, stack2(b * e).astype(BF16)) for xb, b, e in zip(xbs, bs, e_ends)]
    wvs = [_mm(jnp.where(strict, lm[0:2 * c, 2 * c:4 * c], 0.0).astype(BF16), v2.astype(BF16))
           for lm, v2 in zip(lms, v2s)]
    ps = [_mm_tn(a2, q.astype(BF16)) for a2, q in zip(a2s, qs)]
    zs = [_mm_tn(jnp.concatenate([wv, v2], axis=0).astype(BF16),
                 jnp.concatenate([q, stack2(k * e)], axis=0).astype(BF16))
          for wv, v2, q, k, e in zip(wvs, v2s, qs, ks, e_ends)]
    sts = [s_ref[n, hp] for n, hp, _ in chains]
    sbs = [st.astype(BF16) for st in sts]
    for (n, hp, _), st, sb, cum, p, z in zip(chains, sts, sbs, cums, ps, zs):
        s_ref[n, hp] = st * jnp.exp(cum[c - 1:c, :]) + _mm(sb, p.astype(BF16)) + z
    arss = [_mm_nt(ar, sb) for ar, sb in zip(ars_in, sbs)]
    u2s = [_mm(xb, (ars[0:2 * c, :] + wv).astype(BF16)) for xb, ars, wv in zip(xbs, arss, wvs)]
    y2s = [ars[2 * c:4 * c, :]
           + _mm(jnp.concatenate([jnp.where(lower, lm[2 * c:4 * c, 0:2 * c], 0.0),
                                  jnp.where(lower, lm[2 * c:4 * c, 2 * c:4 * c], 0.0)], axis=1).astype(BF16),
                 jnp.concatenate([u2, v2], axis=0).astype(BF16))
           for ars, lm, u2, v2 in zip(arss, lms, u2s, v2s)]
    ys = [(y2[0:c, :] + y2[c:2 * c, :])[0:tb, :] for y2 in y2s]
    def head_sum(x):
        s0 = jnp.sum(jnp.where(m0, x, 0.0), axis=-1, keepdims=True)
        return jnp.where(m0, s0, jnp.sum(x, axis=-1, keepdims=True) - s0)

    mus = [head_sum(y) * (1.0 / DH) for y in ys]
    ds = [y - mu for y, mu in zip(ys, mus)]
    vars_ = [head_sum(d * d) * (1.0 / DH) for d in ds]
    bonus = [head_sum(r_ref[n, :, cs] * k_ref[n, :, cs] * rk_ref[:, cs]) for n, _, cs in chains]
    for (n, _, cs), d, var, bo in zip(chains, ds, vars_, bonus):
        yn = d * lax.rsqrt(var + RW_GN_EPS) * lnw_ref[:, cs] + lnb_ref[:, cs]
        o_ref[n, :, cs] = ((yn + bo * v_ref[n, :, cs]) * g_ref[n, :, cs]).astype(o_ref.dtype)
    @pl.when(pl.program_id(1) == pl.num_programs(1) - 1)
    def _():
        so_ref[...] = s_ref[...]


def _rw_scan(parts, w, e, s0_blk, batch, seq):
    nb = 2
    assert batch % nb == 0
    tb = min(CHUNK, seq)
    nt = seq // tb
    npair = RW_W // LANES
    parts = [x.reshape(batch, seq, RW_W) for x in parts]
    tri = jnp.asarray(np.tril(np.ones((CHUNK, CHUNK), np.float32)), BF16)
    rowspec = pl.BlockSpec((nb, tb, RW_W), lambda bb, t: (bb, t, 0))
    stspec = pl.BlockSpec((nb, npair, LANES, LANES), lambda bb, t: (bb, 0, 0, 0))
    cst = lambda shape: pl.BlockSpec(shape, lambda bb, t: (0,) * len(shape))
    o_b, s_out = pl.pallas_call(
        functools.partial(_rw_scan_body, nb=nb, tb=tb),
        grid=(batch // nb, nt),
        in_specs=[rowspec] * 7 + [cst((1, RW_W))] * 3 + [stspec, cst(tri.shape)],
        out_specs=[rowspec, stspec],
        out_shape=[jax.ShapeDtypeStruct((batch, seq, RW_W), _mix_dtype(tb)), jax.ShapeDtypeStruct(s0_blk.shape, F32)],
        scratch_shapes=[pltpu.VMEM((nb, npair, LANES, LANES), F32)],
        compiler_params=_cparams("arbitrary", "arbitrary"), name="rw_scan",
    )(*parts, w['rw_r_k'][e].reshape(1, RW_W), w['rw_ln_w'][e].reshape(1, RW_W),
      w['rw_ln_b'][e].reshape(1, RW_W), s0_blk, tri)
    return o_b.reshape(batch * seq, RW_W), s_out


def _rw_state_to_blocks(s):
    bsz = s.shape[0]
    s = s.reshape(bsz, H_A // 2, 2, DH, DH)
    z = jnp.zeros_like(s[:, :, 0])
    top = jnp.concatenate([s[:, :, 0], z], axis=-1)
    bot = jnp.concatenate([z, s[:, :, 1]], axis=-1)
    return jnp.concatenate([top, bot], axis=-2)


def _rw_blocks_to_state(sb):
    bsz = sb.shape[0]
    return jnp.stack([sb[:, :, :DH, :DH], sb[:, :, DH:, DH:]], axis=2).reshape(bsz, H_A, DH, DH)


def _gdn_body(q_ref, k_ref, v_ref, z_ref, ba_ref, qi_ref, ki_ref, vi_ref, cwq_ref, cwk_ref, cwv_ref,
              par_ref, nw_ref, s0_ref, tri_ref, o_ref, so_ref, qc_ref, kc_ref, vc_ref,
              s_ref, carry_ref, *, nb, tb):
    @pl.when(pl.program_id(1) == 0)
    def _():
        carry_ref[:, 0, 5:8, :] = qi_ref[...]
        carry_ref[:, 1, 5:8, :] = ki_ref[...]
        carry_ref[:, 2, 5:8, :] = vi_ref[...]
        s_ref[...] = s0_ref[...]

    def conv(x, w_ref, n, idx, out_ref):
        prev = carry_ref[n, idx]
        y = x * w_ref[3:4, :]
        for kk in range(1, 4):
            y = y + _shift_rows(x, kk, prev) * w_ref[3 - kk:4 - kk, :]
        tail = x[tb - 3:tb, :]
        carry_ref[n, idx, 5:8, :] = tail
        out_ref[n] = tail
        return y * _sigmoid(y)

    lane = lax.broadcasted_iota(jnp.int32, (1, LANES), 1)

    def column(x, idx):
        return jnp.sum(jnp.where(lane == idx, x, 0.0), axis=-1, keepdims=True)

    c = CHUNK
    r2i = lax.broadcasted_iota(jnp.int32, (2 * c, 2 * c), 0)
    c2i = lax.broadcasted_iota(jnp.int32, (2 * c, 2 * c), 1)
    same = (r2i // c) == (c2i // c)
    strict = same & (r2i > c2i)
    lower = same & (r2i >= c2i)
    tri = tri_ref[...]

    stacked = []
    for n in range(nb):
        q = conv(q_ref[n], cwq_ref, n, 0, qc_ref)
        k = conv(k_ref[n], cwk_ref, n, 1, kc_ref)
        v = conv(v_ref[n], cwv_ref, n, 2, vc_ref)
        ba = ba_ref[n]
        beta_all = _sigmoid(ba)
        z_in = ba + par_ref[1:2, :]
        g_all = -jnp.exp(par_ref[0:1, :]) * (jnp.maximum(z_in, 0.0) + jnp.log1p(jnp.exp(-jnp.abs(z_in))))
        for hp in range(H_C // 2):
            heads = []
            for hh in range(2):
                h = 2 * hp + hh
                sl = slice(h * DK, (h + 1) * DK)
                qh = q[:, sl]
                kh = k[:, sl]
                qn = qh * lax.rsqrt(jnp.sum(qh * qh, axis=-1, keepdims=True) + L2_EPS) * (DK ** -0.5)
                kn = kh * lax.rsqrt(jnp.sum(kh * kh, axis=-1, keepdims=True) + L2_EPS)
                beta = jnp.broadcast_to(column(beta_all, h), (tb, DK))
                g = jnp.broadcast_to(column(g_all, H_C + h), (tb, DK))
                heads.append((qn, kn, v[:, sl], beta, g))
            stacked.append([jnp.concatenate([_pad_chunk(heads[0][i], c), _pad_chunk(heads[1][i], c)], axis=0)
                            for i in range(5)])
    chains = [(n, hp) for n in range(nb) for hp in range(H_C // 2)]
    qn2s, kn2s, v2s, b2s, g2s = [[s[i] for s in stacked] for i in range(5)]
    gcs = [_sel_l(tri, g2) for g2 in g2s]
    grows = [jnp.transpose(gc) for gc in gcs]
    gammas = [jnp.where(lower, jnp.exp(jnp.where(lower, gc - grow, 0.0)), 0.0) for gc, grow in zip(gcs, grows)]
    kbs = [kn2 * b2 for kn2, b2 in zip(kn2s, b2s)]
    kkqks = [_mm_nt(jnp.concatenate([kb, qn2], axis=0).astype(BF16), kn2.astype(BF16))
             for kb, qn2, kn2 in zip(kbs, qn2s, kn2s)]
    tinvs = _inv_unit_lower([jnp.where(strict, -kkqk[0:2 * c, :] * gamma, 0.0) for kkqk, gamma in zip(kkqks, gammas)])
    egs = [jnp.exp(gc) for gc in gcs]
    uws = [_mm(tinv.astype(BF16), jnp.concatenate([kb * eg, v2 * b2], axis=1).astype(BF16))
           for tinv, kb, eg, v2, b2 in zip(tinvs, kbs, egs, v2s, b2s)]
    heads2 = [(ci, hh) for ci in range(len(chains)) for hh in range(2)]
    rows = lambda hh: slice(hh * c, (hh + 1) * c)
    glasts = [gcs[ci][hh * c + c - 1:hh * c + c, :] for ci, hh in heads2]
    pzs = [_mm_tn((kn2s[ci][rows(hh), :] * jnp.exp(gl - gcs[ci][rows(hh), :])).astype(BF16),
                  uws[ci][rows(hh), :].astype(BF16))
           for (ci, hh), gl in zip(heads2, glasts)]
    sts = [s_ref[chains[ci][0], 2 * chains[ci][1] + hh] for ci, hh in heads2]
    sbs = [st.astype(BF16) for st in sts]
    for (ci, hh), st, sb, gl, pz in zip(heads2, sts, sbs, glasts, pzs):
        s_ref[chains[ci][0], 2 * chains[ci][1] + hh] = (st * jnp.exp(gl[:, 0:1])
                                                        - _mm(pz[:, 0:DK].astype(BF16), sb) + pz[:, DK:2 * DK])
    wqs = [_mm(jnp.concatenate([uws[ci][rows(hh), 0:DK], (qn2s[ci] * egs[ci])[rows(hh), :]], axis=0).astype(BF16), sb)
           for (ci, hh), sb in zip(heads2, sbs)]
    o2s = []
    for ci in range(len(chains)):
        vnew = jnp.concatenate([uws[ci][rows(hh), DK:2 * DK] - wqs[2 * ci + hh][0:c, :] for hh in range(2)], axis=0)
        qs = jnp.concatenate([wqs[2 * ci + hh][c:2 * c, :] for hh in range(2)], axis=0)
        amat = jnp.where(lower, kkqks[ci][2 * c:4 * c, :] * gammas[ci], 0.0)
        o2s.append(qs + _mm(amat.astype(BF16), vnew.astype(BF16)))
    for ci, (n, hp) in enumerate(chains):
        for hh in range(2):
            sl = slice((2 * hp + hh) * DK, (2 * hp + hh + 1) * DK)
            oh = o2s[ci][hh * c:hh * c + tb, :]
            zz = z_ref[n, :, sl]
            on = oh * lax.rsqrt(jnp.mean(oh * oh, axis=-1, keepdims=True) + EPS) * nw_ref[...]
            o_ref[n, :, sl] = (on * (zz * _sigmoid(zz))).astype(o_ref.dtype)
    @pl.when(pl.program_id(1) == pl.num_programs(1) - 1)
    def _():
        so_ref[...] = s_ref[...]


def _gdn(p_qkv, p_z, p_ba, conv_prev, s0, w, o_idx, batch, seq):
    nb = 2
    assert batch % nb == 0 and seq >= 3
    tb = min(CHUNK, seq)
    nt = seq // tb
    gk = H_C * DK
    cw = w['gdn_conv_w'][o_idx]
    par = jnp.concatenate([_pad_lanes(w['gdn_A_log'][o_idx], H_C), _pad_lanes(w['gdn_dt_bias'][o_idx], H_C),
                           jnp.zeros((6, LANES), F32)], axis=0)
    idx = np.arange(2 * CHUNK)
    tri = jnp.asarray((((idx[:, None] // CHUNK) == (idx[None, :] // CHUNK))
                       & (idx[:, None] >= idx[None, :])).astype(np.float32), BF16)
    p_qkv = p_qkv.reshape(batch, seq, 3 * gk)
    col = lambda part: pl.BlockSpec((nb, tb, gk), lambda b, t, part=part: (b, t, part))
    prev = lambda part: pl.BlockSpec((nb, 3, gk), lambda b, t, part=part: (b, 0, part))
    wspec = lambda part: pl.BlockSpec((4, gk), lambda b, t, part=part: (0, part))
    cst = lambda shape: pl.BlockSpec(shape, lambda b, t: (0,) * len(shape))
    stspec = pl.BlockSpec((nb, H_C, DK, DK), lambda b, t: (b, 0, 0, 0))
    outs = pl.pallas_call(
        functools.partial(_gdn_body, nb=nb, tb=tb),
        grid=(batch // nb, nt),
        in_specs=[col(0), col(1), col(2), col(0),
                  pl.BlockSpec((nb, tb, LANES), lambda b, t: (b, t, 0)),
                  prev(0), prev(1), prev(2), wspec(0), wspec(1), wspec(2),
                  cst((8, LANES)), cst((1, DK)), stspec, cst(tri.shape)],
        out_specs=[col(0), stspec, prev(0), prev(0), prev(0)],
        out_shape=[jax.ShapeDtypeStruct((batch, seq, gk), _mix_dtype(tb)), jax.ShapeDtypeStruct(s0.shape, F32)]
                  + [jax.ShapeDtypeStruct((batch, 3, gk), F32)] * 3,
        scratch_shapes=[pltpu.VMEM((nb, H_C, DK, DK), F32), pltpu.VMEM((nb, 3, 8, gk), F32)],
        compiler_params=_cparams("arbitrary", "arbitrary"), name="gdn",
    )(p_qkv, p_qkv, p_qkv, p_z.reshape(batch, seq, gk), p_ba.reshape(batch, seq, LANES),
      conv_prev, conv_prev, conv_prev, cw, cw, cw, par, w['gdn_norm_w'][o_idx].reshape(1, DK), s0, tri)
    o_c, s_out, qc, kc, vc = outs
    return o_c.reshape(batch * seq, gk), jnp.concatenate([qc, kc, vc], axis=-1), s_out


def _run_trunk(x, fox_past, rw_state, rw_shift, gdn_state, gdn_conv, ffn_conv, page_table, w, wb):
    batch, seq, d = x.shape
    n = batch * seq
    depth = w['norm_mix'].shape[0]
    tm = 512 if n % 512 == 0 else n
    x = x.reshape(n, d)
    fk, fv, flf, rws, rwsh, gs, gcv, fcv = [], [], [], [], [], [], [], []
    v_first = None
    for layer in range(depth):
        if layer % 2 == 0:
            e = layer // 2
            p_fox, p_rw, p_fl = _norm_matmul(x, w['norm_mix'][layer], wb['ev_in'], e, (4 * FOX_W, 1792, LANES), tm)
            if fox_past is None:
                q_aug, k_aug, v_bf, k_out, lf = _fox_prep(p_fox, p_fl, w['fox_q_gain'][e], w['fox_k_gain'][e],
                                                          w['fox_b_f'][e], batch, seq)
                o_attn = _fox_attn(q_aug, k_aug, v_bf, batch, seq)
            else:
                o_attn, k_out, lf = _fox_decode(p_fox, p_fl, w['fox_q_gain'][e], w['fox_k_gain'][e], w['fox_b_f'][e],
                                                fox_past[0], fox_past[1], fox_past[2], page_table, e, batch, seq)
            parts, sh = _rw_prep(p_rw, rw_shift[e], w, e, v_first, batch, seq)
            if e == 0:
                v_first = parts[3]
            o_b, s_blk = _rw_scan(parts, w, e, _rw_state_to_blocks(rw_state[e]), batch, seq)
            x = _ev_out(o_attn, p_fox, o_b, wb['ev_out'], e, x, tm)
            fk.append(k_out.reshape(batch, seq, H_A, DH))
            fv.append(p_fox[:, 2 * FOX_W:3 * FOX_W].reshape(batch, seq, H_A, DH))
            flf.append(lf[:, :H_A].reshape(batch, seq, H_A))
            rws.append(_rw_blocks_to_state(s_blk))
            rwsh.append(sh)
        else:
            o_idx = layer // 2
            p_qkv, p_z, p_ba = _norm_matmul(x, w['norm_mix'][layer], wb['od_in'], o_idx,
                                            (3 * H_C * DK, H_C * DK, LANES), tm)
            o_c, cv, s_out = _gdn(p_qkv, p_z, p_ba, gdn_conv[o_idx], gdn_state[o_idx], w, o_idx, batch, seq)
            x = _od_out(o_c, wb['od_out'], o_idx, x, tm)
            gs.append(s_out)
            gcv.append(cv)
        x, buf = _ffn(x, w['norm_ffn'][layer], wb['ffn_up'], w['ffn_conv_w'][layer],
                      w['ffn_conv_b'][layer], wb['ffn_down'], layer, ffn_conv[layer], batch, seq)
        fcv.append(buf)
    y = _final_norm(x, w['norm_out'], tm).reshape(batch, seq, d)
    return (y, jnp.stack(fk), jnp.stack(fv), jnp.stack(flf), jnp.stack(rws), jnp.stack(rwsh),
            jnp.stack(gs), jnp.stack(gcv), jnp.stack(fcv))


def _prep_weights(w):
    fox_cols = 4 * FOX_W + H_A
    ev = w['ev_w_in']
    pad = jnp.zeros(ev.shape[:2] + (LANES - H_A,), ev.dtype)
    ev_in = jnp.concatenate([ev[..., :4 * FOX_W], ev[..., fox_cols:], ev[..., 4 * FOX_W:fox_cols], pad], axis=-1)
    od = w['od_w_in']
    pad2 = jnp.zeros(od.shape[:2] + (LANES - 2 * H_C,), od.dtype)
    od_in = jnp.concatenate([od, pad2], axis=-1)
    return dict(ev_in=ev_in.astype(BF16), ev_out=w['ev_w_out'].astype(BF16), od_in=od_in.astype(BF16),
                od_out=w['od_w_out'].astype(BF16), ffn_up=w['ffn_w_up'].astype(BF16),
                ffn_down=w['ffn_w_down'].astype(BF16))


def kernel(x_prompt, x_sample, cache_fox_k, cache_fox_v, cache_fox_logf, state_rwkv, state_rwkv_shift, state_gdn, state_gdn_conv, state_ffn_conv, page_table, norm_mix, norm_ffn, norm_out, ev_w_in, ev_w_out, fox_b_f, fox_q_gain, fox_k_gain, rw_mu, rw_w0, rw_w2, rw_a0, rw_a2, rw_g2, rw_k_k, rw_k_a, rw_r_k, rw_ln_w, rw_ln_b, rw_v0, rw_v1, rw_v2, od_w_in, od_w_out, gdn_conv_w, gdn_A_log, gdn_dt_bias, gdn_norm_w, ffn_w_up, ffn_conv_w, ffn_conv_b, ffn_w_down):
    w = dict(norm_mix=norm_mix, norm_ffn=norm_ffn, norm_out=norm_out, ev_w_in=ev_w_in, ev_w_out=ev_w_out,
             fox_b_f=fox_b_f, fox_q_gain=fox_q_gain, fox_k_gain=fox_k_gain, rw_mu=rw_mu, rw_w0=rw_w0,
             rw_w2=rw_w2, rw_a0=rw_a0, rw_a2=rw_a2, rw_g2=rw_g2, rw_k_k=rw_k_k, rw_k_a=rw_k_a, rw_r_k=rw_r_k,
             rw_ln_w=rw_ln_w, rw_ln_b=rw_ln_b, rw_v0=rw_v0, rw_v1=rw_v1, rw_v2=rw_v2, od_w_in=od_w_in,
             od_w_out=od_w_out, gdn_conv_w=gdn_conv_w, gdn_A_log=gdn_A_log, gdn_dt_bias=gdn_dt_bias,
             gdn_norm_w=gdn_norm_w, ffn_w_up=ffn_w_up, ffn_conv_w=ffn_conv_w, ffn_conv_b=ffn_conv_b,
             ffn_w_down=ffn_w_down)
    wb = _prep_weights(w)
    bp = x_prompt.shape[0]
    n_even, n_odd, depth = ev_w_in.shape[0], od_w_in.shape[0], norm_mix.shape[0]
    rw_cols = rw_mu.shape[1]
    (y_p, fk_p, fv_p, flf_p, rw_p, rwsh_p, gdn_p, gcv_p, fcv_p) = _run_trunk(
        x_prompt, None,
        jnp.zeros((n_even, bp, H_A, DH, DH), F32), jnp.zeros((n_even, bp, rw_cols), F32),
        jnp.zeros((n_odd, bp, H_C, DK, DK), F32), jnp.zeros((n_odd, bp, 3, 3 * H_C * DK), F32),
        jnp.zeros((depth, bp, 2, ffn_w_up.shape[2]), F32), page_table, w, wb)
    (y_s, fk_s, fv_s, flf_s, rw_s, rwsh_s, gdn_s, gcv_s, fcv_s) = _run_trunk(
        x_sample, (cache_fox_k, cache_fox_v, cache_fox_logf), state_rwkv, state_rwkv_shift,
        state_gdn, state_gdn_conv, state_ffn_conv, page_table, w, wb)
    return (y_p, y_s, fk_p, fv_p, flf_p, fk_s, fv_s, flf_s, rw_p, rw_s, rwsh_p, rwsh_s,
            gdn_p, gdn_s, gcv_p, gcv_s, fcv_p, fcv_s)
```
